```python
import math
import jax, jax.numpy as jnp
from jax import lax
import numpy as np

D_MODEL = 2048
BATCH = 4
SEQ = 2048
DEPTH = 2
DEC_BATCH = 32
DEC_SEQ = 16
PAST_LEN = 1024

CHUNK = 64
EPS = 1e-6
NEG_INF = -1e30
D_SSM = 2048
SSM_HEAD_DIM = 64
SSM_HEADS = D_SSM // SSM_HEAD_DIM
SSM_GROUPS = 4
SSM_HPG = SSM_HEADS // SSM_GROUPS
SSM_STATE = 128
CONV_W = 4
CONV_DIM = D_SSM + 2 * SSM_GROUPS * SSM_STATE
RET_HEADS = 8
RET_DK = 128
RET_DV = 256
ROPE_BASE = 10000.0
ATT_HEADS = 16
ATT_HEAD_DIM = 128
BAND_CHUNKS = 8
WINDOW = BAND_CHUNKS * CHUNK
REL_CLIP = 256
FFN_HIDDEN = -(-8 * D_MODEL // (3 * 256)) * 256
IN_SIZES = (D_SSM, CONV_DIM, SSM_HEADS,
            RET_HEADS * RET_DK, RET_HEADS * RET_DK, RET_HEADS * RET_DV, RET_HEADS * RET_DV,
            ATT_HEADS * ATT_HEAD_DIM, ATT_HEADS * ATT_HEAD_DIM, ATT_HEADS * ATT_HEAD_DIM,
            D_MODEL, D_MODEL, D_MODEL)
N_IN = sum(IN_SIZES)

kernel_name = 'hybrid_streaming_encoder_step'

F32 = jnp.float32


def _normal(k, shape, scale):
    return scale * jax.random.normal(k, shape, F32)


def _rmsnorm(x, g):
    xf = x.astype(F32)
    y = xf * lax.rsqrt(jnp.mean(xf * xf, axis=-1, keepdims=True) + EPS)
    return (y * g.astype(F32)).astype(x.dtype)


def _head_rms(x, g=None):
    xf = x.astype(F32)
    y = xf * lax.rsqrt(jnp.mean(xf * xf, axis=-1, keepdims=True) + EPS)
    if g is not None:
        y = y * g.astype(F32)
    return y.astype(x.dtype)


def _rotary(x, pos):
    d = x.shape[-1]
    half = d // 2
    inv = ROPE_BASE ** (-jnp.arange(half, dtype=F32) * 2.0 / d)
    ang = pos.astype(F32)[:, None] * inv[None, :]
    cos = jnp.cos(ang)[None, :, None, :]
    sin = jnp.sin(ang)[None, :, None, :]
    xf = x.astype(F32)
    x1, x2 = xf[..., :half], xf[..., half:]
    return jnp.concatenate([x1 * cos - x2 * sin, x2 * cos + x1 * sin], axis=-1).astype(x.dtype)


def _causal_conv(xbc, hist, w, b):
    xpad = jnp.concatenate([hist.astype(xbc.dtype), xbc], axis=1)
    y = lax.conv_general_dilated(xpad, w[:, None, :].astype(xpad.dtype), window_strides=(1,),
                                 padding='VALID', dimension_numbers=('NWC', 'WIO', 'NWC'),
                                 feature_group_count=CONV_DIM)
    return jax.nn.silu(y + b.astype(y.dtype)), xpad[:, -(CONV_W - 1):]


def _ssd(x, dt, a, bm, cm, d_skip, h0):
    b, T = x.shape[:2]
    L = min(T, CHUNK)
    nc = T // L
    xc = x.astype(F32).reshape(b, nc, L, SSM_GROUPS, SSM_HPG, SSM_HEAD_DIM)
    dtc = dt.astype(F32).reshape(b, nc, L, SSM_GROUPS, SSM_HPG)
    bc = bm.astype(F32).reshape(b, nc, L, SSM_GROUPS, SSM_STATE)
    cc = cm.astype(F32).reshape(b, nc, L, SSM_GROUPS, SSM_STATE)
    ag = a.astype(F32).reshape(SSM_GROUPS, SSM_HPG)
    acum = jnp.moveaxis(jnp.cumsum(dtc * ag, axis=2), 2, -1)
    dtt = jnp.moveaxis(dtc, 2, -1)
    tri = jnp.tril(jnp.ones((L, L), bool))
    seg = acum[..., :, None] - acum[..., None, :]
    decay = jnp.where(tri, jnp.exp(jnp.where(tri, seg, 0.0)), 0.0)
    cb = jnp.einsum('bcign,bcjgn->bcgij', cc, bc)
    y = jnp.einsum('bcgkij,bcjgkp->bcigkp', cb[:, :, :, None] * decay * dtt[..., None, :], xc)
    to_end = jnp.exp(acum[..., -1:] - acum) * dtt
    states = jnp.einsum('bcjgn,bcgkj,bcjgkp->bcgkpn', bc, to_end, xc)
    chunk_decay = jnp.exp(acum[..., -1])

    def step(h, inp):
        dec, st = inp
        return dec[..., None, None] * h + st, h

    h_init = h0.astype(F32).reshape(b, SSM_GROUPS, SSM_HPG, SSM_HEAD_DIM, SSM_STATE)
    h_last, h_prev = lax.scan(step, h_init, (jnp.moveaxis(chunk_decay, 1, 0), jnp.moveaxis(states, 1, 0)))
    h_prev = jnp.moveaxis(h_prev, 0, 1)
    y = y + jnp.einsum('bcign,bcgkpn,bcgki->bcigkp', cc, h_prev, jnp.exp(acum))
    y = y + d_skip.astype(F32).reshape(SSM_GROUPS, SSM_HPG)[:, :, None] * xc
    return y.reshape(b, T, D_SSM), h_last.reshape(b, SSM_HEADS, SSM_HEAD_DIM, SSM_STATE)


def _gated_group_rms(y, z, w):
    b, T, _ = y.shape
    t = y.astype(F32) * jax.nn.silu(z.astype(F32))
    t = t.reshape(b, T, SSM_GROUPS, D_SSM // SSM_GROUPS)
    t = t * lax.rsqrt(jnp.mean(t * t, axis=-1, keepdims=True) + EPS)
    return (t.reshape(b, T, D_SSM) * w.astype(F32)).astype(z.dtype)


def _retention(q, k, v, s0):
    b, T, h, dk = q.shape
    dv = v.shape[-1]
    L = min(T, CHUNK)
    nc = T // L
    qc = q.astype(F32).reshape(b, nc, L, h, dk)
    kc = k.astype(F32).reshape(b, nc, L, h, dk)
    vc = v.astype(F32).reshape(b, nc, L, h, dv)
    lam = jnp.log1p(-jnp.exp2(-5.0 - jnp.arange(h, dtype=F32)))
    i = jnp.arange(L, dtype=F32)
    diff = i[:, None] - i[None, :]
    dmat = jnp.where(diff >= 0, jnp.exp(lam[:, None, None] * jnp.maximum(diff, 0.0)), 0.0)
    s = jnp.einsum('bcihd,bcjhd->bchij', qc, kc) * dmat
    o = jnp.einsum('bchij,bcjhv->bcihv', s, vc)
    to_end = jnp.exp(lam[:, None] * (L - 1 - i)[None, :])
    kv = jnp.einsum('bcjhd,hj,bcjhv->bchdv', kc, to_end, vc)
    chunk_decay = jnp.exp(lam * L)

    def step(state, kvc):
        return chunk_decay[None, :, None, None] * state + kvc, state

    s_last, s_prev = lax.scan(step, s0.astype(F32), jnp.moveaxis(kv, 1, 0))
    s_prev = jnp.moveaxis(s_prev, 0, 1)
    o = o + jnp.einsum('bcihd,hi,bchdv->bcihv', qc, jnp.exp(lam[:, None] * (i + 1.0)[None, :]), s_prev)
    return o.reshape(b, T, h, dv), s_last


def _band_attention(q, kb, vb, valid, off, rel_bias):
    b, nc, cq = q.shape[:3]
    kbn = kb.shape[2]
    rel = off + jnp.arange(cq)[:, None] - jnp.arange(kbn)[None, :]
    bias = rel_bias.astype(F32)[:, jnp.clip(rel, -REL_CLIP, REL_CLIP) + REL_CLIP]
    s = jnp.einsum('bcqhd,bckhd->bchqk', q, kb).astype(F32) * (ATT_HEAD_DIM ** -0.5) + bias
    s = jnp.where(valid[None, :, None, None, :], s, NEG_INF)
    pr = jax.nn.softmax(s, axis=-1).astype(vb.dtype)
    o = jnp.einsum('bchqk,bckhd->bcqhd', pr, vb)
    return o.reshape(b, nc * cq, ATT_HEADS * ATT_HEAD_DIM)


def _layer(x, pos, p, conv_hist, ssm_h0, ret_s0, kv_hist):
    b, T, _ = x.shape
    hn = _rmsnorm(x, p['norm_mix'])
    u = hn @ p['w_in']
    offsets = np.cumsum(IN_SIZES)[:-1].tolist()
    (z, xbc, dt_raw, rq, rk, rv, rg, aq, ak, av, ga, gb, gc) = jnp.split(u, offsets, axis=-1)

    xbc, conv_state = _causal_conv(xbc, conv_hist, p['conv_w'], p['conv_b'])
    xs, bm, cm = jnp.split(xbc, [D_SSM, D_SSM + SSM_GROUPS * SSM_STATE], axis=-1)
    dt = jax.nn.softplus(dt_raw.astype(F32) + p['dt_bias'].astype(F32))
    a = -jnp.exp(p['a_log'].astype(F32))
    y_ssm, ssm_state = _ssd(xs.reshape(b, T, SSM_HEADS, SSM_HEAD_DIM), dt, a,
                           bm.reshape(b, T, SSM_GROUPS, SSM_STATE), cm.reshape(b, T, SSM_GROUPS, SSM_STATE),
                           p['d_skip'], ssm_h0)
    ys = _gated_group_rms(y_ssm, z, p['ssm_norm'])

    q_r = _rotary(rq.reshape(b, T, RET_HEADS, RET_DK), pos)
    k_r = _rotary(rk.reshape(b, T, RET_HEADS, RET_DK), pos) * (RET_DK ** -0.5)
    o_r, ret_state = _retention(q_r, k_r, rv.reshape(b, T, RET_HEADS, RET_DV), ret_s0)
    yr = (_head_rms(o_r).reshape(b, T, RET_HEADS * RET_DV) * jax.nn.silu(rg.astype(F32))).astype(x.dtype)

    q_a = _head_rms(aq.reshape(b, T, ATT_HEADS, ATT_HEAD_DIM), p['q_norm'])
    k_a = _head_rms(ak.reshape(b, T, ATT_HEADS, ATT_HEAD_DIM), p['k_norm'])
    v_a = av.reshape(b, T, ATT_HEADS, ATT_HEAD_DIM)
    if kv_hist is None:
        nc = T // CHUNK
        idx = jnp.arange(nc)[:, None] + jnp.arange(BAND_CHUNKS + 1)[None, :]

        def gather_band(t):
            tc = t.reshape(b, nc, CHUNK, ATT_HEADS, ATT_HEAD_DIM)
            tp = jnp.concatenate([jnp.zeros((b, BAND_CHUNKS) + tc.shape[2:], tc.dtype), tc], axis=1)
            return tp[:, idx].reshape(b, nc, (BAND_CHUNKS + 1) * CHUNK, ATT_HEADS, ATT_HEAD_DIM)

        valid = jnp.repeat(idx >= BAND_CHUNKS, CHUNK, axis=1)
        ya = _band_attention(q_a.reshape(b, nc, CHUNK, ATT_HEADS, ATT_HEAD_DIM), gather_band(k_a),
                             gather_band(v_a), valid, BAND_CHUNKS * CHUNK, p['rel_bias'])
        keep = min(WINDOW, T)
        k_state = k_a[:, T - keep:]
        v_state = v_a[:, T - keep:]
    else:
        k_hist, v_hist = kv_hist
        kb = jnp.concatenate([k_hist.astype(k_a.dtype), k_a], axis=1)[:, None]
        vb = jnp.concatenate([v_hist.astype(v_a.dtype), v_a], axis=1)[:, None]
        valid = jnp.ones((1, kb.shape[2]), bool)
        ya = _band_attention(q_a[:, None], kb, vb, valid, k_hist.shape[1], p['rel_bias'])
        k_state = k_a
        v_state = v_a

    m = (jax.nn.sigmoid(ga.astype(F32)) * (ys @ p['w_br_ssm'])
         + jax.nn.sigmoid(gb.astype(F32)) * (yr @ p['w_br_ret'])
         + jax.nn.sigmoid(gc.astype(F32)) * (ya @ p['w_br_att']))
    x = x + (m.astype(x.dtype) @ p['w_out'])

    h2 = _rmsnorm(x, p['norm_ffn'])
    fa, fc = jnp.split(h2 @ p['w_ffn_in'], 2, axis=-1)
    x = x + (jax.nn.silu(fa) * fc) @ p['w_ffn_out']
    return x, (k_state, v_state, ret_state, ssm_state, conv_state)


def setup_inputs(seed: int = 0) -> dict:
    key = jax.random.key(seed)
    ks = jax.random.split(key, 26)
    lc = min(WINDOW, PAST_LEN)
    dt0 = jnp.exp(jax.random.uniform(ks[9], (DEPTH, SSM_HEADS), F32)
                  * (math.log(0.1) - math.log(0.001)) + math.log(0.001))
    return {
        'x_prompt': _normal(ks[0], (BATCH, SEQ, D_MODEL), 1.0),
        'x_sample': _normal(ks[1], (DEC_BATCH, DEC_SEQ, D_MODEL), 1.0),
        'cache_attn_k': _normal(ks[2], (DEPTH, DEC_BATCH, lc, ATT_HEADS, ATT_HEAD_DIM), 1.0),
        'cache_attn_v': _normal(ks[3], (DEPTH, DEC_BATCH, lc, ATT_HEADS, ATT_HEAD_DIM), 1.0),
        'state_ret': _normal(ks[4], (DEPTH, DEC_BATCH, RET_HEADS, RET_DK, RET_DV), 1.0),
        'state_ssm': _normal(ks[5], (DEPTH, DEC_BATCH, SSM_HEADS, SSM_HEAD_DIM, SSM_STATE), 0.1),
        'state_conv': _normal(ks[6], (DEPTH, DEC_BATCH, CONV_W - 1, CONV_DIM), 1.0),
        'norm_mix': 1.0 + _normal(ks[7], (DEPTH, D_MODEL), 0.01),
        'w_in': _normal(ks[8], (DEPTH, D_MODEL, N_IN), D_MODEL ** -0.5),
        'conv_w': _normal(ks[10], (DEPTH, CONV_W, CONV_DIM), 0.5),
        'conv_b': _normal(ks[11], (DEPTH, CONV_DIM), 0.02),
        'dt_bias': dt0 + jnp.log(-jnp.expm1(-dt0)),
        'a_log': jnp.log(jax.random.uniform(ks[12], (DEPTH, SSM_HEADS), F32, 1.0, 16.0)),
        'd_skip': 1.0 + _normal(ks[13], (DEPTH, SSM_HEADS), 0.01),
        'ssm_norm': 1.0 + _normal(ks[14], (DEPTH, D_SSM), 0.01),
        'q_norm': 1.0 + _normal(ks[15], (DEPTH, ATT_HEAD_DIM), 0.01),
        'k_norm': 1.0 + _normal(ks[16], (DEPTH, ATT_HEAD_DIM), 0.01),
        'rel_bias': _normal(ks[17], (DEPTH, ATT_HEADS, 2 * REL_CLIP + 1), 0.1),
        'w_br_ssm': _normal(ks[18], (DEPTH, D_SSM, D_MODEL), D_SSM ** -0.5),
        'w_br_ret': _normal(ks[19], (DEPTH, RET_HEADS * RET_DV, D_MODEL), (RET_HEADS * RET_DV) ** -0.5),
        'w_br_att': _normal(ks[20], (DEPTH, ATT_HEADS * ATT_HEAD_DIM, D_MODEL), (ATT_HEADS * ATT_HEAD_DIM) ** -0.5),
        'w_out': _normal(ks[21], (DEPTH, D_MODEL, D_MODEL), D_MODEL ** -0.5),
        'norm_ffn': 1.0 + _normal(ks[22], (DEPTH, D_MODEL), 0.01),
        'w_ffn_in': _normal(ks[23], (DEPTH, D_MODEL, 2 * FFN_HIDDEN), D_MODEL ** -0.5),
        'w_ffn_out': _normal(ks[24], (DEPTH, FFN_HIDDEN, D_MODEL), FFN_HIDDEN ** -0.5),
    }


def reference(x_prompt, x_sample, cache_attn_k, cache_attn_v, state_ret, state_ssm, state_conv,
              norm_mix, w_in, conv_w, conv_b, dt_bias, a_log, d_skip, ssm_norm, q_norm, k_norm,
              rel_bias, w_br_ssm, w_br_ret, w_br_att, w_out, norm_ffn, w_ffn_in, w_ffn_out):
    bp, tp = x_prompt.shape[:2]
    ts = x_sample.shape[1]
    pos_p = jnp.arange(tp)
    pos_s = PAST_LEN + jnp.arange(ts)
    conv0 = jnp.zeros((bp, CONV_W - 1, CONV_DIM), x_prompt.dtype)
    ssm0 = jnp.zeros((bp, SSM_HEADS, SSM_HEAD_DIM, SSM_STATE), F32)
    ret0 = jnp.zeros((bp, RET_HEADS, RET_DK, RET_DV), F32)
    yp, ys = x_prompt, x_sample
    acc_p = ([], [], [], [], [])
    acc_s = ([], [], [], [], [])
    for l in range(DEPTH):
        p = {'norm_mix': norm_mix[l], 'w_in': w_in[l], 'conv_w': conv_w[l], 'conv_b': conv_b[l],
             'dt_bias': dt_bias[l], 'a_log': a_log[l], 'd_skip': d_skip[l], 'ssm_norm': ssm_norm[l],
             'q_norm': q_norm[l], 'k_norm': k_norm[l], 'rel_bias': rel_bias[l],
             'w_br_ssm': w_br_ssm[l], 'w_br_ret': w_br_ret[l], 'w_br_att': w_br_att[l],
             'w_out': w_out[l], 'norm_ffn': norm_ffn[l], 'w_ffn_in': w_ffn_in[l], 'w_ffn_out': w_ffn_out[l]}
        yp, sp = _layer(yp, pos_p, p, conv0, ssm0, ret0, None)
        ys, ss = _layer(ys, pos_s, p, state_conv[l], state_ssm[l], state_ret[l],
                        (cache_attn_k[l], cache_attn_v[l]))
        for i in range(5):
            acc_p[i].append(sp[i])
            acc_s[i].append(ss[i])
    pk, pv, pr, pssm, pconv = [jnp.stack(a) for a in acc_p]
    sk, sv, sr, sssm, sconv = [jnp.stack(a) for a in acc_s]
    return (yp, ys, pk, pv, pr, pssm, pconv, sk, sv, sr, sssm, sconv)
```

```python
import functools
import math

import numpy as np
import jax
import jax.numpy as jnp
from jax import lax
from jax.experimental import pallas as pl
from jax.experimental.pallas import tpu as pltpu

F32 = jnp.float32
BF16 = jnp.bfloat16

D_MODEL = 2048
BATCH = 4
SEQ = 2048
DEPTH = 2
DEC_BATCH = 32
DEC_SEQ = 16
PAST_LEN = 1024
CHUNK = 64
EPS = 1e-6
NEG_INF = -1e30

D_SSM = 2048
SSM_HEAD_DIM = 64
SSM_HEADS = 32
SSM_GROUPS = 4
SSM_STATE = 128
SSM_GROUP_WIDTH = D_SSM // SSM_GROUPS
CONV_W = 4
CONV_DIM = D_SSM + 2 * SSM_GROUPS * SSM_STATE
BC_WIDTH = SSM_GROUPS * SSM_STATE

RET_HEADS = 8
RET_DK = 128
RET_DV = 256
ROPE_BASE = 10000.0

ATT_HEADS = 16
ATT_HEAD_DIM = 128
BAND_CHUNKS = 8
WINDOW = BAND_CHUNKS * CHUNK
REL_CLIP = 256

FFN_HIDDEN = 5632

M_PROMPT = BATCH * SEQ
M_SAMPLE = DEC_BATCH * DEC_SEQ
M_ALL = M_PROMPT + M_SAMPLE

LANES = 128
SUBLANES = 8
V7X_VMEM_LIMIT_BYTES = 56 * 1024 * 1024

U_Z, U_X, U_RV, U_RG, U_AQ, U_AK, U_AV, U_GA, U_GB, U_GC = (i * 2048 for i in range(10))
U_RQ = 20480
U_RK = 21504
U_B = 22528
U_C = 23040
U_WIDTH = 23552

_IN_SIZES = (D_SSM, CONV_DIM, SSM_HEADS, 1024, 1024, 2048, 2048, 2048, 2048, 2048, 2048, 2048, 2048)
_IN_OFF = np.concatenate([[0], np.cumsum(_IN_SIZES)]).tolist()


def _params(sem):
    return pltpu.CompilerParams(dimension_semantics=sem, vmem_limit_bytes=V7X_VMEM_LIMIT_BYTES)


def _sigmoid(x):
    return 1.0 / (1.0 + jnp.exp(-x))


def _silu(x):
    return x * _sigmoid(x)


def _rmsnorm_kernel(x_ref, g_ref, o_ref):
    x = x_ref[...]
    ms = jnp.mean(x * x, axis=-1, keepdims=True)
    o_ref[...] = (x * lax.rsqrt(ms + EPS) * g_ref[...]).astype(o_ref.dtype)


def _rmsnorm(x, g, tm=544):
    m, d = x.shape
    return pl.pallas_call(
        _rmsnorm_kernel,
        grid=(m // tm,),
        in_specs=[pl.BlockSpec((tm, d), lambda i: (i, 0)),
                  pl.BlockSpec((1, d), lambda i: (0, 0))],
        out_specs=pl.BlockSpec((tm, d), lambda i: (i, 0)),
        out_shape=jax.ShapeDtypeStruct((m, d), BF16),
        compiler_params=_params(("parallel",)),
        name="rmsnorm",
    )(x, g.reshape(1, d))


def _mm_kernel(*refs, nk, has_gate, has_add):
    a_ref, w_ref = refs[0], refs[1]
    pos = 2
    gate_ref = add_ref = None
    if has_gate:
        gate_ref = refs[pos]
        pos += 1
    if has_add:
        add_ref = refs[pos]
        pos += 1
    o_ref = refs[pos]
    acc_ref = refs[pos + 1] if nk > 1 else None

    def finish(r):
        if has_gate:
            r = _sigmoid(gate_ref[...]) * r
        if has_add:
            r = add_ref[...] + r
        o_ref[...] = r.astype(o_ref.dtype)

    part = jnp.dot(a_ref[...], w_ref[...], preferred_element_type=F32)
    if nk == 1:
        finish(part)
    else:
        k = pl.program_id(2)

        @pl.when(k == 0)
        def _():
            acc_ref[...] = part

        @pl.when(jnp.logical_and(k > 0, k < nk - 1))
        def _():
            acc_ref[...] += part

        @pl.when(k == nk - 1)
        def _():
            finish(acc_ref[...] + part)


def _matmul(a, w, *, n, out_dtype, tm, tn, tk=None, w_col0=0, gate=None, gate_col0=0, addend=None,
            name="matmul"):
    m, kdim = a.shape
    tk = kdim if tk is None else tk
    nk = kdim // tk
    wj0 = w_col0 // tn
    gj0 = gate_col0 // tn
    in_specs = [pl.BlockSpec((tm, tk), lambda i, j, k: (i, k)),
                pl.BlockSpec((tk, tn), lambda i, j, k: (k, j + wj0))]
    args = [a, w]
    if gate is not None:
        in_specs.append(pl.BlockSpec((tm, tn), lambda i, j, k: (i, j + gj0)))
        args.append(gate)
    if addend is not None:
        in_specs.append(pl.BlockSpec((tm, tn), lambda i, j, k: (i, j)))
        args.append(addend)
    scratch = [pltpu.VMEM((tm, tn), F32)] if nk > 1 else []
    return pl.pallas_call(
        functools.partial(_mm_kernel, nk=nk, has_gate=gate is not None, has_add=addend is not None),
        grid=(m // tm, n // tn, nk),
        in_specs=in_specs,
        out_specs=pl.BlockSpec((tm, tn), lambda i, j, k: (i, j)),
        out_shape=jax.ShapeDtypeStruct((m, n), out_dtype),
        scratch_shapes=scratch,
        compiler_params=_params(("parallel", "parallel", "arbitrary")),
        name=name,
    )(*args)


def _swiglu_kernel(a_ref, wa_ref, wc_ref, o_ref):
    a = a_ref[...]
    fa = jnp.dot(a, wa_ref[...], preferred_element_type=F32)
    fc = jnp.dot(a, wc_ref[...], preferred_element_type=F32)
    o_ref[...] = (_silu(fa) * fc).astype(o_ref.dtype)


def _swiglu_matmul(a, w, *, tm, tn):
    m, kdim = a.shape
    nj = FFN_HIDDEN // tn
    return pl.pallas_call(
        _swiglu_kernel,
        grid=(m // tm, nj),
        in_specs=[pl.BlockSpec((tm, kdim), lambda i, j: (i, 0)),
                  pl.BlockSpec((kdim, tn), lambda i, j: (0, j)),
                  pl.BlockSpec((kdim, tn), lambda i, j: (0, j + nj))],
        out_specs=pl.BlockSpec((tm, tn), lambda i, j: (i, j)),
        out_shape=jax.ShapeDtypeStruct((m, FFN_HIDDEN), BF16),
        compiler_params=_params(("parallel", "parallel")),
        name="ffn_in_swiglu",
    )(a, w, w)


def _split3(a):
    hi = a.astype(BF16)
    r1 = a - hi.astype(F32)
    mid = r1.astype(BF16)
    lo = (r1 - mid.astype(F32)).astype(BF16)
    return hi, mid, lo


def _exact_right(a, onehot):
    return sum(jnp.dot(p, onehot, preferred_element_type=F32) for p in _split3(a))


def _exact_left(onehot, b):
    return sum(jnp.dot(onehot, p, preferred_element_type=F32) for p in _split3(b))


def _dot_nt(a, b):
    return lax.dot_general(a, b, (((1,), (1,)), ((), ())), preferred_element_type=F32)


def _dot_tn(a, b):
    return lax.dot_general(a, b, (((0,), (0,)), ((), ())), preferred_element_type=F32)


def _conv_silu(cur, prev, cw, cb):
    row = lax.broadcasted_iota(jnp.int32, cur.shape, 0)
    acc = cur * cw[CONV_W - 1:CONV_W, :] + cb
    for s in range(1, CONV_W):
        shifted = jnp.where(row < s, pltpu.roll(prev, s, 0), pltpu.roll(cur, s, 0))
        acc = acc + shifted * cw[CONV_W - 1 - s:CONV_W - s, :]
    return _silu(acc)


def _ssd_kernel(x_ref, b_ref, c_ref, z_ref, dt_ref, hist_ref, h0_ref,
                cwx_ref, cwb_ref, cwc_ref, cbx_ref, cbb_ref, cbc_ref,
                dtb_ref, alog_p_ref, alog_l_ref, dskip_ref, normw_ref, exp_p_ref, exp_l_ref,
                *rest, L, nc, aliased):
    if aliased:
        rest = rest[1:]
    y_ref, hout_ref, px_ref, pb_ref, pc_ref, ht_ref = rest
    c = pl.program_id(1)
    lw = SSM_HEADS * L

    @pl.when(c == 0)
    def _():
        px_ref[...] = jnp.zeros_like(px_ref)
        pb_ref[...] = jnp.zeros_like(pb_ref)
        pc_ref[...] = jnp.zeros_like(pc_ref)
        hist = hist_ref[0]
        px_ref[L - SUBLANES:L, :] = hist[:, :D_SSM]
        pb_ref[L - SUBLANES:L, :] = hist[:, D_SSM:D_SSM + BC_WIDTH]
        pc_ref[L - SUBLANES:L, :] = hist[:, D_SSM + BC_WIDTH:]
        ht_ref[...] = h0_ref[0].T

    x_in = x_ref[...]
    b_in = b_ref[...]
    c_in = c_ref[...]
    xs = _conv_silu(x_in, px_ref[...], cwx_ref[...], cbx_ref[...])
    bm = _conv_silu(b_in, pb_ref[...], cwb_ref[...], cbb_ref[...]).astype(BF16)
    cm = _conv_silu(c_in, pc_ref[...], cwc_ref[...], cbc_ref[...]).astype(BF16)
    px_ref[...] = x_in
    pb_ref[...] = b_in
    pc_ref[...] = c_in

    dtr = dt_ref[...] + dtb_ref[...]
    dt = jnp.maximum(dtr, 0.0) + jnp.log1p(jnp.exp(-jnp.abs(dtr)))

    tri = (lax.broadcasted_iota(jnp.int32, (L, L), 1)
           <= lax.broadcasted_iota(jnp.int32, (L, L), 0)).astype(BF16)
    ones_ll = jnp.ones((L, L), BF16)

    dt_p = _exact_right(dt, exp_p_ref[...])
    a_p = -jnp.exp(alog_p_ref[...])
    acum_p = _exact_left(tri, dt_p * a_p)
    last_p = acum_p[L - 1:L, :]
    if L == SSM_HEAD_DIM:
        dt_l, acum_l = dt_p, acum_p
    else:
        dt_l = _exact_right(dt, exp_l_ref[...])
        acum_l = _exact_left(tri, dt_l * (-jnp.exp(alog_l_ref[...])))

    rowi = lax.broadcasted_iota(jnp.int32, (L, lw), 0)
    colj = lax.broadcasted_iota(jnp.int32, (L, lw), 1) % L
    diag = colj == rowi
    causal = colj <= rowi
    acum_row = _exact_left(ones_ll, jnp.where(diag, acum_l, 0.0))
    dt_row = _exact_left(ones_ll, jnp.where(diag, dt_l, 0.0))
    decay = jnp.where(causal, jnp.exp(jnp.where(causal, acum_l - acum_row, 0.0)), 0.0)

    heads_per_group = SSM_HEADS // SSM_GROUPS
    cb = jnp.concatenate(
        [_dot_nt(cm[:, g * SSM_STATE:(g + 1) * SSM_STATE],
                 jnp.concatenate([bm[:, g * SSM_STATE:(g + 1) * SSM_STATE]] * heads_per_group, axis=0))
         for g in range(SSM_GROUPS)], axis=1)
    m_all = (cb * decay * dt_row).astype(BF16)

    hp = LANES // L
    slab = hp * SSM_HEAD_DIM
    xs_b = xs.astype(BF16)
    lane_head = lax.broadcasted_iota(jnp.int32, (L, slab), 1) // SSM_HEAD_DIM
    y_parts = []
    for q in range(lw // LANES):
        xq = xs_b[:, q * slab:(q + 1) * slab]
        bd = jnp.concatenate([jnp.where(lane_head == h, xq, jnp.zeros_like(xq)) for h in range(hp)], axis=0)
        y_parts.append(jnp.dot(m_all[:, q * LANES:(q + 1) * LANES], bd, preferred_element_type=F32))
    y = jnp.concatenate(y_parts, axis=1)

    ht = ht_ref[...]
    ht_b = ht.astype(BF16)
    to_end = jnp.exp(last_p - acum_p) * dt_p
    xw = (xs * to_end).astype(BF16)
    y_inter = []
    st_parts = []
    for g in range(SSM_GROUPS):
        gs = slice(g * SSM_GROUP_WIDTH, (g + 1) * SSM_GROUP_WIDTH)
        ns = slice(g * SSM_STATE, (g + 1) * SSM_STATE)
        y_inter.append(jnp.dot(cm[:, ns], ht_b[:, gs], preferred_element_type=F32))
        st_parts.append(_dot_tn(bm[:, ns], xw[:, gs]))
    y = y + jnp.concatenate(y_inter, axis=1) * jnp.exp(acum_p) + dskip_ref[...] * xs
    ht_new = jnp.exp(last_p) * ht + jnp.concatenate(st_parts, axis=1)
    ht_ref[...] = ht_new

    t = y * _silu(z_ref[...])
    outs = []
    for g in range(SSM_GROUPS):
        tg = t[:, g * SSM_GROUP_WIDTH:(g + 1) * SSM_GROUP_WIDTH]
        outs.append(tg * lax.rsqrt(jnp.mean(tg * tg, axis=-1, keepdims=True) + EPS))
    y_ref[...] = (jnp.concatenate(outs, axis=1) * normw_ref[...]).astype(y_ref.dtype)

    @pl.when(c == nc - 1)
    def _():
        hout_ref[0] = ht_new.T


def _ssd_call(u, dt_raw, hist8, h0, lp, *, L, nb, nc, row0, prev_out):
    rb0 = row0 // L
    aliased = prev_out is not None

    def rows(width, colblock):
        return pl.BlockSpec((L, width), lambda b, c: (rb0 + b * nc + c, colblock))

    def const(shape):
        return pl.BlockSpec(shape, lambda b, c: tuple(0 for _ in shape))

    lw = SSM_HEADS * L
    in_specs = [rows(D_SSM, U_X // D_SSM), rows(BC_WIDTH, U_B // BC_WIDTH), rows(BC_WIDTH, U_C // BC_WIDTH),
                rows(D_SSM, U_Z // D_SSM),
                pl.BlockSpec((L, LANES), lambda b, c: (rb0 + b * nc + c, 0)),
                pl.BlockSpec((1, SUBLANES, CONV_DIM), lambda b, c: (b, 0, 0)),
                pl.BlockSpec((1, D_SSM, SSM_STATE), lambda b, c: (b, 0, 0)),
                const((CONV_W, D_SSM)), const((CONV_W, BC_WIDTH)), const((CONV_W, BC_WIDTH)),
                const((1, D_SSM)), const((1, BC_WIDTH)), const((1, BC_WIDTH)),
                const((1, LANES)), const((1, D_SSM)), const((1, lw)), const((1, D_SSM)), const((1, D_SSM)),
                const((LANES, D_SSM)), const((LANES, lw))]
    args = [u, u, u, u, dt_raw, hist8, h0,
            lp["conv_wx"], lp["conv_wb"], lp["conv_wc"], lp["conv_bx"], lp["conv_bb"], lp["conv_bc"],
            lp["dt_bias"], lp["a_log_p"], lp["a_log_l%d" % L], lp["d_skip_p"], lp["ssm_norm"],
            _expand_onehot(SSM_HEAD_DIM), _expand_onehot(L)]
    aliases = {}
    if aliased:
        in_specs.append(pl.BlockSpec(memory_space=pl.ANY))
        args.append(prev_out)
        aliases = {len(args) - 1: 0}
    y, hout = pl.pallas_call(
        functools.partial(_ssd_kernel, L=L, nc=nc, aliased=aliased),
        grid=(nb, nc),
        in_specs=in_specs,
        out_specs=[pl.BlockSpec((L, D_SSM), lambda b, c: (rb0 + b * nc + c, 0)),
                   pl.BlockSpec((1, D_SSM, SSM_STATE), lambda b, c: (b, 0, 0))],
        out_shape=[jax.ShapeDtypeStruct((M_ALL, D_SSM), BF16),
                   jax.ShapeDtypeStruct((nb, D_SSM, SSM_STATE), F32)],
        scratch_shapes=[pltpu.VMEM((L, D_SSM), F32), pltpu.VMEM((L, BC_WIDTH), F32),
                        pltpu.VMEM((L, BC_WIDTH), F32), pltpu.VMEM((SSM_STATE, D_SSM), F32)],
        input_output_aliases=aliases,
        compiler_params=_params(("parallel", "arbitrary")),
        name="ssd_L%d" % L,
    )(*args)
    return y, hout


def _expand_onehot(per_head):
    e = np.zeros((LANES, SSM_HEADS * per_head), np.float32)
    for h in range(SSM_HEADS):
        e[h, h * per_head:(h + 1) * per_head] = 1.0
    return jnp.asarray(e, BF16)


def _ret_consts(L):
    lam = np.log1p(-np.exp2(-5.0 - np.arange(RET_HEADS, dtype=np.float64)))
    i = np.arange(L, dtype=np.float64)
    diff = i[:, None] - i[None, :]
    dmat = np.where(diff >= 0, np.exp(lam[:, None, None] * np.maximum(diff, 0.0)), 0.0)
    to_end = np.exp(lam[:, None] * (L - 1 - i)[None, :])
    q_dec = np.exp(lam[:, None] * (i + 1.0)[None, :])
    chunk_decay = np.exp(lam * L)
    expand = lambda t: np.repeat(t.T[:, :, None], RET_DK, axis=2).reshape(L, RET_HEADS * RET_DK)
    return (jnp.asarray(dmat, F32), jnp.asarray(expand(q_dec), F32), jnp.asarray(expand(to_end), F32),
            [float(v) for v in chunk_decay])


def _ret_kernel(q_ref, k_ref, v_ref, g_ref, cos_ref, sin_ref, qdec_ref, kend_ref, dmat_ref, s0_ref,
                *rest, L, nc, chunk_decay, aliased):
    if aliased:
        rest = rest[1:]
    y_ref, sout_ref, s_ref = rest
    c = pl.program_id(1)

    @pl.when(c == 0)
    def _():
        s_ref[...] = s0_ref[0]

    cos = cos_ref[...]
    sin = sin_ref[...]
    half = RET_DK // 2
    for h in range(RET_HEADS):
        ks = slice(h * RET_DK, (h + 1) * RET_DK)
        vs = slice(h * RET_DV, (h + 1) * RET_DV)
        qh = q_ref[:, ks]
        kh = k_ref[:, ks]
        qr = qh * cos + pltpu.roll(qh, half, 1) * sin
        kr = (kh * cos + pltpu.roll(kh, half, 1) * sin) * (RET_DK ** -0.5)
        vh = v_ref[:, vs].astype(BF16)
        s_prev = s_ref[h]
        s = _dot_nt(qr.astype(BF16), kr.astype(BF16)) * dmat_ref[h]
        o = jnp.dot(s.astype(BF16), vh, preferred_element_type=F32)
        o = o + jnp.dot((qr * qdec_ref[:, ks]).astype(BF16), s_prev.astype(BF16), preferred_element_type=F32)
        kv = _dot_tn((kr * kend_ref[:, ks]).astype(BF16), vh)
        s_ref[h] = chunk_decay[h] * s_prev + kv
        o = o * lax.rsqrt(jnp.mean(o * o, axis=-1, keepdims=True) + EPS)
        y_ref[:, vs] = (o * _silu(g_ref[:, vs])).astype(y_ref.dtype)

    @pl.when(c == nc - 1)
    def _():
        sout_ref[0] = s_ref[...]


def _ret_call(u, cos_t, sin_t, s0, *, L, nb, nc, row0, prev_out):
    rb0 = row0 // L
    aliased = prev_out is not None
    dmat, qdec, kend, chunk_decay = _ret_consts(L)
    hk = RET_HEADS * RET_DK
    hv = RET_HEADS * RET_DV

    def rows(width, colblock):
        return pl.BlockSpec((L, width), lambda b, c: (rb0 + b * nc + c, colblock))

    in_specs = [rows(hk, U_RQ // hk), rows(hk, U_RK // hk), rows(hv, U_RV // hv), rows(hv, U_RG // hv),
                pl.BlockSpec((L, RET_DK), lambda b, c: (c, 0)),
                pl.BlockSpec((L, RET_DK), lambda b, c: (c, 0)),
                pl.BlockSpec((L, hk), lambda b, c: (0, 0)),
                pl.BlockSpec((L, hk), lambda b, c: (0, 0)),
                pl.BlockSpec((RET_HEADS, L, L), lambda b, c: (0, 0, 0)),
                pl.BlockSpec((1, RET_HEADS, RET_DK, RET_DV), lambda b, c: (b, 0, 0, 0))]
    args = [u, u, u, u, cos_t, sin_t, qdec, kend, dmat, s0]
    aliases = {}
    if aliased:
        in_specs.append(pl.BlockSpec(memory_space=pl.ANY))
        args.append(prev_out)
        aliases = {len(args) - 1: 0}
    y, sout = pl.pallas_call(
        functools.partial(_ret_kernel, L=L, nc=nc, chunk_decay=chunk_decay, aliased=aliased),
        grid=(nb, nc),
        in_specs=in_specs,
        out_specs=[pl.BlockSpec((L, hv), lambda b, c: (rb0 + b * nc + c, 0)),
                   pl.BlockSpec((1, RET_HEADS, RET_DK, RET_DV), lambda b, c: (b, 0, 0, 0))],
        out_shape=[jax.ShapeDtypeStruct((M_ALL, hv), BF16),
                   jax.ShapeDtypeStruct((nb, RET_HEADS, RET_DK, RET_DV), F32)],
        scratch_shapes=[pltpu.VMEM((RET_HEADS, RET_DK, RET_DV), F32)],
        input_output_aliases=aliases,
        compiler_params=_params(("parallel", "arbitrary")),
        name="retention_L%d" % L,
    )(*args)
    return y, sout


def _rope_tables(pos):
    half = RET_DK // 2
    inv = ROPE_BASE ** (-jnp.arange(half, dtype=F32) * 2.0 / RET_DK)
    ang = pos.astype(F32)[:, None] * inv[None, :]
    cos = jnp.cos(ang)
    sin = jnp.sin(ang)
    return jnp.concatenate([cos, cos], axis=1), jnp.concatenate([-sin, sin], axis=1)


def _head_norm(xh, g):
    return xh * lax.rsqrt(jnp.mean(xh * xh, axis=-1, keepdims=True) + EPS) * g


def _kv_prep_kernel(k_ref, v_ref, gk_ref, knf_ref, knb_ref, vb_ref, *, pad_first):
    def body():
        for h in range(ATT_HEADS):
            hs = slice(h * ATT_HEAD_DIM, (h + 1) * ATT_HEAD_DIM)
            kn = _head_norm(k_ref[:, hs], gk_ref[...])
            knf_ref[0, :, hs] = kn
            knb_ref[0, :, hs] = kn.astype(BF16)
        vb_ref[0] = v_ref[...].astype(BF16)

    if pad_first:
        i = pl.program_id(1)

        @pl.when(i == 0)
        def _():
            knb_ref[...] = jnp.zeros_like(knb_ref)
            vb_ref[...] = jnp.zeros_like(vb_ref)

        @pl.when(i > 0)
        def _():
            body()
    else:
        body()


def _kv_prep_prompt(u, gk):
    blk = WINDOW
    nblk = SEQ // blk
    dm = ATT_HEADS * ATT_HEAD_DIM

    def rows(colblock):
        return pl.BlockSpec((blk, dm), lambda b, i: (b * nblk + jnp.maximum(i - 1, 0), colblock))

    return pl.pallas_call(
        functools.partial(_kv_prep_kernel, pad_first=True),
        grid=(BATCH, nblk + 1),
        in_specs=[rows(U_AK // dm), rows(U_AV // dm), pl.BlockSpec((1, ATT_HEAD_DIM), lambda b, i: (0, 0))],
        out_specs=[pl.BlockSpec((1, blk, dm), lambda b, i: (b, 0, 0)),
                   pl.BlockSpec((1, blk, dm), lambda b, i: (b, i, 0)),
                   pl.BlockSpec((1, blk, dm), lambda b, i: (b, i, 0))],
        out_shape=[jax.ShapeDtypeStruct((BATCH, blk, dm), F32),
                   jax.ShapeDtypeStruct((BATCH, SEQ + WINDOW, dm), BF16),
                   jax.ShapeDtypeStruct((BATCH, SEQ + WINDOW, dm), BF16)],
        compiler_params=_params(("parallel", "arbitrary")),
        name="kv_prep_prompt",
    )(u, u, gk)


def _kv_prep_sample(u, gk):
    dm = ATT_HEADS * ATT_HEAD_DIM
    rb = M_PROMPT // M_SAMPLE
    return pl.pallas_call(
        functools.partial(_kv_prep_kernel, pad_first=False),
        grid=(1,),
        in_specs=[pl.BlockSpec((M_SAMPLE, dm), lambda i: (rb, U_AK // dm)),
                  pl.BlockSpec((M_SAMPLE, dm), lambda i: (rb, U_AV // dm)),
                  pl.BlockSpec((1, ATT_HEAD_DIM), lambda i: (0, 0))],
        out_specs=[pl.BlockSpec((1, M_SAMPLE, dm), lambda i: (0, 0, 0)),
                   pl.BlockSpec((1, M_SAMPLE, dm), lambda i: (0, 0, 0)),
                   pl.BlockSpec((1, M_SAMPLE, dm), lambda i: (0, 0, 0))],
        out_shape=[jax.ShapeDtypeStruct((1, M_SAMPLE, dm), F32),
                   jax.ShapeDtypeStruct((1, M_SAMPLE, dm), BF16),
                   jax.ShapeDtypeStruct((1, M_SAMPLE, dm), BF16)],
        compiler_params=_params(("arbitrary",)),
        name="kv_prep_sample",
    )(u, u, gk)


def _attend(qn, k1, v1, k2, v2, bias1, bias2, valid1):
    scale = ATT_HEAD_DIM ** -0.5
    s1 = _dot_nt(qn, k1) * scale + bias1
    if valid1 is not None:
        s1 = jnp.where(valid1, s1, NEG_INF)
    s2 = _dot_nt(qn, k2) * scale + bias2
    m = jnp.maximum(jnp.max(s1, axis=-1, keepdims=True), jnp.max(s2, axis=-1, keepdims=True))
    p1 = jnp.exp(s1 - m)
    p2 = jnp.exp(s2 - m)
    inv = 1.0 / (jnp.sum(p1, axis=-1, keepdims=True) + jnp.sum(p2, axis=-1, keepdims=True))
    return (jnp.dot((p1 * inv).astype(BF16), v1, preferred_element_type=F32)
            + jnp.dot((p2 * inv).astype(BF16), v2, preferred_element_type=F32))


def _attn_prompt_kernel(q_ref, k_ref, v_ref, b1_ref, b2_ref, gq_ref, y_ref):
    c = pl.program_id(1)
    base = pl.multiple_of(c * CHUNK, CHUNK)
    col = lax.broadcasted_iota(jnp.int32, (CHUNK, WINDOW), 1)
    valid1 = col + c * CHUNK >= WINDOW
    for h in range(ATT_HEADS):
        hs = slice(h * ATT_HEAD_DIM, (h + 1) * ATT_HEAD_DIM)
        qn = _head_norm(q_ref[:, hs], gq_ref[...]).astype(BF16)
        k1 = k_ref[0, pl.ds(base, WINDOW), hs]
        v1 = v_ref[0, pl.ds(base, WINDOW), hs]
        k2 = k_ref[0, pl.ds(base + WINDOW, CHUNK), hs]
        v2 = v_ref[0, pl.ds(base + WINDOW, CHUNK), hs]
        o = _attend(qn, k1, v1, k2, v2, b1_ref[h], b2_ref[h], valid1)
        y_ref[:, hs] = o.astype(y_ref.dtype)


def _attn_prompt_call(u, knb, vb, bias1, bias2, gq):
    dm = ATT_HEADS * ATT_HEAD_DIM
    nc = SEQ // CHUNK
    return pl.pallas_call(
        _attn_prompt_kernel,
        grid=(BATCH, nc),
        in_specs=[pl.BlockSpec((CHUNK, dm), lambda b, c: (b * nc + c, U_AQ // dm)),
                  pl.BlockSpec((1, SEQ + WINDOW, dm), lambda b, c: (b, 0, 0)),
                  pl.BlockSpec((1, SEQ + WINDOW, dm), lambda b, c: (b, 0, 0)),
                  pl.BlockSpec((ATT_HEADS, CHUNK, WINDOW), lambda b, c: (0, 0, 0)),
                  pl.BlockSpec((ATT_HEADS, CHUNK, CHUNK), lambda b, c: (0, 0, 0)),
                  pl.BlockSpec((1, ATT_HEAD_DIM), lambda b, c: (0, 0))],
        out_specs=pl.BlockSpec((CHUNK, dm), lambda b, c: (b * nc + c, 0)),
        out_shape=jax.ShapeDtypeStruct((M_ALL, dm), BF16),
        compiler_params=_params(("parallel", "arbitrary")),
        name="attn_prompt",
    )(u, knb, vb, bias1, bias2, gq)


def _attn_sample_kernel(q_ref, kh_ref, vh_ref, kn_ref, vn_ref, b1_ref, b2_ref, gq_ref, prev_ref, y_ref):
    del prev_ref
    for h in range(ATT_HEADS):
        hs = slice(h * ATT_HEAD_DIM, (h + 1) * ATT_HEAD_DIM)
        qn = _head_norm(q_ref[:, hs], gq_ref[...]).astype(BF16)
        k1 = kh_ref[0, :, hs].astype(BF16)
        v1 = vh_ref[0, :, hs].astype(BF16)
        o = _attend(qn, k1, v1, kn_ref[:, hs], vn_ref[:, hs], b1_ref[h], b2_ref[h], None)
        y_ref[:, hs] = o.astype(y_ref.dtype)


def _attn_sample_call(u, k_hist, v_hist, kn_new, vn_new, bias1, bias2, gq, prev_out):
    dm = ATT_HEADS * ATT_HEAD_DIM
    rb0 = M_PROMPT // DEC_SEQ
    hist = k_hist.shape[1]
    return pl.pallas_call(
        _attn_sample_kernel,
        grid=(DEC_BATCH,),
        in_specs=[pl.BlockSpec((DEC_SEQ, dm), lambda b: (rb0 + b, U_AQ // dm)),
                  pl.BlockSpec((1, hist, dm), lambda b: (b, 0, 0)),
                  pl.BlockSpec((1, hist, dm), lambda b: (b, 0, 0)),
                  pl.BlockSpec((DEC_SEQ, dm), lambda b: (b, 0)),
                  pl.BlockSpec((DEC_SEQ, dm), lambda b: (b, 0)),
                  pl.BlockSpec((ATT_HEADS, DEC_SEQ, hist), lambda b: (0, 0, 0)),
                  pl.BlockSpec((ATT_HEADS, DEC_SEQ, DEC_SEQ), lambda b: (0, 0, 0)),
                  pl.BlockSpec((1, ATT_HEAD_DIM), lambda b: (0, 0)),
                  pl.BlockSpec(memory_space=pl.ANY)],
        out_specs=pl.BlockSpec((DEC_SEQ, dm), lambda b: (rb0 + b, 0)),
        out_shape=jax.ShapeDtypeStruct((M_ALL, dm), BF16),
        input_output_aliases={8: 0},
        compiler_params=_params(("arbitrary",)),
        name="attn_sample",
    )(u, k_hist, v_hist, kn_new, vn_new, bias1, bias2, gq, prev_out)


def _bias_table(rel_bias, cq, off, kbn):
    rel = off + jnp.arange(cq)[:, None] - jnp.arange(kbn)[None, :]
    return rel_bias[:, jnp.clip(rel, -REL_CLIP, REL_CLIP) + REL_CLIP]


def _layer_params(l, norm_mix, w_in, conv_w, conv_b, dt_bias, a_log, d_skip, ssm_norm, q_norm, k_norm,
                  rel_bias, w_br_ssm, w_br_ret, w_br_att, w_out, norm_ffn, w_ffn_in, w_ffn_out):
    o = _IN_OFF
    wi = w_in[l]
    seg = lambda i: wi[:, o[i]:o[i + 1]]
    xbc = seg(1)
    w_main = jnp.concatenate(
        [seg(0), xbc[:, :D_SSM], seg(5), seg(6), seg(7), seg(8), seg(9), seg(10), seg(11), seg(12),
         seg(3), seg(4), xbc[:, D_SSM:D_SSM + BC_WIDTH], xbc[:, D_SSM + BC_WIDTH:]], axis=1).astype(BF16)
    w_dt = jnp.pad(seg(2), ((0, 0), (0, LANES - SSM_HEADS))).astype(BF16)
    cw = conv_w[l]
    cb = conv_b[l].reshape(1, CONV_DIM)
    lp = {
        "norm_mix": norm_mix[l], "norm_ffn": norm_ffn[l],
        "w_main": w_main, "w_dt": w_dt,
        "conv_wx": cw[:, :D_SSM], "conv_wb": cw[:, D_SSM:D_SSM + BC_WIDTH], "conv_wc": cw[:, D_SSM + BC_WIDTH:],
        "conv_bx": cb[:, :D_SSM], "conv_bb": cb[:, D_SSM:D_SSM + BC_WIDTH], "conv_bc": cb[:, D_SSM + BC_WIDTH:],
        "dt_bias": jnp.pad(dt_bias[l], (0, LANES - SSM_HEADS)).reshape(1, LANES),
        "a_log_p": jnp.repeat(a_log[l], SSM_HEAD_DIM).reshape(1, D_SSM),
        "a_log_l%d" % CHUNK: jnp.repeat(a_log[l], CHUNK).reshape(1, SSM_HEADS * CHUNK),
        "a_log_l%d" % DEC_SEQ: jnp.repeat(a_log[l], DEC_SEQ).reshape(1, SSM_HEADS * DEC_SEQ),
        "d_skip_p": jnp.repeat(d_skip[l], SSM_HEAD_DIM).reshape(1, D_SSM),
        "ssm_norm": ssm_norm[l].reshape(1, D_SSM),
        "q_norm": q_norm[l].reshape(1, ATT_HEAD_DIM), "k_norm": k_norm[l].reshape(1, ATT_HEAD_DIM),
        "rel_bias": rel_bias[l],
        "w_br_ssm": w_br_ssm[l].astype(BF16), "w_br_ret": w_br_ret[l].astype(BF16),
        "w_br_att": w_br_att[l].astype(BF16), "w_out": w_out[l].astype(BF16),
        "w_ffn_in": w_ffn_in[l].astype(BF16), "w_ffn_out": w_ffn_out[l].astype(BF16),
    }
    return lp


def _pad_hist(hist):
    return jnp.pad(hist, ((0, 0), (SUBLANES - (CONV_W - 1), 0), (0, 0)))


def _layer(x, lp, state, tables):
    k_hist, v_hist, ret_s0, ssm_h0, conv_hist = state
    hn = _rmsnorm(x, lp["norm_mix"])
    u = _matmul(hn, lp["w_main"], n=U_WIDTH, out_dtype=F32, tm=1088, tn=1024, name="in_proj")
    dt_raw = _matmul(hn, lp["w_dt"], n=LANES, out_dtype=F32, tm=1088, tn=LANES, name="in_proj_dt")

    zeros_hist = jnp.zeros((BATCH, SUBLANES, CONV_DIM), F32)
    zeros_h = jnp.zeros((BATCH, D_SSM, SSM_STATE), F32)
    ys, ssm_p = _ssd_call(u, dt_raw, zeros_hist, zeros_h, lp, L=CHUNK, nb=BATCH, nc=SEQ // CHUNK, row0=0,
                          prev_out=None)
    ys, ssm_s = _ssd_call(u, dt_raw, _pad_hist(conv_hist), ssm_h0.reshape(DEC_BATCH, D_SSM, SSM_STATE), lp,
                          L=DEC_SEQ, nb=DEC_BATCH, nc=1, row0=M_PROMPT, prev_out=ys)

    zeros_s = jnp.zeros((BATCH, RET_HEADS, RET_DK, RET_DV), F32)
    yr, ret_p = _ret_call(u, tables["cos_p"], tables["sin_p"], zeros_s, L=CHUNK, nb=BATCH, nc=SEQ // CHUNK,
                          row0=0, prev_out=None)
    yr, ret_s = _ret_call(u, tables["cos_s"], tables["sin_s"], ret_s0, L=DEC_SEQ, nb=DEC_BATCH, nc=1,
                          row0=M_PROMPT, prev_out=yr)

    dm = ATT_HEADS * ATT_HEAD_DIM
    knf_p, knb_p, vb_p = _kv_prep_prompt(u, lp["k_norm"])
    knf_s, knb_s, vb_s = _kv_prep_sample(u, lp["k_norm"])
    hist = k_hist.shape[1]
    tbl = _bias_table(lp["rel_bias"], CHUNK, WINDOW, WINDOW + CHUNK)
    ya = _attn_prompt_call(u, knb_p, vb_p, tbl[:, :, :WINDOW], tbl[:, :, WINDOW:], lp["q_norm"])
    tbl_s = _bias_table(lp["rel_bias"], DEC_SEQ, hist, hist + DEC_SEQ)
    ya = _attn_sample_call(u, k_hist.reshape(DEC_BATCH, hist, dm), v_hist.reshape(DEC_BATCH, hist, dm),
                           knb_s[0], vb_s[0], tbl_s[:, :, :hist], tbl_s[:, :, hist:], lp["q_norm"], ya)

    m1 = _matmul(ys, lp["w_br_ssm"], n=D_MODEL, out_dtype=F32, tm=1088, tn=1024, gate=u, gate_col0=U_GA,
                 name="merge_ssm")
    m2 = _matmul(yr, lp["w_br_ret"], n=D_MODEL, out_dtype=F32, tm=1088, tn=1024, gate=u, gate_col0=U_GB,
                 addend=m1, name="merge_ret")
    mm = _matmul(ya, lp["w_br_att"], n=D_MODEL, out_dtype=BF16, tm=1088, tn=1024, gate=u, gate_col0=U_GC,
                 addend=m2, name="merge_att")
    x1 = _matmul(mm, lp["w_out"], n=D_MODEL, out_dtype=F32, tm=1088, tn=1024, addend=x, name="out_proj")

    h2 = _rmsnorm(x1, lp["norm_ffn"])
    hh = _swiglu_matmul(h2, lp["w_ffn_in"], tm=1088, tn=512)
    x2 = _matmul(hh, lp["w_ffn_out"], n=D_MODEL, out_dtype=F32, tm=1088, tn=1024, tk=FFN_HIDDEN // 2,
                 addend=x1, name="ffn_out")

    up = u[:M_PROMPT].reshape(BATCH, SEQ, U_WIDTH)
    us = u[M_PROMPT:].reshape(DEC_BATCH, DEC_SEQ, U_WIDTH)
    keep = min(WINDOW, SEQ)

    def conv_state(t):
        tail = t[:, -(CONV_W - 1):]
        return jnp.concatenate([tail[..., U_X:U_X + D_SSM], tail[..., U_B:U_B + BC_WIDTH],
                                tail[..., U_C:U_C + BC_WIDTH]], axis=-1)

    outs_p = (knf_p.reshape(BATCH, keep, ATT_HEADS, ATT_HEAD_DIM),
              up[:, SEQ - keep:, U_AV:U_AV + dm].reshape(BATCH, keep, ATT_HEADS, ATT_HEAD_DIM),
              ret_p,
              ssm_p.reshape(BATCH, SSM_HEADS, SSM_HEAD_DIM, SSM_STATE),
              conv_state(up))
    outs_s = (knf_s.reshape(DEC_BATCH, DEC_SEQ, ATT_HEADS, ATT_HEAD_DIM),
              us[:, :, U_AV:U_AV + dm].reshape(DEC_BATCH, DEC_SEQ, ATT_HEADS, ATT_HEAD_DIM),
              ret_s,
              ssm_s.reshape(DEC_BATCH, SSM_HEADS, SSM_HEAD_DIM, SSM_STATE),
              conv_state(us))
    return x2, outs_p, outs_s


def kernel(x_prompt, x_sample, cache_attn_k, cache_attn_v, state_ret, state_ssm, state_conv,
           norm_mix, w_in, conv_w, conv_b, dt_bias, a_log, d_skip, ssm_norm, q_norm, k_norm,
           rel_bias, w_br_ssm, w_br_ret, w_br_att, w_out, norm_ffn, w_ffn_in, w_ffn_out):
    x = jnp.concatenate([x_prompt.reshape(M_PROMPT, D_MODEL), x_sample.reshape(M_SAMPLE, D_MODEL)], axis=0)
    cos_p, sin_p = _rope_tables(jnp.arange(SEQ))
    cos_s, sin_s = _rope_tables(PAST_LEN + jnp.arange(DEC_SEQ))
    tables = {"cos_p": cos_p, "sin_p": sin_p, "cos_s": cos_s, "sin_s": sin_s}
    acc_p = ([], [], [], [], [])
    acc_s = ([], [], [], [], [])
    for l in range(DEPTH):
        lp = _layer_params(l, norm_mix, w_in, conv_w, conv_b, dt_bias, a_log, d_skip, ssm_norm, q_norm,
                           k_norm, rel_bias, w_br_ssm, w_br_ret, w_br_att, w_out, norm_ffn, w_ffn_in,
                           w_ffn_out)
        state = (cache_attn_k[l], cache_attn_v[l], state_ret[l], state_ssm[l], state_conv[l])
        x, outs_p, outs_s = _layer(x, lp, state, tables)
        for i in range(5):
            acc_p[i].append(outs_p[i])
            acc_s[i].append(outs_s[i])
    pk, pv, pr, pssm, pconv = [jnp.stack(a) for a in acc_p]
    sk, sv, sr, sssm, sconv = [jnp.stack(a) for a in acc_s]
    yp = x[:M_PROMPT].reshape(BATCH, SEQ, D_MODEL)
    ys = x[M_PROMPT:].reshape(DEC_BATCH, DEC_SEQ, D_MODEL)
    return (yp, ys, pk, pv, pr, pssm, pconv, sk, sv, sr, sssm, sconv)
```

```python
import functools
import math

import numpy as np
import jax
import jax.numpy as jnp
from jax import lax
from jax.experimental import pallas as pl
from jax.experimental.pallas import tpu as pltpu

F32 = jnp.float32
BF16 = jnp.bfloat16

D_MODEL = 2048
BATCH = 4
SEQ = 2048
DEPTH = 2
DEC_BATCH = 32
DEC_SEQ = 16
PAST_LEN = 1024
CHUNK = 64
EPS = 1e-6
NEG_INF = -1e30

D_SSM = 2048
SSM_HEAD_DIM = 64
SSM_HEADS = 32
SSM_GROUPS = 4
SSM_STATE = 128
SSM_GROUP_WIDTH = D_SSM // SSM_GROUPS
CONV_W = 4
CONV_DIM = D_SSM + 2 * SSM_GROUPS * SSM_STATE
BC_WIDTH = SSM_GROUPS * SSM_STATE

RET_HEADS = 8
RET_DK = 128
RET_DV = 256
ROPE_BASE = 10000.0

ATT_HEADS = 16
ATT_HEAD_DIM = 128
BAND_CHUNKS = 8
WINDOW = BAND_CHUNKS * CHUNK
REL_CLIP = 256

FFN_HIDDEN = 5632

M_PROMPT = BATCH * SEQ
M_SAMPLE = DEC_BATCH * DEC_SEQ
M_ALL = M_PROMPT + M_SAMPLE

LANES = 128
SUBLANES = 8
V7X_VMEM_LIMIT_BYTES = 56 * 1024 * 1024

H_Z, H_X, H_B, H_C = 0, 2048, 4096, 4608
HEAD_WIDTH = 5120
DT_COL0 = 5120
TAIL_COL0 = DT_COL0 + SSM_HEADS
T_RQ, T_RK, T_RV, T_RG, T_AQ, T_AK, T_AV, T_GA, T_GB, T_GC = (
    0, 1024, 2048, 4096, 6144, 8192, 10240, 12288, 14336, 16384)
TAIL_WIDTH = 18432
N_IN = TAIL_COL0 + TAIL_WIDTH


def _params(sem):
    return pltpu.CompilerParams(dimension_semantics=sem, vmem_limit_bytes=V7X_VMEM_LIMIT_BYTES)


def _sigmoid(x):
    return 1.0 / (1.0 + jnp.exp(-x))


def _silu(x):
    return x * _sigmoid(x)


def _rmsnorm_kernel(x_ref, g_ref, o_ref):
    x = x_ref[...]
    ms = jnp.mean(x * x, axis=-1, keepdims=True)
    o_ref[...] = (x * lax.rsqrt(ms + EPS) * g_ref[...]).astype(o_ref.dtype)


def _rmsnorm(x, g, tm=544):
    m, d = x.shape
    return pl.pallas_call(
        _rmsnorm_kernel,
        grid=(m // tm,),
        in_specs=[pl.BlockSpec((tm, d), lambda i: (i, 0)),
                  pl.BlockSpec((1, d), lambda i: (0, 0))],
        out_specs=pl.BlockSpec((tm, d), lambda i: (i, 0)),
        out_shape=jax.ShapeDtypeStruct((m, d), BF16),
        compiler_params=_params(("parallel",)),
        name="rmsnorm",
    )(x, g.reshape(1, d))


W_CAST_ROWS = 256


def _cast_weight_tile(w_ref, wx_ref, wb_ref, shift):
    kdim = w_ref.shape[0]
    for r in range(0, kdim, W_CAST_ROWS):
        rs = slice(r, r + W_CAST_ROWS)
        w = w_ref[rs, :]
        if shift:
            w = jnp.concatenate([w[:, shift:], wx_ref[rs, :shift]], axis=1)
        wb_ref[rs, :] = w.astype(BF16)


def _mm_kernel(*refs, shift, has_gate, has_add):
    a_ref, w_ref = refs[0], refs[1]
    pos = 2
    wx_ref = gate_ref = add_ref = None
    if shift:
        wx_ref = refs[pos]
        pos += 1
    if has_gate:
        gate_ref = refs[pos]
        pos += 1
    if has_add:
        add_ref = refs[pos]
        pos += 1
    o_ref, wb_ref = refs[pos], refs[pos + 1]

    @pl.when(pl.program_id(1) == 0)
    def _():
        _cast_weight_tile(w_ref, wx_ref, wb_ref, shift)

    r = jnp.dot(a_ref[...], wb_ref[...], preferred_element_type=F32)
    if has_gate:
        r = _sigmoid(gate_ref[...]) * r
    if has_add:
        r = add_ref[...] + r
    o_ref[...] = r.astype(o_ref.dtype)


def _matmul(a, w, layer, *, n, out_dtype, tm, tn, w_col0=0, gate=None, gate_col0=0, addend=None,
            name="matmul"):
    m, kdim = a.shape
    shift = w_col0 % LANES
    wj0 = (w_col0 - shift) // tn
    gj0 = gate_col0 // tn
    in_specs = [pl.BlockSpec((tm, kdim), lambda j, i: (i, 0)),
                pl.BlockSpec((None, kdim, tn), lambda j, i: (layer, 0, j + wj0))]
    args = [a, w]
    if shift:
        xb0 = (w_col0 - shift) // LANES
        in_specs.append(pl.BlockSpec((None, kdim, LANES), lambda j, i: (layer, 0, xb0 + (j + 1) * (tn // LANES))))
        args.append(w)
    if gate is not None:
        in_specs.append(pl.BlockSpec((tm, tn), lambda j, i: (i, j + gj0)))
        args.append(gate)
    if addend is not None:
        in_specs.append(pl.BlockSpec((tm, tn), lambda j, i: (i, j)))
        args.append(addend)
    return pl.pallas_call(
        functools.partial(_mm_kernel, shift=shift, has_gate=gate is not None, has_add=addend is not None),
        grid=(n // tn, m // tm),
        in_specs=in_specs,
        out_specs=pl.BlockSpec((tm, tn), lambda j, i: (i, j)),
        out_shape=jax.ShapeDtypeStruct((m, n), out_dtype),
        scratch_shapes=[pltpu.VMEM((kdim, tn), BF16)],
        compiler_params=_params(("parallel", "arbitrary")),
        name=name,
    )(*args)


def _swiglu_kernel(a_ref, wa_ref, wc_ref, o_ref, wab_ref, wcb_ref):
    @pl.when(pl.program_id(1) == 0)
    def _():
        _cast_weight_tile(wa_ref, None, wab_ref, 0)
        _cast_weight_tile(wc_ref, None, wcb_ref, 0)

    a = a_ref[...]
    fa = jnp.dot(a, wab_ref[...], preferred_element_type=F32)
    fc = jnp.dot(a, wcb_ref[...], preferred_element_type=F32)
    o_ref[...] = (_silu(fa) * fc).astype(o_ref.dtype)


def _swiglu_matmul(a, w, layer, *, tm, tn):
    m, kdim = a.shape
    nj = FFN_HIDDEN // tn
    return pl.pallas_call(
        _swiglu_kernel,
        grid=(nj, m // tm),
        in_specs=[pl.BlockSpec((tm, kdim), lambda j, i: (i, 0)),
                  pl.BlockSpec((None, kdim, tn), lambda j, i: (layer, 0, j)),
                  pl.BlockSpec((None, kdim, tn), lambda j, i: (layer, 0, j + nj))],
        out_specs=pl.BlockSpec((tm, tn), lambda j, i: (i, j)),
        out_shape=jax.ShapeDtypeStruct((m, FFN_HIDDEN), BF16),
        scratch_shapes=[pltpu.VMEM((kdim, tn), BF16), pltpu.VMEM((kdim, tn), BF16)],
        compiler_params=_params(("parallel", "arbitrary")),
        name="ffn_in_swiglu",
    )(a, w, w)


def _split3(a):
    hi = a.astype(BF16)
    r1 = a - hi.astype(F32)
    mid = r1.astype(BF16)
    lo = (r1 - mid.astype(F32)).astype(BF16)
    return hi, mid, lo


def _exact_right(a, onehot):
    return sum(jnp.dot(p, onehot, preferred_element_type=F32) for p in _split3(a))


def _exact_left(onehot, b):
    return sum(jnp.dot(onehot, p, preferred_element_type=F32) for p in _split3(b))


def _dot_nt(a, b):
    return lax.dot_general(a, b, (((1,), (1,)), ((), ())), preferred_element_type=F32)


def _dot_tn(a, b):
    return lax.dot_general(a, b, (((0,), (0,)), ((), ())), preferred_element_type=F32)


def _conv_silu(cur, prev, cw, cb):
    row = lax.broadcasted_iota(jnp.int32, cur.shape, 0)
    acc = cur * cw[CONV_W - 1:CONV_W, :] + cb
    for s in range(1, CONV_W):
        shifted = jnp.where(row < s, pltpu.roll(prev, s, 0), pltpu.roll(cur, s, 0))
        acc = acc + shifted * cw[CONV_W - 1 - s:CONV_W - s, :]
    return _silu(acc)


def _ssd_kernel(x_ref, b_ref, c_ref, z_ref, dt_ref, hist_ref, h0_ref,
                cwx_ref, cwb_ref, cwc_ref, cbx_ref, cbb_ref, cbc_ref,
                dtb_ref, alog_p_ref, alog_l_ref, dskip_ref, normw_ref, exp_p_ref, exp_l_ref,
                *rest, L, nc, aliased):
    if aliased:
        rest = rest[1:]
    y_ref, hout_ref, px_ref, pb_ref, pc_ref, ht_ref = rest
    c = pl.program_id(1)
    lw = SSM_HEADS * L

    @pl.when(c == 0)
    def _():
        px_ref[...] = jnp.zeros_like(px_ref)
        pb_ref[...] = jnp.zeros_like(pb_ref)
        pc_ref[...] = jnp.zeros_like(pc_ref)
        hist = hist_ref[0]
        px_ref[L - SUBLANES:L, :] = hist[:, :D_SSM]
        pb_ref[L - SUBLANES:L, :] = hist[:, D_SSM:D_SSM + BC_WIDTH]
        pc_ref[L - SUBLANES:L, :] = hist[:, D_SSM + BC_WIDTH:]
        ht_ref[...] = h0_ref[0].T

    x_in = x_ref[...]
    b_in = b_ref[...]
    c_in = c_ref[...]
    xs = _conv_silu(x_in, px_ref[...], cwx_ref[...], cbx_ref[...])
    bm = _conv_silu(b_in, pb_ref[...], cwb_ref[...], cbb_ref[...]).astype(BF16)
    cm = _conv_silu(c_in, pc_ref[...], cwc_ref[...], cbc_ref[...]).astype(BF16)
    px_ref[...] = x_in
    pb_ref[...] = b_in
    pc_ref[...] = c_in

    dtr = dt_ref[...] + dtb_ref[...]
    dt = jnp.maximum(dtr, 0.0) + jnp.log1p(jnp.exp(-jnp.abs(dtr)))

    tri = (lax.broadcasted_iota(jnp.int32, (L, L), 1)
           <= lax.broadcasted_iota(jnp.int32, (L, L), 0)).astype(BF16)
    ones_ll = jnp.ones((L, L), BF16)

    dt_p = _exact_right(dt, exp_p_ref[...])
    a_p = -jnp.exp(alog_p_ref[...])
    acum_p = _exact_left(tri, dt_p * a_p)
    last_p = acum_p[L - 1:L, :]
    if L == SSM_HEAD_DIM:
        dt_l, acum_l = dt_p, acum_p
    else:
        dt_l = _exact_right(dt, exp_l_ref[...])
        acum_l = _exact_left(tri, dt_l * (-jnp.exp(alog_l_ref[...])))

    rowi = lax.broadcasted_iota(jnp.int32, (L, lw), 0)
    colj = lax.broadcasted_iota(jnp.int32, (L, lw), 1) % L
    diag = colj == rowi
    causal = colj <= rowi
    acum_row = _exact_left(ones_ll, jnp.where(diag, acum_l, 0.0))
    dt_row = _exact_left(ones_ll, jnp.where(diag, dt_l, 0.0))
    decay = jnp.where(causal, jnp.exp(jnp.where(causal, acum_l - acum_row, 0.0)), 0.0)

    heads_per_group = SSM_HEADS // SSM_GROUPS
    cb = jnp.concatenate(
        [_dot_nt(cm[:, g * SSM_STATE:(g + 1) * SSM_STATE],
                 jnp.concatenate([bm[:, g * SSM_STATE:(g + 1) * SSM_STATE]] * heads_per_group, axis=0))
         for g in range(SSM_GROUPS)], axis=1)
    m_all = (cb * decay * dt_row).astype(BF16)

    hp = LANES // L
    slab = hp * SSM_HEAD_DIM
    xs_b = xs.astype(BF16)
    lane_head = lax.broadcasted_iota(jnp.int32, (L, slab), 1) // SSM_HEAD_DIM
    y_parts = []
    for q in range(lw // LANES):
        xq = xs_b[:, q * slab:(q + 1) * slab]
        bd = jnp.concatenate([jnp.where(lane_head == h, xq, jnp.zeros_like(xq)) for h in range(hp)], axis=0)
        y_parts.append(jnp.dot(m_all[:, q * LANES:(q + 1) * LANES], bd, preferred_element_type=F32))
    y = jnp.concatenate(y_parts, axis=1)

    ht = ht_ref[...]
    ht_b = ht.astype(BF16)
    to_end = jnp.exp(last_p - acum_p) * dt_p
    xw = (xs * to_end).astype(BF16)
    y_inter = []
    st_parts = []
    for g in range(SSM_GROUPS):
        gs = slice(g * SSM_GROUP_WIDTH, (g + 1) * SSM_GROUP_WIDTH)
        ns = slice(g * SSM_STATE, (g + 1) * SSM_STATE)
        y_inter.append(jnp.dot(cm[:, ns], ht_b[:, gs], preferred_element_type=F32))
        st_parts.append(_dot_tn(bm[:, ns], xw[:, gs]))
    y = y + jnp.concatenate(y_inter, axis=1) * jnp.exp(acum_p) + dskip_ref[...] * xs
    ht_new = jnp.exp(last_p) * ht + jnp.concatenate(st_parts, axis=1)
    ht_ref[...] = ht_new

    t = y * _silu(z_ref[...])
    outs = []
    for g in range(SSM_GROUPS):
        tg = t[:, g * SSM_GROUP_WIDTH:(g + 1) * SSM_GROUP_WIDTH]
        outs.append(tg * lax.rsqrt(jnp.mean(tg * tg, axis=-1, keepdims=True) + EPS))
    y_ref[...] = (jnp.concatenate(outs, axis=1) * normw_ref[...]).astype(y_ref.dtype)

    @pl.when(c == nc - 1)
    def _():
        hout_ref[0] = ht_new.T


def _ssd_call(u, dt_raw, hist8, h0, layer, lp, *, L, nb, nc, row0, prev_out):
    rb0 = row0 // L
    aliased = prev_out is not None

    def rows(width, colblock):
        return pl.BlockSpec((L, width), lambda b, c: (rb0 + b * nc + c, colblock))

    def const(shape):
        return pl.BlockSpec(shape, lambda b, c: tuple(0 for _ in shape))

    lw = SSM_HEADS * L
    in_specs = [rows(D_SSM, H_X // D_SSM), rows(BC_WIDTH, H_B // BC_WIDTH), rows(BC_WIDTH, H_C // BC_WIDTH),
                rows(D_SSM, H_Z // D_SSM),
                pl.BlockSpec((L, LANES), lambda b, c: (rb0 + b * nc + c, 0)),
                pl.BlockSpec((None, 1, SUBLANES, CONV_DIM), lambda b, c: (layer, b, 0, 0)),
                pl.BlockSpec((None, 1, D_SSM, SSM_STATE), lambda b, c: (layer, b, 0, 0)),
                const((CONV_W, D_SSM)), const((CONV_W, BC_WIDTH)), const((CONV_W, BC_WIDTH)),
                const((1, D_SSM)), const((1, BC_WIDTH)), const((1, BC_WIDTH)),
                const((1, LANES)), const((1, D_SSM)), const((1, lw)), const((1, D_SSM)), const((1, D_SSM)),
                const((LANES, D_SSM)), const((LANES, lw))]
    args = [u, u, u, u, dt_raw, hist8, h0,
            lp["conv_wx"], lp["conv_wb"], lp["conv_wc"], lp["conv_bx"], lp["conv_bb"], lp["conv_bc"],
            lp["dt_bias"], lp["a_log_p"], lp["a_log_l%d" % L], lp["d_skip_p"], lp["ssm_norm"],
            _expand_onehot(SSM_HEAD_DIM), _expand_onehot(L)]
    aliases = {}
    if aliased:
        in_specs.append(pl.BlockSpec(memory_space=pl.ANY))
        args.append(prev_out)
        aliases = {len(args) - 1: 0}
    y, hout = pl.pallas_call(
        functools.partial(_ssd_kernel, L=L, nc=nc, aliased=aliased),
        grid=(nb, nc),
        in_specs=in_specs,
        out_specs=[pl.BlockSpec((L, D_SSM), lambda b, c: (rb0 + b * nc + c, 0)),
                   pl.BlockSpec((1, D_SSM, SSM_STATE), lambda b, c: (b, 0, 0))],
        out_shape=[jax.ShapeDtypeStruct((M_ALL, D_SSM), BF16),
                   jax.ShapeDtypeStruct((nb, D_SSM, SSM_STATE), F32)],
        scratch_shapes=[pltpu.VMEM((L, D_SSM), F32), pltpu.VMEM((L, BC_WIDTH), F32),
                        pltpu.VMEM((L, BC_WIDTH), F32), pltpu.VMEM((SSM_STATE, D_SSM), F32)],
        input_output_aliases=aliases,
        compiler_params=_params(("parallel", "arbitrary")),
        name="ssd_L%d" % L,
    )(*args)
    return y, hout


def _expand_onehot(per_head):
    e = np.zeros((LANES, SSM_HEADS * per_head), np.float32)
    for h in range(SSM_HEADS):
        e[h, h * per_head:(h + 1) * per_head] = 1.0
    return jnp.asarray(e, BF16)


def _ret_consts(L):
    lam = np.log1p(-np.exp2(-5.0 - np.arange(RET_HEADS, dtype=np.float64)))
    i = np.arange(L, dtype=np.float64)
    diff = i[:, None] - i[None, :]
    dmat = np.where(diff >= 0, np.exp(lam[:, None, None] * np.maximum(diff, 0.0)), 0.0)
    to_end = np.exp(lam[:, None] * (L - 1 - i)[None, :])
    q_dec = np.exp(lam[:, None] * (i + 1.0)[None, :])
    chunk_decay = np.exp(lam * L)
    expand = lambda t: np.repeat(t.T[:, :, None], RET_DK, axis=2).reshape(L, RET_HEADS * RET_DK)
    return (jnp.asarray(dmat, F32), jnp.asarray(expand(q_dec), F32), jnp.asarray(expand(to_end), F32),
            [float(v) for v in chunk_decay])


def _ret_kernel(q_ref, k_ref, v_ref, g_ref, cos_ref, sin_ref, qdec_ref, kend_ref, dmat_ref, s0_ref,
                *rest, L, nc, chunk_decay, aliased):
    if aliased:
        rest = rest[1:]
    y_ref, sout_ref, s_ref = rest
    c = pl.program_id(1)

    @pl.when(c == 0)
    def _():
        s_ref[...] = s0_ref[0]

    cos = cos_ref[...]
    sin = sin_ref[...]
    half = RET_DK // 2
    def first_stage(h):
        ks = slice(h * RET_DK, (h + 1) * RET_DK)
        qh = q_ref[:, ks]
        kh = k_ref[:, ks]
        qr = qh * cos + pltpu.roll(qh, half, 1) * sin
        kr = (kh * cos + pltpu.roll(kh, half, 1) * sin) * (RET_DK ** -0.5)
        vh = v_ref[:, h * RET_DV:(h + 1) * RET_DV].astype(BF16)
        s_prev = s_ref[h]
        qk = _dot_nt(qr.astype(BF16), kr.astype(BF16))
        o_state = jnp.dot((qr * qdec_ref[:, ks]).astype(BF16), s_prev.astype(BF16),
                          preferred_element_type=F32)
        kv = _dot_tn((kr * kend_ref[:, ks]).astype(BF16), vh)
        s_ref[h] = chunk_decay[h] * s_prev + kv
        return qk, o_state, vh

    def second_stage(h, qk, o_state, vh):
        vs = slice(h * RET_DV, (h + 1) * RET_DV)
        s = qk * dmat_ref[h]
        o = jnp.dot(s.astype(BF16), vh, preferred_element_type=F32) + o_state
        o = o * lax.rsqrt(jnp.mean(o * o, axis=-1, keepdims=True) + EPS)
        y_ref[:, vs] = (o * _silu(g_ref[:, vs])).astype(y_ref.dtype)

    pending = first_stage(0)
    for h in range(RET_HEADS):
        nxt = first_stage(h + 1) if h + 1 < RET_HEADS else None
        second_stage(h, *pending)
        pending = nxt

    @pl.when(c == nc - 1)
    def _():
        sout_ref[0] = s_ref[...]


def _ret_call(u, cos_t, sin_t, s0, layer, *, L, nb, nc, row0, prev_out):
    rb0 = row0 // L
    aliased = prev_out is not None
    dmat, qdec, kend, chunk_decay = _ret_consts(L)
    hk = RET_HEADS * RET_DK
    hv = RET_HEADS * RET_DV

    def rows(width, colblock):
        return pl.BlockSpec((L, width), lambda b, c: (rb0 + b * nc + c, colblock))

    in_specs = [rows(hk, T_RQ // hk), rows(hk, T_RK // hk), rows(hv, T_RV // hv), rows(hv, T_RG // hv),
                pl.BlockSpec((L, RET_DK), lambda b, c: (c, 0)),
                pl.BlockSpec((L, RET_DK), lambda b, c: (c, 0)),
                pl.BlockSpec((L, hk), lambda b, c: (0, 0)),
                pl.BlockSpec((L, hk), lambda b, c: (0, 0)),
                pl.BlockSpec((RET_HEADS, L, L), lambda b, c: (0, 0, 0)),
                pl.BlockSpec((None, 1, RET_HEADS, RET_DK, RET_DV), lambda b, c: (layer, b, 0, 0, 0))]
    args = [u, u, u, u, cos_t, sin_t, qdec, kend, dmat, s0]
    aliases = {}
    if aliased:
        in_specs.append(pl.BlockSpec(memory_space=pl.ANY))
        args.append(prev_out)
        aliases = {len(args) - 1: 0}
    y, sout = pl.pallas_call(
        functools.partial(_ret_kernel, L=L, nc=nc, chunk_decay=chunk_decay, aliased=aliased),
        grid=(nb, nc),
        in_specs=in_specs,
        out_specs=[pl.BlockSpec((L, hv), lambda b, c: (rb0 + b * nc + c, 0)),
                   pl.BlockSpec((1, RET_HEADS, RET_DK, RET_DV), lambda b, c: (b, 0, 0, 0))],
        out_shape=[jax.ShapeDtypeStruct((M_ALL, hv), BF16),
                   jax.ShapeDtypeStruct((nb, RET_HEADS, RET_DK, RET_DV), F32)],
        scratch_shapes=[pltpu.VMEM((RET_HEADS, RET_DK, RET_DV), F32)],
        input_output_aliases=aliases,
        compiler_params=_params(("parallel", "arbitrary")),
        name="retention_L%d" % L,
    )(*args)
    return y, sout


def _rope_tables(pos):
    half = RET_DK // 2
    inv = ROPE_BASE ** (-jnp.arange(half, dtype=F32) * 2.0 / RET_DK)
    ang = pos.astype(F32)[:, None] * inv[None, :]
    cos = jnp.cos(ang)
    sin = jnp.sin(ang)
    return jnp.concatenate([cos, cos], axis=1), jnp.concatenate([-sin, sin], axis=1)


def _head_norm(xh, g):
    return xh * lax.rsqrt(jnp.mean(xh * xh, axis=-1, keepdims=True) + EPS) * g


def _kv_prep_kernel(k_ref, v_ref, gk_ref, knf_ref, knb_ref, vb_ref, *, pad_first):
    def body():
        for h in range(ATT_HEADS):
            hs = slice(h * ATT_HEAD_DIM, (h + 1) * ATT_HEAD_DIM)
            kn = _head_norm(k_ref[:, hs], gk_ref[...])
            knf_ref[0, :, hs] = kn
            knb_ref[0, :, hs] = kn.astype(BF16)
        vb_ref[0] = v_ref[...].astype(BF16)

    if pad_first:
        i = pl.program_id(1)

        @pl.when(i == 0)
        def _():
            knb_ref[...] = jnp.zeros_like(knb_ref)
            vb_ref[...] = jnp.zeros_like(vb_ref)

        @pl.when(i > 0)
        def _():
            body()
    else:
        body()


def _kv_prep_prompt(u, gk):
    blk = WINDOW
    nblk = SEQ // blk
    dm = ATT_HEADS * ATT_HEAD_DIM

    def rows(colblock):
        return pl.BlockSpec((blk, dm), lambda b, i: (b * nblk + jnp.maximum(i - 1, 0), colblock))

    return pl.pallas_call(
        functools.partial(_kv_prep_kernel, pad_first=True),
        grid=(BATCH, nblk + 1),
        in_specs=[rows(T_AK // dm), rows(T_AV // dm), pl.BlockSpec((1, ATT_HEAD_DIM), lambda b, i: (0, 0))],
        out_specs=[pl.BlockSpec((1, blk, dm), lambda b, i: (b, 0, 0)),
                   pl.BlockSpec((1, blk, dm), lambda b, i: (b, i, 0)),
                   pl.BlockSpec((1, blk, dm), lambda b, i: (b, i, 0))],
        out_shape=[jax.ShapeDtypeStruct((BATCH, blk, dm), F32),
                   jax.ShapeDtypeStruct((BATCH, SEQ + WINDOW, dm), BF16),
                   jax.ShapeDtypeStruct((BATCH, SEQ + WINDOW, dm), BF16)],
        compiler_params=_params(("parallel", "arbitrary")),
        name="kv_prep_prompt",
    )(u, u, gk)


def _kv_prep_sample(u, gk):
    dm = ATT_HEADS * ATT_HEAD_DIM
    rb = M_PROMPT // M_SAMPLE
    return pl.pallas_call(
        functools.partial(_kv_prep_kernel, pad_first=False),
        grid=(1,),
        in_specs=[pl.BlockSpec((M_SAMPLE, dm), lambda i: (rb, T_AK // dm)),
                  pl.BlockSpec((M_SAMPLE, dm), lambda i: (rb, T_AV // dm)),
                  pl.BlockSpec((1, ATT_HEAD_DIM), lambda i: (0, 0))],
        out_specs=[pl.BlockSpec((1, M_SAMPLE, dm), lambda i: (0, 0, 0)),
                   pl.BlockSpec((1, M_SAMPLE, dm), lambda i: (0, 0, 0)),
                   pl.BlockSpec((1, M_SAMPLE, dm), lambda i: (0, 0, 0))],
        out_shape=[jax.ShapeDtypeStruct((1, M_SAMPLE, dm), F32),
                   jax.ShapeDtypeStruct((1, M_SAMPLE, dm), BF16),
                   jax.ShapeDtypeStruct((1, M_SAMPLE, dm), BF16)],
        compiler_params=_params(("arbitrary",)),
        name="kv_prep_sample",
    )(u, u, gk)


def _scores(qn, k1, k2, bias1, bias2, valid1):
    scale = ATT_HEAD_DIM ** -0.5
    s1 = _dot_nt(qn, k1) * scale + bias1
    if valid1 is not None:
        s1 = jnp.where(valid1, s1, NEG_INF)
    s2 = _dot_nt(qn, k2) * scale + bias2
    return s1, s2


def _softmax_pv(s1, s2, v1, v2):
    m = jnp.maximum(jnp.max(s1, axis=-1, keepdims=True), jnp.max(s2, axis=-1, keepdims=True))
    p1 = jnp.exp(s1 - m)
    p2 = jnp.exp(s2 - m)
    inv = 1.0 / (jnp.sum(p1, axis=-1, keepdims=True) + jnp.sum(p2, axis=-1, keepdims=True))
    return (jnp.dot((p1 * inv).astype(BF16), v1, preferred_element_type=F32)
            + jnp.dot((p2 * inv).astype(BF16), v2, preferred_element_type=F32))


def _attend(qn, k1, v1, k2, v2, bias1, bias2, valid1):
    s1, s2 = _scores(qn, k1, k2, bias1, bias2, valid1)
    return _softmax_pv(s1, s2, v1, v2)


ATTN_HEADS_AHEAD = 3


def _attn_prompt_kernel(q_ref, k_ref, v_ref, b1_ref, b2_ref, gq_ref, y_ref):
    c = pl.program_id(1)
    base = pl.multiple_of(c * CHUNK, CHUNK)
    col = lax.broadcasted_iota(jnp.int32, (CHUNK, WINDOW), 1)
    valid1 = col + c * CHUNK >= WINDOW
    heads = [slice(h * ATT_HEAD_DIM, (h + 1) * ATT_HEAD_DIM) for h in range(ATT_HEADS)]

    def scores(h):
        hs = heads[h]
        qn = _head_norm(q_ref[:, hs], gq_ref[...]).astype(BF16)
        return _scores(qn, k_ref[0, pl.ds(base, WINDOW), hs], k_ref[0, pl.ds(base + WINDOW, CHUNK), hs],
                       b1_ref[h], b2_ref[h], valid1)

    pending = [scores(h) for h in range(ATTN_HEADS_AHEAD)]
    for h in range(ATT_HEADS):
        if h + ATTN_HEADS_AHEAD < ATT_HEADS:
            pending.append(scores(h + ATTN_HEADS_AHEAD))
        s1, s2 = pending.pop(0)
        hs = heads[h]
        o = _softmax_pv(s1, s2, v_ref[0, pl.ds(base, WINDOW), hs], v_ref[0, pl.ds(base + WINDOW, CHUNK), hs])
        y_ref[:, hs] = o.astype(y_ref.dtype)


def _attn_prompt_call(u, knb, vb, bias1, bias2, gq):
    dm = ATT_HEADS * ATT_HEAD_DIM
    nc = SEQ // CHUNK
    return pl.pallas_call(
        _attn_prompt_kernel,
        grid=(BATCH, nc),
        in_specs=[pl.BlockSpec((CHUNK, dm), lambda b, c: (b * nc + c, T_AQ // dm)),
                  pl.BlockSpec((1, SEQ + WINDOW, dm), lambda b, c: (b, 0, 0)),
                  pl.BlockSpec((1, SEQ + WINDOW, dm), lambda b, c: (b, 0, 0)),
                  pl.BlockSpec((ATT_HEADS, CHUNK, WINDOW), lambda b, c: (0, 0, 0)),
                  pl.BlockSpec((ATT_HEADS, CHUNK, CHUNK), lambda b, c: (0, 0, 0)),
                  pl.BlockSpec((1, ATT_HEAD_DIM), lambda b, c: (0, 0))],
        out_specs=pl.BlockSpec((CHUNK, dm), lambda b, c: (b * nc + c, 0)),
        out_shape=jax.ShapeDtypeStruct((M_ALL, dm), BF16),
        compiler_params=_params(("parallel", "arbitrary")),
        name="attn_prompt",
    )(u, knb, vb, bias1, bias2, gq)


def _attn_sample_kernel(q_ref, kh_ref, vh_ref, kn_ref, vn_ref, b1_ref, b2_ref, gq_ref, prev_ref, y_ref):
    del prev_ref
    heads = [slice(h * ATT_HEAD_DIM, (h + 1) * ATT_HEAD_DIM) for h in range(ATT_HEADS)]
    hist = kh_ref.shape[1]
    qall = jnp.concatenate([_head_norm(q_ref[:, hs], gq_ref[...]).astype(BF16) for hs in heads], axis=0)
    k1 = kh_ref[0].reshape(hist * ATT_HEADS, ATT_HEAD_DIM).astype(BF16)
    v1 = vh_ref[0].reshape(hist * ATT_HEADS, ATT_HEAD_DIM).astype(BF16)
    k2 = jnp.concatenate([kn_ref[:, hs] for hs in heads], axis=0)
    v2 = jnp.concatenate([vn_ref[:, hs] for hs in heads], axis=0)
    o = _attend(qall, k1, v1, k2, v2, b1_ref[...], b2_ref[...], None)
    for h, hs in enumerate(heads):
        y_ref[:, hs] = o[h * DEC_SEQ:(h + 1) * DEC_SEQ].astype(y_ref.dtype)


def _head_masked_bias(tbl1, tbl2):
    nh, nq, hist = tbl1.shape
    eye = jnp.eye(nh, dtype=bool)
    b1 = jnp.where(eye[:, None, None, :], tbl1[:, :, :, None], NEG_INF).reshape(nh * nq, hist * nh)
    b2 = jnp.where(eye[:, None, :, None], tbl2[:, :, None, :], NEG_INF).reshape(nh * nq, nh * tbl2.shape[2])
    return b1, b2


def _attn_sample_call(u, k_hist, v_hist, layer, kn_new, vn_new, bias1, bias2, gq, prev_out):
    dm = ATT_HEADS * ATT_HEAD_DIM
    rb0 = M_PROMPT // DEC_SEQ
    hist = k_hist.shape[2]
    cache_spec = pl.BlockSpec((None, 1, hist, ATT_HEADS, ATT_HEAD_DIM), lambda b: (layer, b, 0, 0, 0))
    return pl.pallas_call(
        _attn_sample_kernel,
        grid=(DEC_BATCH,),
        in_specs=[pl.BlockSpec((DEC_SEQ, dm), lambda b: (rb0 + b, T_AQ // dm)),
                  cache_spec,
                  cache_spec,
                  pl.BlockSpec((DEC_SEQ, dm), lambda b: (b, 0)),
                  pl.BlockSpec((DEC_SEQ, dm), lambda b: (b, 0)),
                  pl.BlockSpec((ATT_HEADS * DEC_SEQ, hist * ATT_HEADS), lambda b: (0, 0),
                               pipeline_mode=pl.Buffered(1)),
                  pl.BlockSpec((ATT_HEADS * DEC_SEQ, ATT_HEADS * DEC_SEQ), lambda b: (0, 0)),
                  pl.BlockSpec((1, ATT_HEAD_DIM), lambda b: (0, 0)),
                  pl.BlockSpec(memory_space=pl.ANY)],
        out_specs=pl.BlockSpec((DEC_SEQ, dm), lambda b: (rb0 + b, 0)),
        out_shape=jax.ShapeDtypeStruct((M_ALL, dm), BF16),
        input_output_aliases={8: 0},
        compiler_params=_params(("arbitrary",)),
        name="attn_sample",
    )(u, k_hist, v_hist, kn_new, vn_new, bias1, bias2, gq, prev_out)


def _bias_table(rel_bias):
    kbn = WINDOW + CHUNK
    period = kbn + CHUNK
    m = np.arange(period)
    d = np.where(m < kbn, m, m - period)
    idx = np.clip(WINDOW - d, -REL_CLIP, REL_CLIP) + REL_CLIP
    vec = rel_bias[:, idx]
    rows = jnp.tile(vec, (1, CHUNK))[:, :CHUNK * (period - 1)].reshape(ATT_HEADS, CHUNK, period - 1)
    return rows[:, :, :kbn]


def _layer_params(l, norm_mix, w_in, conv_w, conv_b, dt_bias, a_log, d_skip, ssm_norm, q_norm, k_norm,
                  rel_bias, w_br_ssm, w_br_ret, w_br_att, w_out, norm_ffn, w_ffn_in, w_ffn_out):
    assert w_in.shape[-1] == N_IN
    cw = conv_w[l]
    cb = conv_b[l].reshape(1, CONV_DIM)
    lp = {
        "norm_mix": norm_mix[l], "norm_ffn": norm_ffn[l],
        "conv_wx": cw[:, :D_SSM], "conv_wb": cw[:, D_SSM:D_SSM + BC_WIDTH], "conv_wc": cw[:, D_SSM + BC_WIDTH:],
        "conv_bx": cb[:, :D_SSM], "conv_bb": cb[:, D_SSM:D_SSM + BC_WIDTH], "conv_bc": cb[:, D_SSM + BC_WIDTH:],
        "dt_bias": jnp.pad(dt_bias[l], (0, LANES - SSM_HEADS)).reshape(1, LANES),
        "a_log_p": jnp.repeat(a_log[l], SSM_HEAD_DIM).reshape(1, D_SSM),
        "a_log_l%d" % CHUNK: jnp.repeat(a_log[l], CHUNK).reshape(1, SSM_HEADS * CHUNK),
        "a_log_l%d" % DEC_SEQ: jnp.repeat(a_log[l], DEC_SEQ).reshape(1, SSM_HEADS * DEC_SEQ),
        "d_skip_p": jnp.repeat(d_skip[l], SSM_HEAD_DIM).reshape(1, D_SSM),
        "ssm_norm": ssm_norm[l].reshape(1, D_SSM),
        "q_norm": q_norm[l].reshape(1, ATT_HEAD_DIM), "k_norm": k_norm[l].reshape(1, ATT_HEAD_DIM),
        "rel_bias": rel_bias[l],
    }
    return lp


def _row_block(t, row0, rows, col0, width):
    return lax.slice(t, (row0, col0), (row0 + rows, col0 + width))


def _layer(x, l, lp, w, state, tables):
    cache_k, cache_v, state_ret, state_ssm4, hist8 = state
    hn = _rmsnorm(x, lp["norm_mix"])
    u_head = _matmul(hn, w["w_in"], l, n=HEAD_WIDTH, out_dtype=F32, tm=1088, tn=1024, name="in_proj_head")
    dt_raw = _matmul(hn, w["w_in"], l, n=LANES, out_dtype=F32, tm=1088, tn=LANES, w_col0=DT_COL0,
                     name="in_proj_dt")
    u_tail = _matmul(hn, w["w_in"], l, n=TAIL_WIDTH, out_dtype=F32, tm=1088, tn=1024, w_col0=TAIL_COL0,
                     name="in_proj_tail")

    ys, ssm_p = _ssd_call(u_head, dt_raw, tables["zeros_hist"], tables["zeros_h"], 0, lp, L=CHUNK, nb=BATCH,
                          nc=SEQ // CHUNK, row0=0, prev_out=None)
    ys, ssm_s = _ssd_call(u_head, dt_raw, hist8, state_ssm4, l, lp, L=DEC_SEQ, nb=DEC_BATCH, nc=1,
                          row0=M_PROMPT, prev_out=ys)

    yr, ret_p = _ret_call(u_tail, tables["cos_p"], tables["sin_p"], tables["zeros_s"], 0, L=CHUNK, nb=BATCH,
                          nc=SEQ // CHUNK, row0=0, prev_out=None)
    yr, ret_s = _ret_call(u_tail, tables["cos_s"], tables["sin_s"], state_ret, l, L=DEC_SEQ, nb=DEC_BATCH,
                          nc=1, row0=M_PROMPT, prev_out=yr)

    dm = ATT_HEADS * ATT_HEAD_DIM
    knf_p, knb_p, vb_p = _kv_prep_prompt(u_tail, lp["k_norm"])
    knf_s, knb_s, vb_s = _kv_prep_sample(u_tail, lp["k_norm"])
    hist = cache_k.shape[2]
    assert hist == WINDOW
    tbl = _bias_table(lp["rel_bias"])
    ya = _attn_prompt_call(u_tail, knb_p, vb_p, tbl[:, :, :WINDOW], tbl[:, :, WINDOW:], lp["q_norm"])
    bias1_s, bias2_s = _head_masked_bias(tbl[:, :DEC_SEQ, :hist], tbl[:, :DEC_SEQ, hist:hist + DEC_SEQ])
    ya = _attn_sample_call(u_tail, cache_k, cache_v, l, knb_s[0], vb_s[0], bias1_s, bias2_s, lp["q_norm"], ya)

    m1 = _matmul(ys, w["w_br_ssm"], l, n=D_MODEL, out_dtype=F32, tm=544, tn=1024, gate=u_tail, gate_col0=T_GA,
                 name="merge_ssm")
    m2 = _matmul(yr, w["w_br_ret"], l, n=D_MODEL, out_dtype=F32, tm=544, tn=1024, gate=u_tail, gate_col0=T_GB,
                 addend=m1, name="merge_ret")
    mm = _matmul(ya, w["w_br_att"], l, n=D_MODEL, out_dtype=BF16, tm=544, tn=1024, gate=u_tail,
                 gate_col0=T_GC, addend=m2, name="merge_att")
    x1 = _matmul(mm, w["w_out"], l, n=D_MODEL, out_dtype=F32, tm=544, tn=1024, addend=x, name="out_proj")

    h2 = _rmsnorm(x1, lp["norm_ffn"])
    hh = _swiglu_matmul(h2, w["w_ffn_in"], l, tm=1088, tn=512)
    x2 = _matmul(hh, w["w_ffn_out"], l, n=D_MODEL, out_dtype=F32, tm=544, tn=512, addend=x1, name="ffn_out")

    keep = min(WINDOW, SEQ)
    nconv = CONV_W - 1

    def conv_rows(row0):
        return jnp.concatenate([_row_block(u_head, row0, nconv, H_X, D_SSM),
                                _row_block(u_head, row0, nconv, H_B, BC_WIDTH),
                                _row_block(u_head, row0, nconv, H_C, BC_WIDTH)], axis=-1)

    pv = jnp.stack([_row_block(u_tail, (b + 1) * SEQ - keep, keep, T_AV, dm) for b in range(BATCH)])
    pconv = jnp.stack([conv_rows((b + 1) * SEQ - nconv) for b in range(BATCH)])
    sv = _row_block(u_tail, M_PROMPT, M_SAMPLE, T_AV, dm)
    s_xbc = jnp.concatenate([_row_block(u_head, M_PROMPT, M_SAMPLE, H_X, D_SSM),
                             _row_block(u_head, M_PROMPT, M_SAMPLE, H_B, BC_WIDTH),
                             _row_block(u_head, M_PROMPT, M_SAMPLE, H_C, BC_WIDTH)], axis=-1)
    sconv = s_xbc.reshape(DEC_BATCH, DEC_SEQ, CONV_DIM)[:, DEC_SEQ - nconv:]

    outs_p = (knf_p.reshape(BATCH, keep, ATT_HEADS, ATT_HEAD_DIM),
              pv.reshape(BATCH, keep, ATT_HEADS, ATT_HEAD_DIM),
              ret_p,
              ssm_p.reshape(BATCH, SSM_HEADS, SSM_HEAD_DIM, SSM_STATE),
              pconv)
    outs_s = (knf_s.reshape(DEC_BATCH, DEC_SEQ, ATT_HEADS, ATT_HEAD_DIM),
              sv.reshape(DEC_BATCH, DEC_SEQ, ATT_HEADS, ATT_HEAD_DIM),
              ret_s,
              ssm_s.reshape(DEC_BATCH, SSM_HEADS, SSM_HEAD_DIM, SSM_STATE),
              sconv)
    return x2, outs_p, outs_s


def kernel(x_prompt, x_sample, cache_attn_k, cache_attn_v, state_ret, state_ssm, state_conv,
           norm_mix, w_in, conv_w, conv_b, dt_bias, a_log, d_skip, ssm_norm, q_norm, k_norm,
           rel_bias, w_br_ssm, w_br_ret, w_br_att, w_out, norm_ffn, w_ffn_in, w_ffn_out):
    x = jnp.concatenate([x_prompt.reshape(M_PROMPT, D_MODEL), x_sample.reshape(M_SAMPLE, D_MODEL)], axis=0)
    cos_p, sin_p = _rope_tables(jnp.arange(SEQ))
    cos_s, sin_s = _rope_tables(PAST_LEN + jnp.arange(DEC_SEQ))
    tables = {"cos_p": cos_p, "sin_p": sin_p, "cos_s": cos_s, "sin_s": sin_s,
              "zeros_hist": jnp.zeros((1, BATCH, SUBLANES, CONV_DIM), F32),
              "zeros_h": jnp.zeros((1, BATCH, D_SSM, SSM_STATE), F32),
              "zeros_s": jnp.zeros((1, BATCH, RET_HEADS, RET_DK, RET_DV), F32)}
    w = {"w_in": w_in, "w_br_ssm": w_br_ssm, "w_br_ret": w_br_ret, "w_br_att": w_br_att, "w_out": w_out,
         "w_ffn_in": w_ffn_in, "w_ffn_out": w_ffn_out}
    hist8 = jnp.pad(state_conv, ((0, 0), (0, 0), (SUBLANES - (CONV_W - 1), 0), (0, 0)))
    state = (cache_attn_k, cache_attn_v, state_ret,
             state_ssm.reshape(DEPTH, DEC_BATCH, D_SSM, SSM_STATE), hist8)
    acc_p = ([], [], [], [], [])
    acc_s = ([], [], [], [], [])
    for l in range(DEPTH):
        lp = _layer_params(l, norm_mix, w_in, conv_w, conv_b, dt_bias, a_log, d_skip, ssm_norm, q_norm,
                           k_norm, rel_bias, w_br_ssm, w_br_ret, w_br_att, w_out, norm_ffn, w_ffn_in,
                           w_ffn_out)
        x, outs_p, outs_s = _layer(x, l, lp, w, state, tables)
        for i in range(5):
            acc_p[i].append(outs_p[i])
            acc_s[i].append(outs_s[i])
    pk, pv, pr, pssm, pconv = [jnp.stack(a) for a in acc_p]
    sk, sv, sr, sssm, sconv = [jnp.stack(a) for a in acc_s]
    yp = x[:M_PROMPT].reshape(BATCH, SEQ, D_MODEL)
    ys = x[M_PROMPT:].reshape(DEC_BATCH, DEC_SEQ, D_MODEL)
    return (yp, ys, pk, pv, pr, pssm, pconv, sk, sv, sr, sssm, sconv)
```

```python
import functools
import math

import numpy as np
import jax
import jax.numpy as jnp
from jax import lax
from jax.experimental import pallas as pl
from jax.experimental.pallas import tpu as pltpu

F32 = jnp.float32
BF16 = jnp.bfloat16

D_MODEL = 2048
BATCH = 4
SEQ = 2048
DEPTH = 2
DEC_BATCH = 32
DEC_SEQ = 16
PAST_LEN = 1024
CHUNK = 64
EPS = 1e-6
NEG_INF = -1e30

D_SSM = 2048
SSM_HEAD_DIM = 64
SSM_HEADS = 32
SSM_GROUPS = 4
SSM_STATE = 128
SSM_GROUP_WIDTH = D_SSM // SSM_GROUPS
CONV_W = 4
CONV_DIM = D_SSM + 2 * SSM_GROUPS * SSM_STATE
BC_WIDTH = SSM_GROUPS * SSM_STATE

RET_HEADS = 8
RET_DK = 128
RET_DV = 256
ROPE_BASE = 10000.0

ATT_HEADS = 16
ATT_HEAD_DIM = 128
BAND_CHUNKS = 8
WINDOW = BAND_CHUNKS * CHUNK
REL_CLIP = 256

FFN_HIDDEN = 5632

M_PROMPT = BATCH * SEQ
M_SAMPLE = DEC_BATCH * DEC_SEQ
M_ALL = M_PROMPT + M_SAMPLE

LANES = 128
SUBLANES = 8
V7X_VMEM_LIMIT_BYTES = 56 * 1024 * 1024

H_Z, H_X, H_B, H_C = 0, 2048, 4096, 4608
HEAD_WIDTH = 5120
DT_COL0 = 5120
TAIL_COL0 = DT_COL0 + SSM_HEADS
T_RQ, T_RK, T_RV, T_RG, T_AQ, T_AK, T_AV, T_GA, T_GB, T_GC = (
    0, 1024, 2048, 4096, 6144, 8192, 10240, 12288, 14336, 16384)
TAIL_WIDTH = 18432
N_IN = TAIL_COL0 + TAIL_WIDTH


def _params(sem):
    return pltpu.CompilerParams(dimension_semantics=sem, vmem_limit_bytes=V7X_VMEM_LIMIT_BYTES)


def _append_aliased(in_specs, args, out_buffers):
    aliases = {}
    for k, buf in enumerate(out_buffers):
        if buf is not None:
            in_specs.append(pl.BlockSpec(memory_space=pl.ANY))
            args.append(buf)
            aliases[len(args) - 1] = k
    return aliases


def _sigmoid(x):
    return 1.0 / (1.0 + jnp.exp(-x))


def _silu(x):
    return x * _sigmoid(x)


def _rmsnorm_kernel(x_ref, g_ref, o_ref):
    x = x_ref[...]
    ms = jnp.mean(x * x, axis=-1, keepdims=True)
    o_ref[...] = (x * lax.rsqrt(ms + EPS) * g_ref[...]).astype(o_ref.dtype)


def _rmsnorm(x, g, tm=544):
    m, d = x.shape
    return pl.pallas_call(
        _rmsnorm_kernel,
        grid=(m // tm,),
        in_specs=[pl.BlockSpec((tm, d), lambda i: (i, 0)),
                  pl.BlockSpec((1, d), lambda i: (0, 0))],
        out_specs=pl.BlockSpec((tm, d), lambda i: (i, 0)),
        out_shape=jax.ShapeDtypeStruct((m, d), BF16),
        compiler_params=_params(("parallel",)),
        name="rmsnorm",
    )(x, g.reshape(1, d))


W_CAST_ROWS = 256


def _cast_weight_tile(w_ref, wx_ref, wb_ref, shift):
    rows = wb_ref.shape[0]
    for r in range(0, rows - shift, W_CAST_ROWS):
        n = min(W_CAST_ROWS, rows - shift - r)
        wb_ref[r:r + n, :] = w_ref[r + shift:r + shift + n, :].astype(BF16)
    if shift:
        wb_ref[rows - shift:, :] = wx_ref[:shift, :].astype(BF16)


def _mm_kernel(*refs, shift, has_gate, has_add, w_transposed):
    a_ref, w_ref = refs[0], refs[1]
    pos = 2
    wx_ref = gate_ref = add_ref = None
    if shift:
        wx_ref = refs[pos]
        pos += 1
    if has_gate:
        gate_ref = refs[pos]
        pos += 1
    if has_add:
        add_ref = refs[pos]
        pos += 1
    o_ref, wb_ref = refs[pos], refs[pos + 1]

    @pl.when(pl.program_id(1) == 0)
    def _():
        _cast_weight_tile(w_ref, wx_ref, wb_ref, shift)

    if w_transposed:
        r = _dot_nt(a_ref[...], wb_ref[...])
    else:
        r = jnp.dot(a_ref[...], wb_ref[...], preferred_element_type=F32)
    if has_gate:
        r = _sigmoid(gate_ref[...]) * r
    if has_add:
        r = add_ref[...] + r
    o_ref[...] = r.astype(o_ref.dtype)


def _matmul(a, w, layer, *, n, out_dtype, tm, tn, w_col0=0, w_transposed=False, gate=None, gate_col0=0,
            addend=None, name="matmul"):
    m, kdim = a.shape
    wj0 = w_col0 // tn
    shift = w_col0 - wj0 * tn
    gj0 = gate_col0 // tn
    extra = LANES
    assert shift == 0 or (w_transposed and shift % SUBLANES == 0 and shift <= extra and tn % extra == 0)
    if w_transposed:
        w_spec = pl.BlockSpec((None, tn, kdim), lambda j, i: (layer, j + wj0, 0))
        wb_shape = (tn, kdim)
    else:
        w_spec = pl.BlockSpec((None, kdim, tn), lambda j, i: (layer, 0, j + wj0))
        wb_shape = (kdim, tn)
    in_specs = [pl.BlockSpec((tm, kdim), lambda j, i: (i, 0)), w_spec]
    args = [a, w]
    if shift:
        in_specs.append(pl.BlockSpec((None, extra, kdim), lambda j, i: (layer, (j + wj0 + 1) * (tn // extra), 0)))
        args.append(w)
    if gate is not None:
        in_specs.append(pl.BlockSpec((tm, tn), lambda j, i: (i, j + gj0)))
        args.append(gate)
    if addend is not None:
        in_specs.append(pl.BlockSpec((tm, tn), lambda j, i: (i, j)))
        args.append(addend)
    return pl.pallas_call(
        functools.partial(_mm_kernel, shift=shift, has_gate=gate is not None, has_add=addend is not None,
                          w_transposed=w_transposed),
        grid=(n // tn, m // tm),
        in_specs=in_specs,
        out_specs=pl.BlockSpec((tm, tn), lambda j, i: (i, j)),
        out_shape=jax.ShapeDtypeStruct((m, n), out_dtype),
        scratch_shapes=[pltpu.VMEM(wb_shape, BF16)],
        compiler_params=_params(("parallel", "arbitrary")),
        name=name,
    )(*args)


def _swiglu_kernel(a_ref, wa_ref, wc_ref, o_ref, wab_ref, wcb_ref):
    @pl.when(pl.program_id(1) == 0)
    def _():
        _cast_weight_tile(wa_ref, None, wab_ref, 0)
        _cast_weight_tile(wc_ref, None, wcb_ref, 0)

    a = a_ref[...]
    fa = jnp.dot(a, wab_ref[...], preferred_element_type=F32)
    fc = jnp.dot(a, wcb_ref[...], preferred_element_type=F32)
    o_ref[...] = (_silu(fa) * fc).astype(o_ref.dtype)


def _swiglu_matmul(a, w, layer, *, tm, tn):
    m, kdim = a.shape
    nj = FFN_HIDDEN // tn
    return pl.pallas_call(
        _swiglu_kernel,
        grid=(nj, m // tm),
        in_specs=[pl.BlockSpec((tm, kdim), lambda j, i: (i, 0)),
                  pl.BlockSpec((None, kdim, tn), lambda j, i: (layer, 0, j)),
                  pl.BlockSpec((None, kdim, tn), lambda j, i: (layer, 0, j + nj))],
        out_specs=pl.BlockSpec((tm, tn), lambda j, i: (i, j)),
        out_shape=jax.ShapeDtypeStruct((m, FFN_HIDDEN), BF16),
        scratch_shapes=[pltpu.VMEM((kdim, tn), BF16), pltpu.VMEM((kdim, tn), BF16)],
        compiler_params=_params(("parallel", "arbitrary")),
        name="ffn_in_swiglu",
    )(a, w, w)


def _split3(a):
    hi = a.astype(BF16)
    r1 = a - hi.astype(F32)
    mid = r1.astype(BF16)
    lo = (r1 - mid.astype(F32)).astype(BF16)
    return hi, mid, lo


def _exact_right(a, onehot):
    return sum(jnp.dot(p, onehot, preferred_element_type=F32) for p in _split3(a))


def _exact_left(onehot, b):
    return sum(jnp.dot(onehot, p, preferred_element_type=F32) for p in _split3(b))


def _dot_nt(a, b):
    return lax.dot_general(a, b, (((1,), (1,)), ((), ())), preferred_element_type=F32)


def _dot_tn(a, b):
    return lax.dot_general(a, b, (((0,), (0,)), ((), ())), preferred_element_type=F32)


def _conv_silu(ext_ref, cw, cb, L):
    acc = ext_ref[SUBLANES:SUBLANES + L, :] * cw[CONV_W - 1:CONV_W, :] + cb
    for s in range(1, CONV_W):
        acc = acc + ext_ref[SUBLANES - s:SUBLANES - s + L, :] * cw[CONV_W - 1 - s:CONV_W - s, :]
    return _silu(acc)


def _ssd_kernel(x_ref, b_ref, c_ref, z_ref, dt_ref, hist_ref, h0_ref,
                cwx_ref, cwb_ref, cwc_ref, cbx_ref, cbb_ref, cbc_ref,
                dtb_ref, alog_ref, dskip_ref, normw_ref, exp_p_ref, exp_l_ref,
                *rest, L, nc, n_aliased):
    y_ref, hout_ref, ex_ref, eb_ref, ec_ref, ht_ref = rest[n_aliased:]
    c = pl.program_id(1)
    lw = SSM_HEADS * L

    @pl.when(c == 0)
    def _():
        hist = hist_ref[0]
        ex_ref[0:SUBLANES, :] = hist[:, :D_SSM]
        eb_ref[0:SUBLANES, :] = hist[:, D_SSM:D_SSM + BC_WIDTH]
        ec_ref[0:SUBLANES, :] = hist[:, D_SSM + BC_WIDTH:]
        ht_ref[...] = h0_ref[0].T

    @pl.when(c > 0)
    def _():
        ex_ref[0:SUBLANES, :] = ex_ref[L:L + SUBLANES, :]
        eb_ref[0:SUBLANES, :] = eb_ref[L:L + SUBLANES, :]
        ec_ref[0:SUBLANES, :] = ec_ref[L:L + SUBLANES, :]

    ex_ref[SUBLANES:SUBLANES + L, :] = x_ref[...]
    eb_ref[SUBLANES:SUBLANES + L, :] = b_ref[...]
    ec_ref[SUBLANES:SUBLANES + L, :] = c_ref[...]
    xs = _conv_silu(ex_ref, cwx_ref[...], cbx_ref[...], L)
    bm = _conv_silu(eb_ref, cwb_ref[...], cbb_ref[...], L).astype(BF16)
    cm = _conv_silu(ec_ref, cwc_ref[...], cbc_ref[...], L).astype(BF16)

    dtr = dt_ref[...] + dtb_ref[...]
    dt = jnp.maximum(dtr, 0.0) + jnp.log1p(jnp.exp(-jnp.abs(dtr)))

    tri = (lax.broadcasted_iota(jnp.int32, (L, L), 1)
           <= lax.broadcasted_iota(jnp.int32, (L, L), 0)).astype(BF16)
    acum = _exact_left(tri, dt * (-jnp.exp(alog_ref[...])))

    dt_p = _exact_right(dt, exp_p_ref[...])
    acum_p = _exact_right(acum, exp_p_ref[...])
    last_p = acum_p[L - 1:L, :]
    if L == SSM_HEAD_DIM:
        dt_l, acum_l = dt_p, acum_p
    else:
        dt_l = _exact_right(dt, exp_l_ref[...])
        acum_l = _exact_right(acum, exp_l_ref[...])

    rowi = lax.broadcasted_iota(jnp.int32, (L, lw), 0)
    colj = lax.broadcasted_iota(jnp.int32, (L, lw), 1) % L
    diag = colj == rowi
    causal = colj <= rowi
    acum_row = jnp.sum(jnp.where(diag, acum_l, 0.0), axis=0, keepdims=True)
    dt_row = jnp.sum(jnp.where(diag, dt_l, 0.0), axis=0, keepdims=True)
    decay = jnp.where(causal, jnp.exp(jnp.where(causal, acum_l - acum_row, 0.0)), 0.0)

    heads_per_group = SSM_HEADS // SSM_GROUPS
    cb = jnp.concatenate(
        [_dot_nt(cm[:, g * SSM_STATE:(g + 1) * SSM_STATE],
                 jnp.concatenate([bm[:, g * SSM_STATE:(g + 1) * SSM_STATE]] * heads_per_group, axis=0))
         for g in range(SSM_GROUPS)], axis=1)
    m_all = (cb * decay * dt_row).astype(BF16)

    hp = LANES // L
    slab = hp * SSM_HEAD_DIM
    xs_b = xs.astype(BF16)
    lane_head = lax.broadcasted_iota(jnp.int32, (L, slab), 1) // SSM_HEAD_DIM
    y_parts = []
    for q in range(lw // LANES):
        xq = xs_b[:, q * slab:(q + 1) * slab]
        bd = jnp.concatenate([jnp.where(lane_head == h, xq, jnp.zeros_like(xq)) for h in range(hp)], axis=0)
        y_parts.append(jnp.dot(m_all[:, q * LANES:(q + 1) * LANES], bd, preferred_element_type=F32))
    y = jnp.concatenate(y_parts, axis=1)

    ht = ht_ref[...]
    ht_b = ht.astype(BF16)
    to_end = jnp.exp(last_p - acum_p) * dt_p
    xw = (xs * to_end).astype(BF16)
    y_inter = []
    st_parts = []
    for g in range(SSM_GROUPS):
        gs = slice(g * SSM_GROUP_WIDTH, (g + 1) * SSM_GROUP_WIDTH)
        ns = slice(g * SSM_STATE, (g + 1) * SSM_STATE)
        y_inter.append(jnp.dot(cm[:, ns], ht_b[:, gs], preferred_element_type=F32))
        st_parts.append(_dot_tn(bm[:, ns], xw[:, gs]))
    y = y + jnp.concatenate(y_inter, axis=1) * jnp.exp(acum_p) + dskip_ref[...] * xs
    ht_new = jnp.exp(last_p) * ht + jnp.concatenate(st_parts, axis=1)
    ht_ref[...] = ht_new

    t = y * _silu(z_ref[...])
    outs = []
    for g in range(SSM_GROUPS):
        tg = t[:, g * SSM_GROUP_WIDTH:(g + 1) * SSM_GROUP_WIDTH]
        outs.append(tg * lax.rsqrt(jnp.mean(tg * tg, axis=-1, keepdims=True) + EPS))
    y_ref[...] = (jnp.concatenate(outs, axis=1) * normw_ref[...]).astype(y_ref.dtype)

    @pl.when(c == nc - 1)
    def _():
        hout_ref[0] = ht_new.T


def _ssd_call(u, dt_raw, hist8, h0, layer, lp, *, L, nb, nc, row0, out_layer, prev_out, prev_state):
    rb0 = row0 // L

    def rows(width, colblock):
        return pl.BlockSpec((L, width), lambda b, c: (rb0 + b * nc + c, colblock))

    def const(shape):
        return pl.BlockSpec(shape, lambda b, c: tuple(0 for _ in shape))

    lw = SSM_HEADS * L
    in_specs = [rows(D_SSM, H_X // D_SSM), rows(BC_WIDTH, H_B // BC_WIDTH), rows(BC_WIDTH, H_C // BC_WIDTH),
                rows(D_SSM, H_Z // D_SSM),
                pl.BlockSpec((L, LANES), lambda b, c: (rb0 + b * nc + c, 0)),
                pl.BlockSpec((None, 1, SUBLANES, CONV_DIM), lambda b, c: (layer, b, 0, 0)),
                pl.BlockSpec((None, 1, D_SSM, SSM_STATE), lambda b, c: (layer, b, 0, 0)),
                const((CONV_W, D_SSM)), const((CONV_W, BC_WIDTH)), const((CONV_W, BC_WIDTH)),
                const((1, D_SSM)), const((1, BC_WIDTH)), const((1, BC_WIDTH)),
                const((1, LANES)), const((1, LANES)), const((1, D_SSM)), const((1, D_SSM)),
                const((LANES, D_SSM)), const((LANES, lw))]
    args = [u, u, u, u, dt_raw, hist8, h0,
            lp["conv_wx"], lp["conv_wb"], lp["conv_wc"], lp["conv_bx"], lp["conv_bb"], lp["conv_bc"],
            lp["dt_bias"], lp["a_log"], lp["d_skip_p"], lp["ssm_norm"],
            _expand_onehot(SSM_HEAD_DIM), _expand_onehot(L)]
    aliases = _append_aliased(in_specs, args, (prev_out, prev_state))
    y, hout = pl.pallas_call(
        functools.partial(_ssd_kernel, L=L, nc=nc, n_aliased=len(aliases)),
        grid=(nb, nc),
        in_specs=in_specs,
        out_specs=[pl.BlockSpec((L, D_SSM), lambda b, c: (rb0 + b * nc + c, 0)),
                   pl.BlockSpec((None, 1, D_SSM, SSM_STATE), lambda b, c: (out_layer, b, 0, 0))],
        out_shape=[jax.ShapeDtypeStruct((M_ALL, D_SSM), BF16),
                   jax.ShapeDtypeStruct((DEPTH, nb, D_SSM, SSM_STATE), F32)],
        scratch_shapes=[pltpu.VMEM((SUBLANES + L, D_SSM), F32), pltpu.VMEM((SUBLANES + L, BC_WIDTH), F32),
                        pltpu.VMEM((SUBLANES + L, BC_WIDTH), F32), pltpu.VMEM((SSM_STATE, D_SSM), F32)],
        input_output_aliases=aliases,
        compiler_params=_params(("parallel", "arbitrary")),
        name="ssd_L%d" % L,
    )(*args)
    return y, hout


def _expand_onehot(per_head):
    e = np.zeros((LANES, SSM_HEADS * per_head), np.float32)
    for h in range(SSM_HEADS):
        e[h, h * per_head:(h + 1) * per_head] = 1.0
    return jnp.asarray(e, BF16)


def _ret_consts(L):
    lam = np.log1p(-np.exp2(-5.0 - np.arange(RET_HEADS, dtype=np.float64)))
    i = np.arange(L, dtype=np.float64)
    diff = i[:, None] - i[None, :]
    dmat = np.where(diff >= 0, np.exp(lam[:, None, None] * np.maximum(diff, 0.0)), 0.0)
    to_end = np.exp(lam[:, None] * (L - 1 - i)[None, :])
    q_dec = np.exp(lam[:, None] * (i + 1.0)[None, :])
    chunk_decay = np.exp(lam * L)
    expand = lambda t: np.repeat(t.T[:, :, None], RET_DK, axis=2).reshape(L, RET_HEADS * RET_DK)
    return (jnp.asarray(dmat, F32), jnp.asarray(expand(q_dec), F32), jnp.asarray(expand(to_end), F32),
            [float(v) for v in chunk_decay])


def _ret_kernel(q_ref, k_ref, v_ref, g_ref, cos_ref, sin_ref, qdec_ref, kend_ref, dmat_ref, s0_ref,
                *rest, L, nc, chunk_decay, n_aliased):
    y_ref, sout_ref, s_ref = rest[n_aliased:]
    c = pl.program_id(1)

    @pl.when(c == 0)
    def _():
        s_ref[...] = s0_ref[0]

    cos = cos_ref[...]
    sin = sin_ref[...]
    half = RET_DK // 2
    def first_stage(h):
        ks = slice(h * RET_DK, (h + 1) * RET_DK)
        qh = q_ref[:, ks]
        kh = k_ref[:, ks]
        qr = qh * cos + pltpu.roll(qh, half, 1) * sin
        kr = (kh * cos + pltpu.roll(kh, half, 1) * sin) * (RET_DK ** -0.5)
        vh = v_ref[:, h * RET_DV:(h + 1) * RET_DV].astype(BF16)
        s_prev = s_ref[h]
        qk = _dot_nt(qr.astype(BF16), kr.astype(BF16))
        o_state = jnp.dot((qr * qdec_ref[:, ks]).astype(BF16), s_prev.astype(BF16),
                          preferred_element_type=F32)
        kv = _dot_tn((kr * kend_ref[:, ks]).astype(BF16), vh)
        s_ref[h] = chunk_decay[h] * s_prev + kv
        return qk, o_state, vh

    def second_stage(h, qk, o_state, vh):
        vs = slice(h * RET_DV, (h + 1) * RET_DV)
        s = qk * dmat_ref[h]
        o = jnp.dot(s.astype(BF16), vh, preferred_element_type=F32) + o_state
        o = o * lax.rsqrt(jnp.mean(o * o, axis=-1, keepdims=True) + EPS)
        y_ref[:, vs] = (o * _silu(g_ref[:, vs])).astype(y_ref.dtype)

    pending = first_stage(0)
    for h in range(RET_HEADS):
        nxt = first_stage(h + 1) if h + 1 < RET_HEADS else None
        second_stage(h, *pending)
        pending = nxt

    @pl.when(c == nc - 1)
    def _():
        sout_ref[0] = s_ref[...]


def _ret_call(u, cos_t, sin_t, s0, layer, *, L, nb, nc, row0, out_layer, prev_out, prev_state):
    rb0 = row0 // L
    dmat, qdec, kend, chunk_decay = _ret_consts(L)
    hk = RET_HEADS * RET_DK
    hv = RET_HEADS * RET_DV

    def rows(width, colblock):
        return pl.BlockSpec((L, width), lambda b, c: (rb0 + b * nc + c, colblock))

    in_specs = [rows(hk, T_RQ // hk), rows(hk, T_RK // hk), rows(hv, T_RV // hv), rows(hv, T_RG // hv),
                pl.BlockSpec((L, RET_DK), lambda b, c: (c, 0)),
                pl.BlockSpec((L, RET_DK), lambda b, c: (c, 0)),
                pl.BlockSpec((L, hk), lambda b, c: (0, 0)),
                pl.BlockSpec((L, hk), lambda b, c: (0, 0)),
                pl.BlockSpec((RET_HEADS, L, L), lambda b, c: (0, 0, 0)),
                pl.BlockSpec((None, 1, RET_HEADS, RET_DK, RET_DV), lambda b, c: (layer, b, 0, 0, 0))]
    args = [u, u, u, u, cos_t, sin_t, qdec, kend, dmat, s0]
    aliases = _append_aliased(in_specs, args, (prev_out, prev_state))
    y, sout = pl.pallas_call(
        functools.partial(_ret_kernel, L=L, nc=nc, chunk_decay=chunk_decay, n_aliased=len(aliases)),
        grid=(nb, nc),
        in_specs=in_specs,
        out_specs=[pl.BlockSpec((L, hv), lambda b, c: (rb0 + b * nc + c, 0)),
                   pl.BlockSpec((None, 1, RET_HEADS, RET_DK, RET_DV), lambda b, c: (out_layer, b, 0, 0, 0))],
        out_shape=[jax.ShapeDtypeStruct((M_ALL, hv), BF16),
                   jax.ShapeDtypeStruct((DEPTH, nb, RET_HEADS, RET_DK, RET_DV), F32)],
        scratch_shapes=[pltpu.VMEM((RET_HEADS, RET_DK, RET_DV), F32)],
        input_output_aliases=aliases,
        compiler_params=_params(("parallel", "arbitrary")),
        name="retention_L%d" % L,
    )(*args)
    return y, sout


def _rope_tables(pos):
    half = RET_DK // 2
    inv = ROPE_BASE ** (-jnp.arange(half, dtype=F32) * 2.0 / RET_DK)
    ang = pos.astype(F32)[:, None] * inv[None, :]
    cos = jnp.cos(ang)
    sin = jnp.sin(ang)
    return jnp.concatenate([cos, cos], axis=1), jnp.concatenate([-sin, sin], axis=1)


def _head_norm(xh, g):
    return xh * lax.rsqrt(jnp.mean(xh * xh, axis=-1, keepdims=True) + EPS) * g


def _kv_prep_kernel(k_ref, v_ref, gk_ref, knf_ref, knb_ref, vb_ref, *, pad_first):
    def body():
        for h in range(ATT_HEADS):
            hs = slice(h * ATT_HEAD_DIM, (h + 1) * ATT_HEAD_DIM)
            kn = _head_norm(k_ref[:, hs], gk_ref[...])
            knf_ref[0, :, hs] = kn
            knb_ref[0, :, hs] = kn.astype(BF16)
        vb_ref[0] = v_ref[...].astype(BF16)

    if pad_first:
        i = pl.program_id(1)

        @pl.when(i == 0)
        def _():
            knb_ref[...] = jnp.zeros_like(knb_ref)
            vb_ref[...] = jnp.zeros_like(vb_ref)

        @pl.when(i > 0)
        def _():
            body()
    else:
        body()


def _kv_prep_prompt(u, gk):
    blk = WINDOW
    nblk = SEQ // blk
    dm = ATT_HEADS * ATT_HEAD_DIM

    def rows(colblock):
        return pl.BlockSpec((blk, dm), lambda b, i: (b * nblk + jnp.maximum(i - 1, 0), colblock))

    return pl.pallas_call(
        functools.partial(_kv_prep_kernel, pad_first=True),
        grid=(BATCH, nblk + 1),
        in_specs=[rows(T_AK // dm), rows(T_AV // dm), pl.BlockSpec((1, ATT_HEAD_DIM), lambda b, i: (0, 0))],
        out_specs=[pl.BlockSpec((1, blk, dm), lambda b, i: (b, 0, 0)),
                   pl.BlockSpec((1, blk, dm), lambda b, i: (b, i, 0)),
                   pl.BlockSpec((1, blk, dm), lambda b, i: (b, i, 0))],
        out_shape=[jax.ShapeDtypeStruct((BATCH, blk, dm), F32),
                   jax.ShapeDtypeStruct((BATCH, SEQ + WINDOW, dm), BF16),
                   jax.ShapeDtypeStruct((BATCH, SEQ + WINDOW, dm), BF16)],
        compiler_params=_params(("parallel", "arbitrary")),
        name="kv_prep_prompt",
    )(u, u, gk)


def _kv_prep_sample(u, gk):
    dm = ATT_HEADS * ATT_HEAD_DIM
    rb = M_PROMPT // M_SAMPLE
    return pl.pallas_call(
        functools.partial(_kv_prep_kernel, pad_first=False),
        grid=(1,),
        in_specs=[pl.BlockSpec((M_SAMPLE, dm), lambda i: (rb, T_AK // dm)),
                  pl.BlockSpec((M_SAMPLE, dm), lambda i: (rb, T_AV // dm)),
                  pl.BlockSpec((1, ATT_HEAD_DIM), lambda i: (0, 0))],
        out_specs=[pl.BlockSpec((1, M_SAMPLE, dm), lambda i: (0, 0, 0)),
                   pl.BlockSpec((1, M_SAMPLE, dm), lambda i: (0, 0, 0)),
                   pl.BlockSpec((1, M_SAMPLE, dm), lambda i: (0, 0, 0))],
        out_shape=[jax.ShapeDtypeStruct((1, M_SAMPLE, dm), F32),
                   jax.ShapeDtypeStruct((1, M_SAMPLE, dm), BF16),
                   jax.ShapeDtypeStruct((1, M_SAMPLE, dm), BF16)],
        compiler_params=_params(("arbitrary",)),
        name="kv_prep_sample",
    )(u, u, gk)


def _scores(qn, k1, k2, bias1, bias2, valid1):
    scale = ATT_HEAD_DIM ** -0.5
    s1 = _dot_nt(qn, k1) * scale + bias1
    if valid1 is not None:
        s1 = jnp.where(valid1, s1, NEG_INF)
    s2 = _dot_nt(qn, k2) * scale + bias2
    return s1, s2


def _softmax_pv(s1, s2, v1, v2):
    m = jnp.maximum(jnp.max(s1, axis=-1, keepdims=True), jnp.max(s2, axis=-1, keepdims=True))
    p1 = jnp.exp(s1 - m)
    p2 = jnp.exp(s2 - m)
    inv = 1.0 / (jnp.sum(p1, axis=-1, keepdims=True) + jnp.sum(p2, axis=-1, keepdims=True))
    return (jnp.dot((p1 * inv).astype(BF16), v1, preferred_element_type=F32)
            + jnp.dot((p2 * inv).astype(BF16), v2, preferred_element_type=F32))


def _attend(qn, k1, v1, k2, v2, bias1, bias2, valid1):
    s1, s2 = _scores(qn, k1, k2, bias1, bias2, valid1)
    return _softmax_pv(s1, s2, v1, v2)


ATTN_HEADS_AHEAD = 3


def _attn_prompt_kernel(q_ref, k_ref, v_ref, b1_ref, b2_ref, gq_ref, y_ref):
    c = pl.program_id(1)
    base = pl.multiple_of(c * CHUNK, CHUNK)
    col = lax.broadcasted_iota(jnp.int32, (CHUNK, WINDOW), 1)
    valid1 = col + c * CHUNK >= WINDOW
    heads = [slice(h * ATT_HEAD_DIM, (h + 1) * ATT_HEAD_DIM) for h in range(ATT_HEADS)]

    def scores(h):
        hs = heads[h]
        qn = _head_norm(q_ref[:, hs], gq_ref[...]).astype(BF16)
        return _scores(qn, k_ref[0, pl.ds(base, WINDOW), hs], k_ref[0, pl.ds(base + WINDOW, CHUNK), hs],
                       b1_ref[h], b2_ref[h], valid1)

    pending = [scores(h) for h in range(ATTN_HEADS_AHEAD)]
    for h in range(ATT_HEADS):
        if h + ATTN_HEADS_AHEAD < ATT_HEADS:
            pending.append(scores(h + ATTN_HEADS_AHEAD))
        s1, s2 = pending.pop(0)
        hs = heads[h]
        o = _softmax_pv(s1, s2, v_ref[0, pl.ds(base, WINDOW), hs], v_ref[0, pl.ds(base + WINDOW, CHUNK), hs])
        y_ref[:, hs] = o.astype(y_ref.dtype)


def _attn_prompt_call(u, knb, vb, bias1, bias2, gq):
    dm = ATT_HEADS * ATT_HEAD_DIM
    nc = SEQ // CHUNK
    return pl.pallas_call(
        _attn_prompt_kernel,
        grid=(BATCH, nc),
        in_specs=[pl.BlockSpec((CHUNK, dm), lambda b, c: (b * nc + c, T_AQ // dm)),
                  pl.BlockSpec((1, SEQ + WINDOW, dm), lambda b, c: (b, 0, 0)),
                  pl.BlockSpec((1, SEQ + WINDOW, dm), lambda b, c: (b, 0, 0)),
                  pl.BlockSpec((ATT_HEADS, CHUNK, WINDOW), lambda b, c: (0, 0, 0)),
                  pl.BlockSpec((ATT_HEADS, CHUNK, CHUNK), lambda b, c: (0, 0, 0)),
                  pl.BlockSpec((1, ATT_HEAD_DIM), lambda b, c: (0, 0))],
        out_specs=pl.BlockSpec((CHUNK, dm), lambda b, c: (b * nc + c, 0)),
        out_shape=jax.ShapeDtypeStruct((M_ALL, dm), BF16),
        compiler_params=_params(("parallel", "arbitrary")),
        name="attn_prompt",
    )(u, knb, vb, bias1, bias2, gq)


def _attn_sample_kernel(q_ref, kh_ref, vh_ref, kn_ref, vn_ref, b1_ref, b2_ref, gq_ref, prev_ref, y_ref):
    del prev_ref
    heads = [slice(h * ATT_HEAD_DIM, (h + 1) * ATT_HEAD_DIM) for h in range(ATT_HEADS)]
    hist = kh_ref.shape[1]
    qall = jnp.concatenate([_head_norm(q_ref[:, hs], gq_ref[...]).astype(BF16) for hs in heads], axis=0)
    k1 = kh_ref[0].reshape(hist * ATT_HEADS, ATT_HEAD_DIM).astype(BF16)
    v1 = vh_ref[0].reshape(hist * ATT_HEADS, ATT_HEAD_DIM).astype(BF16)
    k2 = jnp.concatenate([kn_ref[:, hs] for hs in heads], axis=0)
    v2 = jnp.concatenate([vn_ref[:, hs] for hs in heads], axis=0)
    o = _attend(qall, k1, v1, k2, v2, b1_ref[...], b2_ref[...], None)
    for h, hs in enumerate(heads):
        y_ref[:, hs] = o[h * DEC_SEQ:(h + 1) * DEC_SEQ].astype(y_ref.dtype)


def _bias_expand_kernel(t1_ref, t2_ref, o1_ref, o2_ref):
    rows, cb = o1_ref.shape
    row_head = lax.broadcasted_iota(jnp.int32, (rows, cb), 0) // DEC_SEQ
    col = lax.broadcasted_iota(jnp.int32, (rows, cb), 1)
    ej = lax.broadcasted_iota(jnp.int32, (LANES, cb), 0)
    ec = lax.broadcasted_iota(jnp.int32, (LANES, cb), 1)
    e1 = jnp.where(ec // ATT_HEADS == ej, 1.0, 0.0).astype(BF16)
    o1_ref[...] = jnp.where(col % ATT_HEADS == row_head, _exact_right(t1_ref[...], e1), NEG_INF)
    n2 = o2_ref.shape[1]
    row_head2 = lax.broadcasted_iota(jnp.int32, (rows, n2), 0) // DEC_SEQ
    col2 = lax.broadcasted_iota(jnp.int32, (rows, n2), 1)
    ej2 = lax.broadcasted_iota(jnp.int32, (LANES, n2), 0)
    ec2 = lax.broadcasted_iota(jnp.int32, (LANES, n2), 1)
    e2 = jnp.where(ec2 % DEC_SEQ == ej2, 1.0, 0.0).astype(BF16)
    o2_ref[...] = jnp.where(col2 // DEC_SEQ == row_head2, _exact_right(t2_ref[...], e2), NEG_INF)


def _head_masked_bias(tbl1, tbl2):
    nh, nq, hist = tbl1.shape
    rows = nh * nq
    t1 = tbl1.reshape(rows, hist)
    t2 = jnp.pad(tbl2.reshape(rows, nq), ((0, 0), (0, LANES - nq)))
    cb = LANES * nh
    return pl.pallas_call(
        _bias_expand_kernel,
        grid=(hist // LANES,),
        in_specs=[pl.BlockSpec((rows, LANES), lambda j: (0, j)),
                  pl.BlockSpec((rows, LANES), lambda j: (0, 0))],
        out_specs=[pl.BlockSpec((rows, cb), lambda j: (0, j)),
                   pl.BlockSpec((rows, nh * nq), lambda j: (0, 0))],
        out_shape=[jax.ShapeDtypeStruct((rows, hist * nh), F32),
                   jax.ShapeDtypeStruct((rows, nh * nq), F32)],
        compiler_params=_params(("arbitrary",)),
        name="bias_expand",
    )(t1, t2)


def _attn_sample_call(u, k_hist, v_hist, layer, kn_new, vn_new, bias1, bias2, gq, prev_out):
    dm = ATT_HEADS * ATT_HEAD_DIM
    rb0 = M_PROMPT // DEC_SEQ
    hist = k_hist.shape[2]
    cache_spec = pl.BlockSpec((None, 1, hist, ATT_HEADS, ATT_HEAD_DIM), lambda b: (layer, b, 0, 0, 0))
    return pl.pallas_call(
        _attn_sample_kernel,
        grid=(DEC_BATCH,),
        in_specs=[pl.BlockSpec((DEC_SEQ, dm), lambda b: (rb0 + b, T_AQ // dm)),
                  cache_spec,
                  cache_spec,
                  pl.BlockSpec((DEC_SEQ, dm), lambda b: (b, 0)),
                  pl.BlockSpec((DEC_SEQ, dm), lambda b: (b, 0)),
                  pl.BlockSpec((ATT_HEADS * DEC_SEQ, hist * ATT_HEADS), lambda b: (0, 0),
                               pipeline_mode=pl.Buffered(1)),
                  pl.BlockSpec((ATT_HEADS * DEC_SEQ, ATT_HEADS * DEC_SEQ), lambda b: (0, 0)),
                  pl.BlockSpec((1, ATT_HEAD_DIM), lambda b: (0, 0)),
                  pl.BlockSpec(memory_space=pl.ANY)],
        out_specs=pl.BlockSpec((DEC_SEQ, dm), lambda b: (rb0 + b, 0)),
        out_shape=jax.ShapeDtypeStruct((M_ALL, dm), BF16),
        input_output_aliases={8: 0},
        compiler_params=_params(("arbitrary",)),
        name="attn_sample",
    )(u, k_hist, v_hist, kn_new, vn_new, bias1, bias2, gq, prev_out)


def _bias_table(rel_bias):
    kbn = WINDOW + CHUNK
    period = kbn + CHUNK
    m = np.arange(period)
    d = np.where(m < kbn, m, m - period)
    idx = np.clip(WINDOW - d, -REL_CLIP, REL_CLIP) + REL_CLIP
    vec = rel_bias[:, idx]
    rows = jnp.tile(vec, (1, CHUNK))[:, :CHUNK * (period - 1)].reshape(ATT_HEADS, CHUNK, period - 1)
    return rows[:, :, :kbn]


def _layer_params(l, norm_mix, w_in, conv_w, conv_b, dt_bias, a_log, d_skip, ssm_norm, q_norm, k_norm,
                  rel_bias, w_br_ssm, w_br_ret, w_br_att, w_out, norm_ffn, w_ffn_in, w_ffn_out):
    assert w_in.shape[-1] == N_IN
    cw = conv_w[l]
    cb = conv_b[l].reshape(1, CONV_DIM)
    lp = {
        "norm_mix": norm_mix[l], "norm_ffn": norm_ffn[l],
        "conv_wx": cw[:, :D_SSM], "conv_wb": cw[:, D_SSM:D_SSM + BC_WIDTH], "conv_wc": cw[:, D_SSM + BC_WIDTH:],
        "conv_bx": cb[:, :D_SSM], "conv_bb": cb[:, D_SSM:D_SSM + BC_WIDTH], "conv_bc": cb[:, D_SSM + BC_WIDTH:],
        "dt_bias": jnp.pad(dt_bias[l], (0, LANES - SSM_HEADS)).reshape(1, LANES),
        "a_log": jnp.pad(a_log[l], (0, LANES - SSM_HEADS)).reshape(1, LANES),
        "d_skip_p": jnp.repeat(d_skip[l], SSM_HEAD_DIM).reshape(1, D_SSM),
        "ssm_norm": ssm_norm[l].reshape(1, D_SSM),
        "q_norm": q_norm[l].reshape(1, ATT_HEAD_DIM), "k_norm": k_norm[l].reshape(1, ATT_HEAD_DIM),
        "rel_bias": rel_bias[l],
    }
    return lp


def _row_block(t, row0, rows, col0, width):
    return lax.slice(t, (row0, col0), (row0 + rows, col0 + width))


def _layer(x, l, lp, w, state, tables, prev):
    cache_k, cache_v, state_ret, state_ssm4, hist8 = state
    hn = _rmsnorm(x, lp["norm_mix"])
    u_head = _matmul(hn, w["w_in_t"], l, n=HEAD_WIDTH, out_dtype=F32, tm=1088, tn=1024, w_transposed=True,
                     name="in_proj_head")
    dt_raw = _matmul(hn, w["w_in_t"], l, n=LANES, out_dtype=F32, tm=1088, tn=LANES, w_col0=DT_COL0,
                     w_transposed=True, name="in_proj_dt")
    u_tail = _matmul(hn, w["w_in_t"], l, n=TAIL_WIDTH, out_dtype=F32, tm=1088, tn=1024, w_col0=TAIL_COL0,
                     w_transposed=True, name="in_proj_tail")

    ys, ssm_p = _ssd_call(u_head, dt_raw, tables["zeros_hist"], tables["zeros_h"], 0, lp, L=CHUNK, nb=BATCH,
                          nc=SEQ // CHUNK, row0=0, out_layer=l, prev_out=None, prev_state=prev["ssm_p"])
    ys, ssm_s = _ssd_call(u_head, dt_raw, hist8, state_ssm4, l, lp, L=DEC_SEQ, nb=DEC_BATCH, nc=1,
                          row0=M_PROMPT, out_layer=l, prev_out=ys, prev_state=prev["ssm_s"])

    yr, ret_p = _ret_call(u_tail, tables["cos_p"], tables["sin_p"], tables["zeros_s"], 0, L=CHUNK, nb=BATCH,
                          nc=SEQ // CHUNK, row0=0, out_layer=l, prev_out=None, prev_state=prev["ret_p"])
    yr, ret_s = _ret_call(u_tail, tables["cos_s"], tables["sin_s"], state_ret, l, L=DEC_SEQ, nb=DEC_BATCH,
                          nc=1, row0=M_PROMPT, out_layer=l, prev_out=yr, prev_state=prev["ret_s"])

    dm = ATT_HEADS * ATT_HEAD_DIM
    knf_p, knb_p, vb_p = _kv_prep_prompt(u_tail, lp["k_norm"])
    knf_s, knb_s, vb_s = _kv_prep_sample(u_tail, lp["k_norm"])
    hist = cache_k.shape[2]
    assert hist == WINDOW
    tbl = _bias_table(lp["rel_bias"])
    ya = _attn_prompt_call(u_tail, knb_p, vb_p, tbl[:, :, :WINDOW], tbl[:, :, WINDOW:], lp["q_norm"])
    bias1_s, bias2_s = _head_masked_bias(tbl[:, :DEC_SEQ, :hist], tbl[:, :DEC_SEQ, hist:hist + DEC_SEQ])
    ya = _attn_sample_call(u_tail, cache_k, cache_v, l, knb_s[0], vb_s[0], bias1_s, bias2_s, lp["q_norm"], ya)

    m1 = _matmul(ys, w["w_br_ssm"], l, n=D_MODEL, out_dtype=F32, tm=544, tn=1024, gate=u_tail, gate_col0=T_GA,
                 name="merge_ssm")
    m2 = _matmul(yr, w["w_br_ret"], l, n=D_MODEL, out_dtype=F32, tm=544, tn=1024, gate=u_tail, gate_col0=T_GB,
                 addend=m1, name="merge_ret")
    mm = _matmul(ya, w["w_br_att"], l, n=D_MODEL, out_dtype=BF16, tm=544, tn=1024, gate=u_tail,
                 gate_col0=T_GC, addend=m2, name="merge_att")
    x1 = _matmul(mm, w["w_out"], l, n=D_MODEL, out_dtype=F32, tm=544, tn=1024, addend=x, name="out_proj")

    h2 = _rmsnorm(x1, lp["norm_ffn"])
    hh = _swiglu_matmul(h2, w["w_ffn_in"], l, tm=1088, tn=512)
    x2 = _matmul(hh, w["w_ffn_out"], l, n=D_MODEL, out_dtype=F32, tm=544, tn=512, addend=x1, name="ffn_out")

    keep = min(WINDOW, SEQ)
    nconv = CONV_W - 1

    def conv_rows(row0):
        return jnp.concatenate([_row_block(u_head, row0, nconv, H_X, D_SSM),
                                _row_block(u_head, row0, nconv, H_B, BC_WIDTH),
                                _row_block(u_head, row0, nconv, H_C, BC_WIDTH)], axis=-1)

    pv = jnp.stack([_row_block(u_tail, (b + 1) * SEQ - keep, keep, T_AV, dm) for b in range(BATCH)])
    pconv = jnp.stack([conv_rows((b + 1) * SEQ - nconv) for b in range(BATCH)])
    sv = _row_block(u_tail, M_PROMPT, M_SAMPLE, T_AV, dm)
    s_xbc = jnp.concatenate([_row_block(u_head, M_PROMPT, M_SAMPLE, H_X, D_SSM),
                             _row_block(u_head, M_PROMPT, M_SAMPLE, H_B, BC_WIDTH),
                             _row_block(u_head, M_PROMPT, M_SAMPLE, H_C, BC_WIDTH)], axis=-1)
    sconv = s_xbc.reshape(DEC_BATCH, DEC_SEQ, CONV_DIM)[:, DEC_SEQ - nconv:]

    outs_p = (knf_p.reshape(BATCH, keep, ATT_HEADS, ATT_HEAD_DIM),
              pv.reshape(BATCH, keep, ATT_HEADS, ATT_HEAD_DIM),
              pconv)
    outs_s = (knf_s.reshape(DEC_BATCH, DEC_SEQ, ATT_HEADS, ATT_HEAD_DIM),
              sv.reshape(DEC_BATCH, DEC_SEQ, ATT_HEADS, ATT_HEAD_DIM),
              sconv)
    new_states = {"ssm_p": ssm_p, "ssm_s": ssm_s, "ret_p": ret_p, "ret_s": ret_s}
    return x2, outs_p, outs_s, new_states


def kernel(x_prompt, x_sample, cache_attn_k, cache_attn_v, state_ret, state_ssm, state_conv,
           norm_mix, w_in, conv_w, conv_b, dt_bias, a_log, d_skip, ssm_norm, q_norm, k_norm,
           rel_bias, w_br_ssm, w_br_ret, w_br_att, w_out, norm_ffn, w_ffn_in, w_ffn_out):
    x = jnp.concatenate([x_prompt.reshape(M_PROMPT, D_MODEL), x_sample.reshape(M_SAMPLE, D_MODEL)], axis=0)
    cos_p, sin_p = _rope_tables(jnp.arange(SEQ))
    cos_s, sin_s = _rope_tables(PAST_LEN + jnp.arange(DEC_SEQ))
    tables = {"cos_p": cos_p, "sin_p": sin_p, "cos_s": cos_s, "sin_s": sin_s,
              "zeros_hist": jnp.zeros((1, BATCH, SUBLANES, CONV_DIM), F32),
              "zeros_h": jnp.zeros((1, BATCH, D_SSM, SSM_STATE), F32),
              "zeros_s": jnp.zeros((1, BATCH, RET_HEADS, RET_DK, RET_DV), F32)}
    w = {"w_in_t": jnp.swapaxes(w_in, 1, 2), "w_br_ssm": w_br_ssm, "w_br_ret": w_br_ret, "w_br_att": w_br_att, "w_out": w_out,
         "w_ffn_in": w_ffn_in, "w_ffn_out": w_ffn_out}
    hist8 = jnp.pad(state_conv, ((0, 0), (0, 0), (SUBLANES - (CONV_W - 1), 0), (0, 0)))
    state = (cache_attn_k, cache_attn_v, state_ret,
             state_ssm.reshape(DEPTH, DEC_BATCH, D_SSM, SSM_STATE), hist8)
    acc_p = ([], [], [])
    acc_s = ([], [], [])
    prev = {"ssm_p": None, "ssm_s": None, "ret_p": None, "ret_s": None}
    for l in range(DEPTH):
        lp = _layer_params(l, norm_mix, w_in, conv_w, conv_b, dt_bias, a_log, d_skip, ssm_norm, q_norm,
                           k_norm, rel_bias, w_br_ssm, w_br_ret, w_br_att, w_out, norm_ffn, w_ffn_in,
                           w_ffn_out)
        x, outs_p, outs_s, prev = _layer(x, l, lp, w, state, tables, prev)
        for i in range(3):
            acc_p[i].append(outs_p[i])
            acc_s[i].append(outs_s[i])
    pk, pv, pconv = [jnp.stack(a) for a in acc_p]
    sk, sv, sconv = [jnp.stack(a) for a in acc_s]
    pr, sr = prev["ret_p"], prev["ret_s"]
    pssm = prev["ssm_p"].reshape(DEPTH, BATCH, SSM_HEADS, SSM_HEAD_DIM, SSM_STATE)
    sssm = prev["ssm_s"].reshape(DEPTH, DEC_BATCH, SSM_HEADS, SSM_HEAD_DIM, SSM_STATE)
    yp = x[:M_PROMPT].reshape(BATCH, SEQ, D_MODEL)
    ys = x[M_PROMPT:].reshape(DEC_BATCH, DEC_SEQ, D_MODEL)
    return (yp, ys, pk, pv, pr, pssm, pconv, sk, sv, sr, sssm, sconv)
```

```python
import functools
import math

import numpy as np
import jax
import jax.numpy as jnp
from jax import lax
from jax.experimental import pallas as pl
from jax.experimental.pallas import tpu as pltpu

F32 = jnp.float32
BF16 = jnp.bfloat16

D_MODEL = 2048
BATCH = 4
SEQ = 2048
DEPTH = 2
DEC_BATCH = 32
DEC_SEQ = 16
PAST_LEN = 1024
CHUNK = 64
EPS = 1e-6
NEG_INF = -1e30

D_SSM = 2048
SSM_HEAD_DIM = 64
SSM_HEADS = 32
SSM_GROUPS = 4
SSM_STATE = 128
SSM_GROUP_WIDTH = D_SSM // SSM_GROUPS
CONV_W = 4
CONV_DIM = D_SSM + 2 * SSM_GROUPS * SSM_STATE
BC_WIDTH = SSM_GROUPS * SSM_STATE

RET_HEADS = 8
RET_DK = 128
RET_DV = 256
ROPE_BASE = 10000.0

ATT_HEADS = 16
ATT_HEAD_DIM = 128
BAND_CHUNKS = 8
WINDOW = BAND_CHUNKS * CHUNK
REL_CLIP = 256

FFN_HIDDEN = 5632

M_PROMPT = BATCH * SEQ
M_SAMPLE = DEC_BATCH * DEC_SEQ
M_ALL = M_PROMPT + M_SAMPLE

LANES = 128
SUBLANES = 8
V7X_VMEM_LIMIT_BYTES = 56 * 1024 * 1024

H_Z, H_X, H_B, H_C = 0, 2048, 4096, 4608
HEAD_WIDTH = 5120
DT_COL0 = 5120
TAIL_COL0 = DT_COL0 + SSM_HEADS
T_RQ, T_RK, T_RV, T_RG, T_AQ, T_AK, T_AV, T_GA, T_GB, T_GC = (
    0, 1024, 2048, 4096, 6144, 8192, 10240, 12288, 14336, 16384)
TAIL_WIDTH = 18432
N_IN = TAIL_COL0 + TAIL_WIDTH


def _params(sem):
    return pltpu.CompilerParams(dimension_semantics=sem, vmem_limit_bytes=V7X_VMEM_LIMIT_BYTES)


def _append_aliased(in_specs, args, out_buffers):
    aliases = {}
    for k, buf in enumerate(out_buffers):
        if buf is not None:
            in_specs.append(pl.BlockSpec(memory_space=pl.ANY))
            args.append(buf)
            aliases[len(args) - 1] = k
    return aliases


def _sigmoid(x):
    return 1.0 / (1.0 + jnp.exp(-x))


def _silu(x):
    return x * _sigmoid(x)


def _rmsnorm_kernel(x_ref, g_ref, o_ref):
    x = x_ref[...]
    ms = jnp.mean(x * x, axis=-1, keepdims=True)
    o_ref[...] = (x * lax.rsqrt(ms + EPS) * g_ref[...]).astype(o_ref.dtype)


def _rmsnorm_split_kernel(xp_ref, xs_ref, g_ref, o_ref):
    last = pl.num_programs(0) - 1

    @pl.when(pl.program_id(0) < last)
    def _():
        _rmsnorm_kernel(xp_ref, g_ref, o_ref)

    @pl.when(pl.program_id(0) == last)
    def _():
        _rmsnorm_kernel(xs_ref, g_ref, o_ref)


def _rmsnorm(x, g, tm=544):
    d = g.shape[-1]
    gspec = pl.BlockSpec((1, d), lambda i: (0, 0))
    if isinstance(x, (tuple, list)):
        tm = M_SAMPLE
        last_prompt_tile = M_PROMPT // tm - 1
        return pl.pallas_call(
            _rmsnorm_split_kernel,
            grid=(M_ALL // tm,),
            in_specs=[pl.BlockSpec((tm, d), lambda i: (jnp.minimum(i, last_prompt_tile), 0)),
                      pl.BlockSpec((tm, d), lambda i: (0, 0)), gspec],
            out_specs=pl.BlockSpec((tm, d), lambda i: (i, 0)),
            out_shape=jax.ShapeDtypeStruct((M_ALL, d), BF16),
            compiler_params=_params(("arbitrary",)),
            name="rmsnorm_split",
        )(x[0], x[1], g.reshape(1, d))
    m = x.shape[0]
    return pl.pallas_call(
        _rmsnorm_kernel,
        grid=(m // tm,),
        in_specs=[pl.BlockSpec((tm, d), lambda i: (i, 0)), gspec],
        out_specs=pl.BlockSpec((tm, d), lambda i: (i, 0)),
        out_shape=jax.ShapeDtypeStruct((m, d), BF16),
        compiler_params=_params(("parallel",)),
        name="rmsnorm",
    )(x, g.reshape(1, d))


W_CAST_ROWS = 256


def _cast_weight_tile(w_ref, wx_ref, wb_ref, shift):
    rows = wb_ref.shape[0]
    for r in range(0, rows - shift, W_CAST_ROWS):
        n = min(W_CAST_ROWS, rows - shift - r)
        wb_ref[r:r + n, :] = w_ref[r + shift:r + shift + n, :].astype(BF16)
    if shift:
        wb_ref[rows - shift:, :] = wx_ref[:shift, :].astype(BF16)


def _mm_kernel(*refs, shift, has_gate, n_add, n_out, w_transposed):
    a_ref, w_ref = refs[0], refs[1]
    pos = 2
    wx_ref = gate_ref = None
    if shift:
        wx_ref = refs[pos]
        pos += 1
    if has_gate:
        gate_ref = refs[pos]
        pos += 1
    add_refs = refs[pos:pos + n_add]
    pos += n_add
    o_refs = refs[pos:pos + n_out]
    wb_ref = refs[pos + n_out]
    i = pl.program_id(1)

    @pl.when(i == 0)
    def _():
        _cast_weight_tile(w_ref, wx_ref, wb_ref, shift)

    if w_transposed:
        r = _dot_nt(a_ref[...], wb_ref[...])
    else:
        r = jnp.dot(a_ref[...], wb_ref[...], preferred_element_type=F32)
    if has_gate:
        r = _sigmoid(gate_ref[...]) * r

    def finish(add_ref, o_ref):
        res = r if add_ref is None else add_ref[...] + r
        o_ref[...] = res.astype(o_ref.dtype)

    if n_add < 2 and n_out < 2:
        finish(add_refs[0] if n_add else None, o_refs[0])
    else:
        last = pl.num_programs(1) - 1

        @pl.when(i < last)
        def _():
            finish(add_refs[0] if n_add else None, o_refs[0])

        @pl.when(i == last)
        def _():
            finish(add_refs[-1] if n_add else None, o_refs[-1])


def _matmul(a, w, layer, *, n, out_dtype, tm, tn, w_col0=0, w_transposed=False, w_single_buffer=False,
            gate=None, gate_col0=0, addend=None, split_out=False, name="matmul"):
    m, kdim = a.shape
    add_parts = isinstance(addend, (tuple, list))
    if add_parts or split_out:
        assert tm == M_SAMPLE and M_PROMPT % tm == 0 and m == M_ALL
    last_prompt_tile = M_PROMPT // tm - 1
    wj0 = w_col0 // tn
    shift = w_col0 - wj0 * tn
    gj0 = gate_col0 // tn
    extra = LANES
    assert shift == 0 or (w_transposed and shift % SUBLANES == 0 and shift <= extra and tn % extra == 0)
    w_mode = dict(pipeline_mode=pl.Buffered(1)) if w_single_buffer else {}
    if w_transposed:
        w_spec = pl.BlockSpec((None, tn, kdim), lambda j, i: (layer, j + wj0, 0), **w_mode)
        wb_shape = (tn, kdim)
    else:
        w_spec = pl.BlockSpec((None, kdim, tn), lambda j, i: (layer, 0, j + wj0), **w_mode)
        wb_shape = (kdim, tn)
    in_specs = [pl.BlockSpec((tm, kdim), lambda j, i: (i, 0)), w_spec]
    args = [a, w]
    if shift:
        in_specs.append(pl.BlockSpec((None, extra, kdim), lambda j, i: (layer, (j + wj0 + 1) * (tn // extra), 0)))
        args.append(w)
    if gate is not None:
        in_specs.append(pl.BlockSpec((tm, tn), lambda j, i: (i, j + gj0)))
        args.append(gate)
    prompt_rows = pl.BlockSpec((tm, tn), lambda j, i: (jnp.minimum(i, last_prompt_tile), j))
    sample_rows = pl.BlockSpec((tm, tn), lambda j, i: (0, j))
    all_rows = pl.BlockSpec((tm, tn), lambda j, i: (i, j))
    if add_parts:
        in_specs += [prompt_rows, sample_rows]
        args += list(addend)
    elif addend is not None:
        in_specs.append(all_rows)
        args.append(addend)
    n_add = 2 if add_parts else int(addend is not None)
    if split_out:
        out_specs = [prompt_rows, sample_rows]
        out_shape = [jax.ShapeDtypeStruct((M_PROMPT, n), out_dtype), jax.ShapeDtypeStruct((M_SAMPLE, n), out_dtype)]
    else:
        out_specs = all_rows
        out_shape = jax.ShapeDtypeStruct((m, n), out_dtype)
    return pl.pallas_call(
        functools.partial(_mm_kernel, shift=shift, has_gate=gate is not None, n_add=n_add,
                          n_out=2 if split_out else 1, w_transposed=w_transposed),
        grid=(n // tn, m // tm),
        in_specs=in_specs,
        out_specs=out_specs,
        out_shape=out_shape,
        scratch_shapes=[pltpu.VMEM(wb_shape, BF16)],
        compiler_params=_params(("parallel", "arbitrary")),
        name=name,
    )(*args)


def _swiglu_kernel(a_ref, wa_ref, wc_ref, o_ref, wab_ref, wcb_ref):
    @pl.when(pl.program_id(1) == 0)
    def _():
        _cast_weight_tile(wa_ref, None, wab_ref, 0)
        _cast_weight_tile(wc_ref, None, wcb_ref, 0)

    a = a_ref[...]
    fa = jnp.dot(a, wab_ref[...], preferred_element_type=F32)
    fc = jnp.dot(a, wcb_ref[...], preferred_element_type=F32)
    o_ref[...] = (_silu(fa) * fc).astype(o_ref.dtype)


def _swiglu_matmul(a, w, layer, *, tm, tn):
    m, kdim = a.shape
    nj = FFN_HIDDEN // tn
    return pl.pallas_call(
        _swiglu_kernel,
        grid=(nj, m // tm),
        in_specs=[pl.BlockSpec((tm, kdim), lambda j, i: (i, 0)),
                  pl.BlockSpec((None, kdim, tn), lambda j, i: (layer, 0, j)),
                  pl.BlockSpec((None, kdim, tn), lambda j, i: (layer, 0, j + nj))],
        out_specs=pl.BlockSpec((tm, tn), lambda j, i: (i, j)),
        out_shape=jax.ShapeDtypeStruct((m, FFN_HIDDEN), BF16),
        scratch_shapes=[pltpu.VMEM((kdim, tn), BF16), pltpu.VMEM((kdim, tn), BF16)],
        compiler_params=_params(("parallel", "arbitrary")),
        name="ffn_in_swiglu",
    )(a, w, w)


def _split3(a):
    hi = a.astype(BF16)
    r1 = a - hi.astype(F32)
    mid = r1.astype(BF16)
    lo = (r1 - mid.astype(F32)).astype(BF16)
    return hi, mid, lo


def _exact_right(a, onehot):
    return sum(jnp.dot(p, onehot, preferred_element_type=F32) for p in _split3(a))


def _exact_left(onehot, b):
    return sum(jnp.dot(onehot, p, preferred_element_type=F32) for p in _split3(b))


def _dot_nt(a, b):
    return lax.dot_general(a, b, (((1,), (1,)), ((), ())), preferred_element_type=F32)


def _dot_tn(a, b):
    return lax.dot_general(a, b, (((0,), (0,)), ((), ())), preferred_element_type=F32)


def _conv_silu(ext_ref, cw, cb, L):
    acc = ext_ref[SUBLANES:SUBLANES + L, :] * cw[CONV_W - 1:CONV_W, :] + cb
    for s in range(1, CONV_W):
        acc = acc + ext_ref[SUBLANES - s:SUBLANES - s + L, :] * cw[CONV_W - 1 - s:CONV_W - s, :]
    return _silu(acc)


def _ssd_kernel(x_ref, b_ref, c_ref, z_ref, dt_ref, hist_ref, h0_ref,
                cwx_ref, cwb_ref, cwc_ref, cbx_ref, cbb_ref, cbc_ref,
                dtb_ref, alog_ref, dskip_ref, normw_ref, exp_p_ref, exp_l_ref,
                *rest, L, nc, n_aliased):
    y_ref, hout_ref, ex_ref, eb_ref, ec_ref, ht_ref = rest[n_aliased:]
    c = pl.program_id(1)
    lw = SSM_HEADS * L

    @pl.when(c == 0)
    def _():
        hist = hist_ref[0]
        ex_ref[0:SUBLANES, :] = hist[:, :D_SSM]
        eb_ref[0:SUBLANES, :] = hist[:, D_SSM:D_SSM + BC_WIDTH]
        ec_ref[0:SUBLANES, :] = hist[:, D_SSM + BC_WIDTH:]
        ht_ref[...] = h0_ref[0].T

    @pl.when(c > 0)
    def _():
        ex_ref[0:SUBLANES, :] = ex_ref[L:L + SUBLANES, :]
        eb_ref[0:SUBLANES, :] = eb_ref[L:L + SUBLANES, :]
        ec_ref[0:SUBLANES, :] = ec_ref[L:L + SUBLANES, :]

    ex_ref[SUBLANES:SUBLANES + L, :] = x_ref[...]
    eb_ref[SUBLANES:SUBLANES + L, :] = b_ref[...]
    ec_ref[SUBLANES:SUBLANES + L, :] = c_ref[...]
    xs = _conv_silu(ex_ref, cwx_ref[...], cbx_ref[...], L)
    bm = _conv_silu(eb_ref, cwb_ref[...], cbb_ref[...], L).astype(BF16)
    cm = _conv_silu(ec_ref, cwc_ref[...], cbc_ref[...], L).astype(BF16)

    dtr = dt_ref[...] + dtb_ref[...]
    dt = jnp.maximum(dtr, 0.0) + jnp.log1p(jnp.exp(-jnp.abs(dtr)))

    tri = (lax.broadcasted_iota(jnp.int32, (L, L), 1)
           <= lax.broadcasted_iota(jnp.int32, (L, L), 0)).astype(BF16)
    acum = _exact_left(tri, dt * (-jnp.exp(alog_ref[...])))

    dt_p = _exact_right(dt, exp_p_ref[...])
    acum_p = _exact_right(acum, exp_p_ref[...])
    last_p = acum_p[L - 1:L, :]
    if L == SSM_HEAD_DIM:
        dt_l, acum_l = dt_p, acum_p
    else:
        dt_l = _exact_right(dt, exp_l_ref[...])
        acum_l = _exact_right(acum, exp_l_ref[...])

    rowi = lax.broadcasted_iota(jnp.int32, (L, lw), 0)
    colj = lax.broadcasted_iota(jnp.int32, (L, lw), 1) % L
    diag = colj == rowi
    causal = colj <= rowi
    acum_row = jnp.sum(jnp.where(diag, acum_l, 0.0), axis=0, keepdims=True)
    dt_row = jnp.sum(jnp.where(diag, dt_l, 0.0), axis=0, keepdims=True)
    decay = jnp.where(causal, jnp.exp(jnp.where(causal, acum_l - acum_row, 0.0)), 0.0)

    heads_per_group = SSM_HEADS // SSM_GROUPS
    cb = jnp.concatenate(
        [_dot_nt(cm[:, g * SSM_STATE:(g + 1) * SSM_STATE],
                 jnp.concatenate([bm[:, g * SSM_STATE:(g + 1) * SSM_STATE]] * heads_per_group, axis=0))
         for g in range(SSM_GROUPS)], axis=1)
    m_all = (cb * decay * dt_row).astype(BF16)

    hp = LANES // L
    slab = hp * SSM_HEAD_DIM
    xs_b = xs.astype(BF16)
    lane_head = lax.broadcasted_iota(jnp.int32, (L, slab), 1) // SSM_HEAD_DIM
    y_parts = []
    for q in range(lw // LANES):
        xq = xs_b[:, q * slab:(q + 1) * slab]
        bd = jnp.concatenate([jnp.where(lane_head == h, xq, jnp.zeros_like(xq)) for h in range(hp)], axis=0)
        y_parts.append(jnp.dot(m_all[:, q * LANES:(q + 1) * LANES], bd, preferred_element_type=F32))
    y = jnp.concatenate(y_parts, axis=1)

    ht = ht_ref[...]
    ht_b = ht.astype(BF16)
    to_end = jnp.exp(last_p - acum_p) * dt_p
    xw = (xs * to_end).astype(BF16)
    y_inter = []
    st_parts = []
    for g in range(SSM_GROUPS):
        gs = slice(g * SSM_GROUP_WIDTH, (g + 1) * SSM_GROUP_WIDTH)
        ns = slice(g * SSM_STATE, (g + 1) * SSM_STATE)
        y_inter.append(jnp.dot(cm[:, ns], ht_b[:, gs], preferred_element_type=F32))
        st_parts.append(_dot_tn(bm[:, ns], xw[:, gs]))
    y = y + jnp.concatenate(y_inter, axis=1) * jnp.exp(acum_p) + dskip_ref[...] * xs
    ht_new = jnp.exp(last_p) * ht + jnp.concatenate(st_parts, axis=1)
    ht_ref[...] = ht_new

    t = y * _silu(z_ref[...])
    outs = []
    for g in range(SSM_GROUPS):
        tg = t[:, g * SSM_GROUP_WIDTH:(g + 1) * SSM_GROUP_WIDTH]
        outs.append(tg * lax.rsqrt(jnp.mean(tg * tg, axis=-1, keepdims=True) + EPS))
    y_ref[...] = (jnp.concatenate(outs, axis=1) * normw_ref[...]).astype(y_ref.dtype)

    @pl.when(c == nc - 1)
    def _():
        hout_ref[0] = ht_new.T


def _ssd_call(u, dt_raw, hist8, h0, layer, lp, *, L, nb, nc, row0, out_layer, prev_out, prev_state):
    rb0 = row0 // L

    def rows(width, colblock):
        return pl.BlockSpec((L, width), lambda b, c: (rb0 + b * nc + c, colblock))

    def const(shape):
        return pl.BlockSpec(shape, lambda b, c: tuple(0 for _ in shape))

    lw = SSM_HEADS * L
    in_specs = [rows(D_SSM, H_X // D_SSM), rows(BC_WIDTH, H_B // BC_WIDTH), rows(BC_WIDTH, H_C // BC_WIDTH),
                rows(D_SSM, H_Z // D_SSM),
                pl.BlockSpec((L, LANES), lambda b, c: (rb0 + b * nc + c, 0)),
                pl.BlockSpec((None, 1, SUBLANES, CONV_DIM), lambda b, c: (layer, b, 0, 0)),
                pl.BlockSpec((None, 1, D_SSM, SSM_STATE), lambda b, c: (layer, b, 0, 0)),
                const((CONV_W, D_SSM)), const((CONV_W, BC_WIDTH)), const((CONV_W, BC_WIDTH)),
                const((1, D_SSM)), const((1, BC_WIDTH)), const((1, BC_WIDTH)),
                const((1, LANES)), const((1, LANES)), const((1, D_SSM)), const((1, D_SSM)),
                const((LANES, D_SSM)), const((LANES, lw))]
    args = [u, u, u, u, dt_raw, hist8, h0,
            lp["conv_wx"], lp["conv_wb"], lp["conv_wc"], lp["conv_bx"], lp["conv_bb"], lp["conv_bc"],
            lp["dt_bias"], lp["a_log"], lp["d_skip_p"], lp["ssm_norm"],
            _expand_onehot(SSM_HEAD_DIM), _expand_onehot(L)]
    aliases = _append_aliased(in_specs, args, (prev_out, prev_state))
    y, hout = pl.pallas_call(
        functools.partial(_ssd_kernel, L=L, nc=nc, n_aliased=len(aliases)),
        grid=(nb, nc),
        in_specs=in_specs,
        out_specs=[pl.BlockSpec((L, D_SSM), lambda b, c: (rb0 + b * nc + c, 0)),
                   pl.BlockSpec((None, 1, D_SSM, SSM_STATE), lambda b, c: (out_layer, b, 0, 0))],
        out_shape=[jax.ShapeDtypeStruct((M_ALL, D_SSM), BF16),
                   jax.ShapeDtypeStruct((DEPTH, nb, D_SSM, SSM_STATE), F32)],
        scratch_shapes=[pltpu.VMEM((SUBLANES + L, D_SSM), F32), pltpu.VMEM((SUBLANES + L, BC_WIDTH), F32),
                        pltpu.VMEM((SUBLANES + L, BC_WIDTH), F32), pltpu.VMEM((SSM_STATE, D_SSM), F32)],
        input_output_aliases=aliases,
        compiler_params=_params(("parallel", "arbitrary")),
        name="ssd_L%d" % L,
    )(*args)
    return y, hout


def _expand_onehot(per_head):
    e = np.zeros((LANES, SSM_HEADS * per_head), np.float32)
    for h in range(SSM_HEADS):
        e[h, h * per_head:(h + 1) * per_head] = 1.0
    return jnp.asarray(e, BF16)


def _ret_consts(L):
    lam = np.log1p(-np.exp2(-5.0 - np.arange(RET_HEADS, dtype=np.float64)))
    i = np.arange(L, dtype=np.float64)
    diff = i[:, None] - i[None, :]
    dmat = np.where(diff >= 0, np.exp(lam[:, None, None] * np.maximum(diff, 0.0)), 0.0)
    to_end = np.exp(lam[:, None] * (L - 1 - i)[None, :])
    q_dec = np.exp(lam[:, None] * (i + 1.0)[None, :])
    chunk_decay = np.exp(lam * L)
    expand = lambda t: np.repeat(t.T[:, :, None], RET_DK, axis=2).reshape(L, RET_HEADS * RET_DK)
    return (jnp.asarray(dmat, F32), jnp.asarray(expand(q_dec), F32), jnp.asarray(expand(to_end), F32),
            [float(v) for v in chunk_decay])


def _ret_kernel(q_ref, k_ref, v_ref, g_ref, cos_ref, sin_ref, qdec_ref, kend_ref, dmat_ref, s0_ref,
                *rest, L, nc, chunk_decay, n_aliased):
    y_ref, sout_ref, s_ref = rest[n_aliased:]
    c = pl.program_id(1)

    @pl.when(c == 0)
    def _():
        s_ref[...] = s0_ref[0]

    cos = cos_ref[...]
    sin = sin_ref[...]
    half = RET_DK // 2
    def first_stage(h):
        ks = slice(h * RET_DK, (h + 1) * RET_DK)
        qh = q_ref[:, ks]
        kh = k_ref[:, ks]
        qr = qh * cos + pltpu.roll(qh, half, 1) * sin
        kr = (kh * cos + pltpu.roll(kh, half, 1) * sin) * (RET_DK ** -0.5)
        vh = v_ref[:, h * RET_DV:(h + 1) * RET_DV].astype(BF16)
        s_prev = s_ref[h]
        qk = _dot_nt(qr.astype(BF16), kr.astype(BF16))
        o_state = jnp.dot((qr * qdec_ref[:, ks]).astype(BF16), s_prev.astype(BF16),
                          preferred_element_type=F32)
        kv = _dot_tn((kr * kend_ref[:, ks]).astype(BF16), vh)
        s_ref[h] = chunk_decay[h] * s_prev + kv
        return qk, o_state, vh

    def second_stage(h, qk, o_state, vh):
        vs = slice(h * RET_DV, (h + 1) * RET_DV)
        s = qk * dmat_ref[h]
        o = jnp.dot(s.astype(BF16), vh, preferred_element_type=F32) + o_state
        o = o * lax.rsqrt(jnp.mean(o * o, axis=-1, keepdims=True) + EPS)
        y_ref[:, vs] = (o * _silu(g_ref[:, vs])).astype(y_ref.dtype)

    pending = first_stage(0)
    for h in range(RET_HEADS):
        nxt = first_stage(h + 1) if h + 1 < RET_HEADS else None
        second_stage(h, *pending)
        pending = nxt

    @pl.when(c == nc - 1)
    def _():
        sout_ref[0] = s_ref[...]


def _ret_call(u, cos_t, sin_t, s0, layer, *, L, nb, nc, row0, out_layer, prev_out, prev_state):
    rb0 = row0 // L
    dmat, qdec, kend, chunk_decay = _ret_consts(L)
    hk = RET_HEADS * RET_DK
    hv = RET_HEADS * RET_DV

    def rows(width, colblock):
        return pl.BlockSpec((L, width), lambda b, c: (rb0 + b * nc + c, colblock))

    in_specs = [rows(hk, T_RQ // hk), rows(hk, T_RK // hk), rows(hv, T_RV // hv), rows(hv, T_RG // hv),
                pl.BlockSpec((L, RET_DK), lambda b, c: (c, 0)),
                pl.BlockSpec((L, RET_DK), lambda b, c: (c, 0)),
                pl.BlockSpec((L, hk), lambda b, c: (0, 0)),
                pl.BlockSpec((L, hk), lambda b, c: (0, 0)),
                pl.BlockSpec((RET_HEADS, L, L), lambda b, c: (0, 0, 0)),
                pl.BlockSpec((None, 1, RET_HEADS, RET_DK, RET_DV), lambda b, c: (layer, b, 0, 0, 0))]
    args = [u, u, u, u, cos_t, sin_t, qdec, kend, dmat, s0]
    aliases = _append_aliased(in_specs, args, (prev_out, prev_state))
    y, sout = pl.pallas_call(
        functools.partial(_ret_kernel, L=L, nc=nc, chunk_decay=chunk_decay, n_aliased=len(aliases)),
        grid=(nb, nc),
        in_specs=in_specs,
        out_specs=[pl.BlockSpec((L, hv), lambda b, c: (rb0 + b * nc + c, 0)),
                   pl.BlockSpec((None, 1, RET_HEADS, RET_DK, RET_DV), lambda b, c: (out_layer, b, 0, 0, 0))],
        out_shape=[jax.ShapeDtypeStruct((M_ALL, hv), BF16),
                   jax.ShapeDtypeStruct((DEPTH, nb, RET_HEADS, RET_DK, RET_DV), F32)],
        scratch_shapes=[pltpu.VMEM((RET_HEADS, RET_DK, RET_DV), F32)],
        input_output_aliases=aliases,
        compiler_params=_params(("parallel", "arbitrary")),
        name="retention_L%d" % L,
    )(*args)
    return y, sout


def _rope_tables(pos):
    half = RET_DK // 2
    inv = ROPE_BASE ** (-jnp.arange(half, dtype=F32) * 2.0 / RET_DK)
    ang = pos.astype(F32)[:, None] * inv[None, :]
    cos = jnp.cos(ang)
    sin = jnp.sin(ang)
    return jnp.concatenate([cos, cos], axis=1), jnp.concatenate([-sin, sin], axis=1)


def _head_norm(xh, g):
    return xh * lax.rsqrt(jnp.mean(xh * xh, axis=-1, keepdims=True) + EPS) * g


def _kv_prep_kernel(k_ref, v_ref, gk_ref, knf_ref, knb_ref, vb_ref):
    for h in range(ATT_HEADS):
        hs = slice(h * ATT_HEAD_DIM, (h + 1) * ATT_HEAD_DIM)
        kn = _head_norm(k_ref[:, hs], gk_ref[...])
        knf_ref[0, :, hs] = kn
        knb_ref[0, :, hs] = kn.astype(BF16)
    vb_ref[0] = v_ref[...].astype(BF16)


def _kv_prep_sample(u, gk):
    dm = ATT_HEADS * ATT_HEAD_DIM
    rb = M_PROMPT // M_SAMPLE
    return pl.pallas_call(
        _kv_prep_kernel,
        grid=(1,),
        in_specs=[pl.BlockSpec((M_SAMPLE, dm), lambda i: (rb, T_AK // dm)),
                  pl.BlockSpec((M_SAMPLE, dm), lambda i: (rb, T_AV // dm)),
                  pl.BlockSpec((1, ATT_HEAD_DIM), lambda i: (0, 0))],
        out_specs=[pl.BlockSpec((1, M_SAMPLE, dm), lambda i: (0, 0, 0)),
                   pl.BlockSpec((1, M_SAMPLE, dm), lambda i: (0, 0, 0)),
                   pl.BlockSpec((1, M_SAMPLE, dm), lambda i: (0, 0, 0))],
        out_shape=[jax.ShapeDtypeStruct((1, M_SAMPLE, dm), F32),
                   jax.ShapeDtypeStruct((1, M_SAMPLE, dm), BF16),
                   jax.ShapeDtypeStruct((1, M_SAMPLE, dm), BF16)],
        compiler_params=_params(("arbitrary",)),
        name="kv_prep_sample",
    )(u, u, gk)


def _scores(qn, k1, k2, bias1, bias2, valid1):
    scale = ATT_HEAD_DIM ** -0.5
    s1 = _dot_nt(qn, k1) * scale + bias1
    if valid1 is not None:
        s1 = jnp.where(valid1, s1, NEG_INF)
    s2 = _dot_nt(qn, k2) * scale + bias2
    return s1, s2


def _softmax_pv(s1, s2, v1, v2):
    m = jnp.maximum(jnp.max(s1, axis=-1, keepdims=True), jnp.max(s2, axis=-1, keepdims=True))
    p1 = jnp.exp(s1 - m)
    p2 = jnp.exp(s2 - m)
    inv = 1.0 / (jnp.sum(p1, axis=-1, keepdims=True) + jnp.sum(p2, axis=-1, keepdims=True))
    return (jnp.dot((p1 * inv).astype(BF16), v1, preferred_element_type=F32)
            + jnp.dot((p2 * inv).astype(BF16), v2, preferred_element_type=F32))


def _attend(qn, k1, v1, k2, v2, bias1, bias2, valid1):
    s1, s2 = _scores(qn, k1, k2, bias1, bias2, valid1)
    return _softmax_pv(s1, s2, v1, v2)


ATTN_HEADS_AHEAD = 3


def _attn_prompt_kernel(q_ref, k_ref, v_ref, b1_ref, b2_ref, gq_ref, gk_ref, y_ref, knf_ref, kb_ref, vb_ref):
    c = pl.program_id(1)
    base = pl.multiple_of(c * CHUNK, CHUNK)
    own = pl.multiple_of(c * CHUNK + WINDOW, CHUNK)
    heads = [slice(h * ATT_HEAD_DIM, (h + 1) * ATT_HEAD_DIM) for h in range(ATT_HEADS)]

    @pl.when(c == 0)
    def _():
        kb_ref[0:WINDOW, :] = jnp.zeros((WINDOW, kb_ref.shape[1]), BF16)
        vb_ref[0:WINDOW, :] = jnp.zeros((WINDOW, vb_ref.shape[1]), BF16)

    for hs in heads:
        kn = _head_norm(k_ref[:, hs], gk_ref[...])
        knf_ref[0, :, hs] = kn
        kb_ref[pl.ds(own, CHUNK), hs] = kn.astype(BF16)
    vb_ref[pl.ds(own, CHUNK), :] = v_ref[...].astype(BF16)

    col = lax.broadcasted_iota(jnp.int32, (CHUNK, WINDOW), 1)
    valid1 = col + c * CHUNK >= WINDOW

    def scores(h):
        hs = heads[h]
        qn = _head_norm(q_ref[:, hs], gq_ref[...]).astype(BF16)
        return _scores(qn, kb_ref[pl.ds(base, WINDOW), hs], kb_ref[pl.ds(own, CHUNK), hs],
                       b1_ref[h], b2_ref[h], valid1)

    pending = [scores(h) for h in range(ATTN_HEADS_AHEAD)]
    for h in range(ATT_HEADS):
        if h + ATTN_HEADS_AHEAD < ATT_HEADS:
            pending.append(scores(h + ATTN_HEADS_AHEAD))
        s1, s2 = pending.pop(0)
        hs = heads[h]
        o = _softmax_pv(s1, s2, vb_ref[pl.ds(base, WINDOW), hs], vb_ref[pl.ds(own, CHUNK), hs])
        y_ref[:, hs] = o.astype(y_ref.dtype)


def _attn_prompt_call(u, bias1, bias2, gq, gk):
    dm = ATT_HEADS * ATT_HEAD_DIM
    nc = SEQ // CHUNK
    keep_chunks = min(WINDOW, SEQ) // CHUNK
    first_kept = nc - keep_chunks

    def rows(col0):
        return pl.BlockSpec((CHUNK, dm), lambda b, c: (b * nc + c, col0 // dm))

    return pl.pallas_call(
        _attn_prompt_kernel,
        grid=(BATCH, nc),
        in_specs=[rows(T_AQ), rows(T_AK), rows(T_AV),
                  pl.BlockSpec((ATT_HEADS, CHUNK, WINDOW), lambda b, c: (0, 0, 0)),
                  pl.BlockSpec((ATT_HEADS, CHUNK, CHUNK), lambda b, c: (0, 0, 0)),
                  pl.BlockSpec((1, ATT_HEAD_DIM), lambda b, c: (0, 0)),
                  pl.BlockSpec((1, ATT_HEAD_DIM), lambda b, c: (0, 0))],
        out_specs=[pl.BlockSpec((CHUNK, dm), lambda b, c: (b * nc + c, 0)),
                   pl.BlockSpec((1, CHUNK, dm), lambda b, c: (b, jnp.maximum(c - first_kept, 0), 0))],
        out_shape=[jax.ShapeDtypeStruct((M_ALL, dm), BF16),
                   jax.ShapeDtypeStruct((BATCH, keep_chunks * CHUNK, dm), F32)],
        scratch_shapes=[pltpu.VMEM((SEQ + WINDOW, dm), BF16), pltpu.VMEM((SEQ + WINDOW, dm), BF16)],
        compiler_params=_params(("parallel", "arbitrary")),
        name="attn_prompt",
    )(u, u, u, bias1, bias2, gq, gk)


def _attn_sample_kernel(q_ref, kh_ref, vh_ref, kn_ref, vn_ref, b1_ref, b2_ref, gq_ref, prev_ref, y_ref):
    del prev_ref
    heads = [slice(h * ATT_HEAD_DIM, (h + 1) * ATT_HEAD_DIM) for h in range(ATT_HEADS)]
    hist = kh_ref.shape[1]
    qall = jnp.concatenate([_head_norm(q_ref[:, hs], gq_ref[...]).astype(BF16) for hs in heads], axis=0)
    k1 = kh_ref[0].reshape(hist * ATT_HEADS, ATT_HEAD_DIM).astype(BF16)
    v1 = vh_ref[0].reshape(hist * ATT_HEADS, ATT_HEAD_DIM).astype(BF16)
    k2 = jnp.concatenate([kn_ref[:, hs] for hs in heads], axis=0)
    v2 = jnp.concatenate([vn_ref[:, hs] for hs in heads], axis=0)
    o = _attend(qall, k1, v1, k2, v2, b1_ref[...], b2_ref[...], None)
    for h, hs in enumerate(heads):
        y_ref[:, hs] = o[h * DEC_SEQ:(h + 1) * DEC_SEQ].astype(y_ref.dtype)


def _bias_expand_kernel(t1_ref, t2_ref, o1_ref, o2_ref):
    rows, cb = o1_ref.shape
    row_head = lax.broadcasted_iota(jnp.int32, (rows, cb), 0) // DEC_SEQ
    col = lax.broadcasted_iota(jnp.int32, (rows, cb), 1)
    ej = lax.broadcasted_iota(jnp.int32, (LANES, cb), 0)
    ec = lax.broadcasted_iota(jnp.int32, (LANES, cb), 1)
    e1 = jnp.where(ec // ATT_HEADS == ej, 1.0, 0.0).astype(BF16)
    o1_ref[...] = jnp.where(col % ATT_HEADS == row_head, _exact_right(t1_ref[...], e1), NEG_INF)
    n2 = o2_ref.shape[1]
    row_head2 = lax.broadcasted_iota(jnp.int32, (rows, n2), 0) // DEC_SEQ
    col2 = lax.broadcasted_iota(jnp.int32, (rows, n2), 1)
    ej2 = lax.broadcasted_iota(jnp.int32, (LANES, n2), 0)
    ec2 = lax.broadcasted_iota(jnp.int32, (LANES, n2), 1)
    e2 = jnp.where(ec2 % DEC_SEQ == ej2, 1.0, 0.0).astype(BF16)
    o2_ref[...] = jnp.where(col2 // DEC_SEQ == row_head2, _exact_right(t2_ref[...], e2), NEG_INF)


def _head_masked_bias(tbl1, tbl2):
    nh, nq, hist = tbl1.shape
    rows = nh * nq
    t1 = tbl1.reshape(rows, hist)
    t2 = jnp.pad(tbl2.reshape(rows, nq), ((0, 0), (0, LANES - nq)))
    cb = LANES * nh
    return pl.pallas_call(
        _bias_expand_kernel,
        grid=(hist // LANES,),
        in_specs=[pl.BlockSpec((rows, LANES), lambda j: (0, j)),
                  pl.BlockSpec((rows, LANES), lambda j: (0, 0))],
        out_specs=[pl.BlockSpec((rows, cb), lambda j: (0, j)),
                   pl.BlockSpec((rows, nh * nq), lambda j: (0, 0))],
        out_shape=[jax.ShapeDtypeStruct((rows, hist * nh), F32),
                   jax.ShapeDtypeStruct((rows, nh * nq), F32)],
        compiler_params=_params(("arbitrary",)),
        name="bias_expand",
    )(t1, t2)


def _attn_sample_call(u, k_hist, v_hist, layer, kn_new, vn_new, bias1, bias2, gq, prev_out):
    dm = ATT_HEADS * ATT_HEAD_DIM
    rb0 = M_PROMPT // DEC_SEQ
    hist = k_hist.shape[2]
    cache_spec = pl.BlockSpec((None, 1, hist, ATT_HEADS, ATT_HEAD_DIM), lambda b: (layer, b, 0, 0, 0))
    return pl.pallas_call(
        _attn_sample_kernel,
        grid=(DEC_BATCH,),
        in_specs=[pl.BlockSpec((DEC_SEQ, dm), lambda b: (rb0 + b, T_AQ // dm)),
                  cache_spec,
                  cache_spec,
                  pl.BlockSpec((DEC_SEQ, dm), lambda b: (b, 0)),
                  pl.BlockSpec((DEC_SEQ, dm), lambda b: (b, 0)),
                  pl.BlockSpec((ATT_HEADS * DEC_SEQ, hist * ATT_HEADS), lambda b: (0, 0),
                               pipeline_mode=pl.Buffered(1)),
                  pl.BlockSpec((ATT_HEADS * DEC_SEQ, ATT_HEADS * DEC_SEQ), lambda b: (0, 0)),
                  pl.BlockSpec((1, ATT_HEAD_DIM), lambda b: (0, 0)),
                  pl.BlockSpec(memory_space=pl.ANY)],
        out_specs=pl.BlockSpec((DEC_SEQ, dm), lambda b: (rb0 + b, 0)),
        out_shape=jax.ShapeDtypeStruct((M_ALL, dm), BF16),
        input_output_aliases={8: 0},
        compiler_params=_params(("arbitrary",)),
        name="attn_sample",
    )(u, k_hist, v_hist, kn_new, vn_new, bias1, bias2, gq, prev_out)


def _bias_table(rel_bias):
    kbn = WINDOW + CHUNK
    period = kbn + CHUNK
    m = np.arange(period)
    d = np.where(m < kbn, m, m - period)
    idx = np.clip(WINDOW - d, -REL_CLIP, REL_CLIP) + REL_CLIP
    vec = rel_bias[:, idx]
    rows = jnp.tile(vec, (1, CHUNK))[:, :CHUNK * (period - 1)].reshape(ATT_HEADS, CHUNK, period - 1)
    return rows[:, :, :kbn]


def _layer_params(l, norm_mix, w_in, conv_w, conv_b, dt_bias, a_log, d_skip, ssm_norm, q_norm, k_norm,
                  rel_bias, w_br_ssm, w_br_ret, w_br_att, w_out, norm_ffn, w_ffn_in, w_ffn_out):
    assert w_in.shape[-1] == N_IN
    cw = conv_w[l]
    cb = conv_b[l].reshape(1, CONV_DIM)
    lp = {
        "norm_mix": norm_mix[l], "norm_ffn": norm_ffn[l],
        "conv_wx": cw[:, :D_SSM], "conv_wb": cw[:, D_SSM:D_SSM + BC_WIDTH], "conv_wc": cw[:, D_SSM + BC_WIDTH:],
        "conv_bx": cb[:, :D_SSM], "conv_bb": cb[:, D_SSM:D_SSM + BC_WIDTH], "conv_bc": cb[:, D_SSM + BC_WIDTH:],
        "dt_bias": jnp.pad(dt_bias[l], (0, LANES - SSM_HEADS)).reshape(1, LANES),
        "a_log": jnp.pad(a_log[l], (0, LANES - SSM_HEADS)).reshape(1, LANES),
        "d_skip_p": jnp.repeat(d_skip[l], SSM_HEAD_DIM).reshape(1, D_SSM),
        "ssm_norm": ssm_norm[l].reshape(1, D_SSM),
        "q_norm": q_norm[l].reshape(1, ATT_HEAD_DIM), "k_norm": k_norm[l].reshape(1, ATT_HEAD_DIM),
        "rel_bias": rel_bias[l],
    }
    return lp


def _row_block(t, row0, rows, col0, width):
    return lax.slice(t, (row0, col0), (row0 + rows, col0 + width))


def _layer(x, l, lp, w, state, tables, prev):
    cache_k, cache_v, state_ret, state_ssm4, hist8 = state
    hn = _rmsnorm(x, lp["norm_mix"])
    u_head = _matmul(hn, w["w_in_t"], l, n=HEAD_WIDTH, out_dtype=F32, tm=1088, tn=1024, w_transposed=True,
                     name="in_proj_head")
    dt_raw = _matmul(hn, w["w_in_t"], l, n=LANES, out_dtype=F32, tm=1088, tn=LANES, w_col0=DT_COL0,
                     w_transposed=True, name="in_proj_dt")
    u_tail = _matmul(hn, w["w_in_t"], l, n=TAIL_WIDTH, out_dtype=F32, tm=1088, tn=1024, w_col0=TAIL_COL0,
                     w_transposed=True, name="in_proj_tail")

    ys, ssm_p = _ssd_call(u_head, dt_raw, tables["zeros_hist"], tables["zeros_h"], 0, lp, L=CHUNK, nb=BATCH,
                          nc=SEQ // CHUNK, row0=0, out_layer=l, prev_out=None, prev_state=prev["ssm_p"])
    ys, ssm_s = _ssd_call(u_head, dt_raw, hist8, state_ssm4, l, lp, L=DEC_SEQ, nb=DEC_BATCH, nc=1,
                          row0=M_PROMPT, out_layer=l, prev_out=ys, prev_state=prev["ssm_s"])

    yr, ret_p = _ret_call(u_tail, tables["cos_p"], tables["sin_p"], tables["zeros_s"], 0, L=CHUNK, nb=BATCH,
                          nc=SEQ // CHUNK, row0=0, out_layer=l, prev_out=None, prev_state=prev["ret_p"])
    yr, ret_s = _ret_call(u_tail, tables["cos_s"], tables["sin_s"], state_ret, l, L=DEC_SEQ, nb=DEC_BATCH,
                          nc=1, row0=M_PROMPT, out_layer=l, prev_out=yr, prev_state=prev["ret_s"])

    dm = ATT_HEADS * ATT_HEAD_DIM
    knf_s, knb_s, vb_s = _kv_prep_sample(u_tail, lp["k_norm"])
    hist = cache_k.shape[2]
    assert hist == WINDOW
    tbl = _bias_table(lp["rel_bias"])
    ya, knf_p = _attn_prompt_call(u_tail, tbl[:, :, :WINDOW], tbl[:, :, WINDOW:], lp["q_norm"], lp["k_norm"])
    bias1_s, bias2_s = _head_masked_bias(tbl[:, :DEC_SEQ, :hist], tbl[:, :DEC_SEQ, hist:hist + DEC_SEQ])
    ya = _attn_sample_call(u_tail, cache_k, cache_v, l, knb_s[0], vb_s[0], bias1_s, bias2_s, lp["q_norm"], ya)

    m1 = _matmul(ys, w["w_br_ssm"], l, n=D_MODEL, out_dtype=F32, tm=1088, tn=1024, w_single_buffer=True,
                 gate=u_tail, gate_col0=T_GA, name="merge_ssm")
    m2 = _matmul(yr, w["w_br_ret"], l, n=D_MODEL, out_dtype=F32, tm=1088, tn=1024, w_single_buffer=True,
                 gate=u_tail, gate_col0=T_GB, addend=m1, name="merge_ret")
    mm = _matmul(ya, w["w_br_att"], l, n=D_MODEL, out_dtype=BF16, tm=1088, tn=1024, w_single_buffer=True,
                 gate=u_tail, gate_col0=T_GC, addend=m2, name="merge_att")
    x1 = _matmul(mm, w["w_out"], l, n=D_MODEL, out_dtype=F32, tm=M_SAMPLE, tn=1024, addend=x, name="out_proj")

    h2 = _rmsnorm(x1, lp["norm_ffn"])
    hh = _swiglu_matmul(h2, w["w_ffn_in"], l, tm=1088, tn=512)
    x2 = _matmul(hh, w["w_ffn_out"], l, n=D_MODEL, out_dtype=F32, tm=M_SAMPLE, tn=512, addend=x1,
                 split_out=(l == DEPTH - 1), name="ffn_out")

    keep = min(WINDOW, SEQ)
    nconv = CONV_W - 1

    def conv_rows(row0):
        return jnp.concatenate([_row_block(u_head, row0, nconv, H_X, D_SSM),
                                _row_block(u_head, row0, nconv, H_B, BC_WIDTH),
                                _row_block(u_head, row0, nconv, H_C, BC_WIDTH)], axis=-1)

    pv = jnp.stack([_row_block(u_tail, (b + 1) * SEQ - keep, keep, T_AV, dm) for b in range(BATCH)])
    pconv = jnp.stack([conv_rows((b + 1) * SEQ - nconv) for b in range(BATCH)])
    sv = _row_block(u_tail, M_PROMPT, M_SAMPLE, T_AV, dm)
    s_xbc = jnp.concatenate([_row_block(u_head, M_PROMPT, M_SAMPLE, H_X, D_SSM),
                             _row_block(u_head, M_PROMPT, M_SAMPLE, H_B, BC_WIDTH),
                             _row_block(u_head, M_PROMPT, M_SAMPLE, H_C, BC_WIDTH)], axis=-1)
    sconv = s_xbc.reshape(DEC_BATCH, DEC_SEQ, CONV_DIM)[:, DEC_SEQ - nconv:]

    outs_p = (knf_p.reshape(BATCH, keep, ATT_HEADS, ATT_HEAD_DIM),
              pv.reshape(BATCH, keep, ATT_HEADS, ATT_HEAD_DIM),
              pconv)
    outs_s = (knf_s.reshape(DEC_BATCH, DEC_SEQ, ATT_HEADS, ATT_HEAD_DIM),
              sv.reshape(DEC_BATCH, DEC_SEQ, ATT_HEADS, ATT_HEAD_DIM),
              sconv)
    new_states = {"ssm_p": ssm_p, "ssm_s": ssm_s, "ret_p": ret_p, "ret_s": ret_s}
    return x2, outs_p, outs_s, new_states


def kernel(x_prompt, x_sample, cache_attn_k, cache_attn_v, state_ret, state_ssm, state_conv,
           norm_mix, w_in, conv_w, conv_b, dt_bias, a_log, d_skip, ssm_norm, q_norm, k_norm,
           rel_bias, w_br_ssm, w_br_ret, w_br_att, w_out, norm_ffn, w_ffn_in, w_ffn_out):
    x = (x_prompt.reshape(M_PROMPT, D_MODEL), x_sample.reshape(M_SAMPLE, D_MODEL))
    cos_p, sin_p = _rope_tables(jnp.arange(SEQ))
    cos_s, sin_s = _rope_tables(PAST_LEN + jnp.arange(DEC_SEQ))
    tables = {"cos_p": cos_p, "sin_p": sin_p, "cos_s": cos_s, "sin_s": sin_s,
              "zeros_hist": jnp.zeros((1, BATCH, SUBLANES, CONV_DIM), F32),
              "zeros_h": jnp.zeros((1, BATCH, D_SSM, SSM_STATE), F32),
              "zeros_s": jnp.zeros((1, BATCH, RET_HEADS, RET_DK, RET_DV), F32)}
    w = {"w_in_t": jnp.swapaxes(w_in, 1, 2), "w_br_ssm": w_br_ssm, "w_br_ret": w_br_ret, "w_br_att": w_br_att, "w_out": w_out,
         "w_ffn_in": w_ffn_in, "w_ffn_out": w_ffn_out}
    hist8 = jnp.pad(state_conv, ((0, 0), (0, 0), (SUBLANES - (CONV_W - 1), 0), (0, 0)))
    state = (cache_attn_k, cache_attn_v, state_ret,
             state_ssm.reshape(DEPTH, DEC_BATCH, D_SSM, SSM_STATE), hist8)
    acc_p = ([], [], [])
    acc_s = ([], [], [])
    prev = {"ssm_p": None, "ssm_s": None, "ret_p": None, "ret_s": None}
    for l in range(DEPTH):
        lp = _layer_params(l, norm_mix, w_in, conv_w, conv_b, dt_bias, a_log, d_skip, ssm_norm, q_norm,
                           k_norm, rel_bias, w_br_ssm, w_br_ret, w_br_att, w_out, norm_ffn, w_ffn_in,
                           w_ffn_out)
        x, outs_p, outs_s, prev = _layer(x, l, lp, w, state, tables, prev)
        for i in range(3):
            acc_p[i].append(outs_p[i])
            acc_s[i].append(outs_s[i])
    pk, pv, pconv = [jnp.stack(a) for a in acc_p]
    sk, sv, sconv = [jnp.stack(a) for a in acc_s]
    pr, sr = prev["ret_p"], prev["ret_s"]
    pssm = prev["ssm_p"].reshape(DEPTH, BATCH, SSM_HEADS, SSM_HEAD_DIM, SSM_STATE)
    sssm = prev["ssm_s"].reshape(DEPTH, DEC_BATCH, SSM_HEADS, SSM_HEAD_DIM, SSM_STATE)
    yp = x[0].reshape(BATCH, SEQ, D_MODEL)
    ys = x[1].reshape(DEC_BATCH, DEC_SEQ, D_MODEL)
    return (yp, ys, pk, pv, pr, pssm, pconv, sk, sv, sr, sssm, sconv)
```

```python
import functools
import math

import numpy as np
import jax
import jax.numpy as jnp
from jax import lax
from jax.experimental import pallas as pl
from jax.experimental.pallas import tpu as pltpu

F32 = jnp.float32
BF16 = jnp.bfloat16

D_MODEL = 2048
BATCH = 4
SEQ = 2048
DEPTH = 2
DEC_BATCH = 32
DEC_SEQ = 16
PAST_LEN = 1024
CHUNK = 64
EPS = 1e-6
NEG_INF = -1e30

D_SSM = 2048
SSM_HEAD_DIM = 64
SSM_HEADS = 32
SSM_GROUPS = 4
SSM_STATE = 128
SSM_GROUP_WIDTH = D_SSM // SSM_GROUPS
CONV_W = 4
CONV_DIM = D_SSM + 2 * SSM_GROUPS * SSM_STATE
BC_WIDTH = SSM_GROUPS * SSM_STATE

RET_HEADS = 8
RET_DK = 128
RET_DV = 256
ROPE_BASE = 10000.0

ATT_HEADS = 16
ATT_HEAD_DIM = 128
BAND_CHUNKS = 8
WINDOW = BAND_CHUNKS * CHUNK
REL_CLIP = 256

FFN_HIDDEN = 5632

M_PROMPT = BATCH * SEQ
M_SAMPLE = DEC_BATCH * DEC_SEQ
M_ALL = M_PROMPT + M_SAMPLE

LANES = 128
SUBLANES = 8
V7X_VMEM_LIMIT_BYTES = 56 * 1024 * 1024

H_Z, H_X, H_B, H_C = 0, 2048, 4096, 4608
HEAD_WIDTH = 5120
DT_COL0 = 5120
TAIL_COL0 = DT_COL0 + SSM_HEADS
T_RQ, T_RK, T_RV, T_RG, T_AQ, T_AK, T_AV, T_GA, T_GB, T_GC = (
    0, 1024, 2048, 4096, 6144, 8192, 10240, 12288, 14336, 16384)
TAIL_WIDTH = 18432
N_IN = TAIL_COL0 + TAIL_WIDTH


def _params(sem):
    return pltpu.CompilerParams(dimension_semantics=sem, vmem_limit_bytes=V7X_VMEM_LIMIT_BYTES)


def _append_aliased(in_specs, args, out_buffers):
    aliases = {}
    for k, buf in enumerate(out_buffers):
        if buf is not None:
            in_specs.append(pl.BlockSpec(memory_space=pl.ANY))
            args.append(buf)
            aliases[len(args) - 1] = k
    return aliases


def _sigmoid(x):
    return 1.0 / (1.0 + jnp.exp(-x))


def _silu(x):
    return x * _sigmoid(x)


def _rmsnorm_kernel(x_ref, g_ref, o_ref):
    x = x_ref[...]
    ms = jnp.mean(x * x, axis=-1, keepdims=True)
    o_ref[...] = (x * lax.rsqrt(ms + EPS) * g_ref[...]).astype(o_ref.dtype)


def _rmsnorm_split_kernel(xp_ref, xs_ref, g_ref, o_ref):
    last = pl.num_programs(0) - 1

    @pl.when(pl.program_id(0) < last)
    def _():
        _rmsnorm_kernel(xp_ref, g_ref, o_ref)

    @pl.when(pl.program_id(0) == last)
    def _():
        _rmsnorm_kernel(xs_ref, g_ref, o_ref)


def _rmsnorm(x, g, tm=544):
    d = g.shape[-1]
    gspec = pl.BlockSpec((1, d), lambda i: (0, 0))
    if isinstance(x, (tuple, list)):
        tm = M_SAMPLE
        last_prompt_tile = M_PROMPT // tm - 1
        return pl.pallas_call(
            _rmsnorm_split_kernel,
            grid=(M_ALL // tm,),
            in_specs=[pl.BlockSpec((tm, d), lambda i: (jnp.minimum(i, last_prompt_tile), 0)),
                      pl.BlockSpec((tm, d), lambda i: (0, 0)), gspec],
            out_specs=pl.BlockSpec((tm, d), lambda i: (i, 0)),
            out_shape=jax.ShapeDtypeStruct((M_ALL, d), BF16),
            compiler_params=_params(("arbitrary",)),
            name="rmsnorm_split",
        )(x[0], x[1], g.reshape(1, d))
    m = x.shape[0]
    return pl.pallas_call(
        _rmsnorm_kernel,
        grid=(m // tm,),
        in_specs=[pl.BlockSpec((tm, d), lambda i: (i, 0)), gspec],
        out_specs=pl.BlockSpec((tm, d), lambda i: (i, 0)),
        out_shape=jax.ShapeDtypeStruct((m, d), BF16),
        compiler_params=_params(("parallel",)),
        name="rmsnorm",
    )(x, g.reshape(1, d))


W_CAST_ROWS = 256


def _cast_weight_tile(w_ref, wx_ref, wb_ref, shift):
    rows = wb_ref.shape[0]
    for r in range(0, rows - shift, W_CAST_ROWS):
        n = min(W_CAST_ROWS, rows - shift - r)
        wb_ref[r:r + n, :] = w_ref[r + shift:r + shift + n, :].astype(BF16)
    if shift:
        wb_ref[rows - shift:, :] = wx_ref[:shift, :].astype(BF16)


def _mm_kernel(*refs, shift, n_add, n_out, w_transposed):
    a_ref, w_ref = refs[0], refs[1]
    pos = 2
    wx_ref = None
    if shift:
        wx_ref = refs[pos]
        pos += 1
    add_refs = refs[pos:pos + n_add]
    pos += n_add
    o_refs = refs[pos:pos + n_out]
    wb_ref = refs[pos + n_out]
    i = pl.program_id(1)

    @pl.when(i == 0)
    def _():
        _cast_weight_tile(w_ref, wx_ref, wb_ref, shift)

    if w_transposed:
        r = _dot_nt(a_ref[...], wb_ref[...])
    else:
        r = jnp.dot(a_ref[...], wb_ref[...], preferred_element_type=F32)

    def finish(add_ref, o_ref):
        res = r if add_ref is None else add_ref[...] + r
        o_ref[...] = res.astype(o_ref.dtype)

    if n_add < 2 and n_out < 2:
        finish(add_refs[0] if n_add else None, o_refs[0])
    else:
        last = pl.num_programs(1) - 1

        @pl.when(i < last)
        def _():
            finish(add_refs[0] if n_add else None, o_refs[0])

        @pl.when(i == last)
        def _():
            finish(add_refs[-1] if n_add else None, o_refs[-1])


def _matmul(a, w, layer, *, n, out_dtype, tm, tn, w_col0=0, w_transposed=False, addend=None, split_out=False,
            name="matmul"):
    m, kdim = a.shape
    add_parts = isinstance(addend, (tuple, list))
    if add_parts or split_out:
        assert tm == M_SAMPLE and M_PROMPT % tm == 0 and m == M_ALL
    last_prompt_tile = M_PROMPT // tm - 1
    wj0 = w_col0 // tn
    shift = w_col0 - wj0 * tn
    extra = LANES
    assert shift == 0 or (w_transposed and shift % SUBLANES == 0 and shift <= extra and tn % extra == 0)
    if w_transposed:
        w_spec = pl.BlockSpec((None, tn, kdim), lambda j, i: (layer, j + wj0, 0))
        wb_shape = (tn, kdim)
    else:
        w_spec = pl.BlockSpec((None, kdim, tn), lambda j, i: (layer, 0, j + wj0))
        wb_shape = (kdim, tn)
    in_specs = [pl.BlockSpec((tm, kdim), lambda j, i: (i, 0)), w_spec]
    args = [a, w]
    if shift:
        in_specs.append(pl.BlockSpec((None, extra, kdim), lambda j, i: (layer, (j + wj0 + 1) * (tn // extra), 0)))
        args.append(w)
    prompt_rows = pl.BlockSpec((tm, tn), lambda j, i: (jnp.minimum(i, last_prompt_tile), j))
    sample_rows = pl.BlockSpec((tm, tn), lambda j, i: (0, j))
    all_rows = pl.BlockSpec((tm, tn), lambda j, i: (i, j))
    if add_parts:
        in_specs += [prompt_rows, sample_rows]
        args += list(addend)
    elif addend is not None:
        in_specs.append(all_rows)
        args.append(addend)
    n_add = 2 if add_parts else int(addend is not None)
    if split_out:
        out_specs = [prompt_rows, sample_rows]
        out_shape = [jax.ShapeDtypeStruct((M_PROMPT, n), out_dtype), jax.ShapeDtypeStruct((M_SAMPLE, n), out_dtype)]
    else:
        out_specs = all_rows
        out_shape = jax.ShapeDtypeStruct((m, n), out_dtype)
    return pl.pallas_call(
        functools.partial(_mm_kernel, shift=shift, n_add=n_add, n_out=2 if split_out else 1,
                          w_transposed=w_transposed),
        grid=(n // tn, m // tm),
        in_specs=in_specs,
        out_specs=out_specs,
        out_shape=out_shape,
        scratch_shapes=[pltpu.VMEM(wb_shape, BF16)],
        compiler_params=_params(("parallel", "arbitrary")),
        name=name,
    )(*args)


def _merge_kernel(a0_ref, a1_ref, a2_ref, w0_ref, w1_ref, w2_ref, g0_ref, g1_ref, g2_ref, o_ref,
                  wb0_ref, wb1_ref, wb2_ref):
    @pl.when(pl.program_id(1) == 0)
    def _():
        for w_ref, wb_ref in ((w0_ref, wb0_ref), (w1_ref, wb1_ref), (w2_ref, wb2_ref)):
            _cast_weight_tile(w_ref, None, wb_ref, 0)

    acc = None
    for a_ref, wb_ref, g_ref in ((a0_ref, wb0_ref, g0_ref), (a1_ref, wb1_ref, g1_ref), (a2_ref, wb2_ref, g2_ref)):
        term = _sigmoid(g_ref[...]) * jnp.dot(a_ref[...], wb_ref[...], preferred_element_type=F32)
        acc = term if acc is None else acc + term
    o_ref[...] = acc.astype(o_ref.dtype)


def _gated_merge(branches, weights, layer, gates, gate_cols, *, tm, tn):
    m, kdim = branches[0].shape
    a_spec = pl.BlockSpec((tm, kdim), lambda j, i: (i, 0))
    w_spec = pl.BlockSpec((None, kdim, tn), lambda j, i: (layer, 0, j), pipeline_mode=pl.Buffered(1))
    g_specs = [pl.BlockSpec((tm, tn), functools.partial(lambda j, i, j0: (i, j + j0), j0=col // tn))
               for col in gate_cols]
    return pl.pallas_call(
        _merge_kernel,
        grid=(D_MODEL // tn, m // tm),
        in_specs=[a_spec] * 3 + [w_spec] * 3 + g_specs,
        out_specs=pl.BlockSpec((tm, tn), lambda j, i: (i, j)),
        out_shape=jax.ShapeDtypeStruct((m, D_MODEL), BF16),
        scratch_shapes=[pltpu.VMEM((kdim, tn), BF16)] * 3,
        compiler_params=_params(("parallel", "arbitrary")),
        name="gated_merge",
    )(*branches, *weights, gates, gates, gates)


def _swiglu_kernel(a_ref, wa_ref, wc_ref, o_ref, wab_ref, wcb_ref):
    @pl.when(pl.program_id(1) == 0)
    def _():
        _cast_weight_tile(wa_ref, None, wab_ref, 0)
        _cast_weight_tile(wc_ref, None, wcb_ref, 0)

    a = a_ref[...]
    fa = jnp.dot(a, wab_ref[...], preferred_element_type=F32)
    fc = jnp.dot(a, wcb_ref[...], preferred_element_type=F32)
    o_ref[...] = (_silu(fa) * fc).astype(o_ref.dtype)


def _swiglu_matmul(a, w, layer, *, tm, tn):
    m, kdim = a.shape
    nj = FFN_HIDDEN // tn
    return pl.pallas_call(
        _swiglu_kernel,
        grid=(nj, m // tm),
        in_specs=[pl.BlockSpec((tm, kdim), lambda j, i: (i, 0)),
                  pl.BlockSpec((None, kdim, tn), lambda j, i: (layer, 0, j)),
                  pl.BlockSpec((None, kdim, tn), lambda j, i: (layer, 0, j + nj))],
        out_specs=pl.BlockSpec((tm, tn), lambda j, i: (i, j)),
        out_shape=jax.ShapeDtypeStruct((m, FFN_HIDDEN), BF16),
        scratch_shapes=[pltpu.VMEM((kdim, tn), BF16), pltpu.VMEM((kdim, tn), BF16)],
        compiler_params=_params(("parallel", "arbitrary")),
        name="ffn_in_swiglu",
    )(a, w, w)


def _split3(a):
    hi = a.astype(BF16)
    r1 = a - hi.astype(F32)
    mid = r1.astype(BF16)
    lo = (r1 - mid.astype(F32)).astype(BF16)
    return hi, mid, lo


def _exact_right(a, onehot):
    return sum(jnp.dot(p, onehot, preferred_element_type=F32) for p in _split3(a))


def _exact_left(onehot, b):
    return sum(jnp.dot(onehot, p, preferred_element_type=F32) for p in _split3(b))


def _dot_nt(a, b):
    return lax.dot_general(a, b, (((1,), (1,)), ((), ())), preferred_element_type=F32)


def _dot_tn(a, b):
    return lax.dot_general(a, b, (((0,), (0,)), ((), ())), preferred_element_type=F32)


def _conv_silu(ext_ref, cw, cb, L):
    acc = ext_ref[SUBLANES:SUBLANES + L, :] * cw[CONV_W - 1:CONV_W, :] + cb
    for s in range(1, CONV_W):
        acc = acc + ext_ref[SUBLANES - s:SUBLANES - s + L, :] * cw[CONV_W - 1 - s:CONV_W - s, :]
    return _silu(acc)


def _ssd_kernel(x_ref, b_ref, c_ref, z_ref, dt_ref, hist_ref, h0_ref,
                cwx_ref, cwb_ref, cwc_ref, cbx_ref, cbb_ref, cbc_ref,
                dtb_ref, alog_ref, dskip_ref, normw_ref, exp_p_ref, exp_l_ref,
                *rest, L, nc, n_aliased):
    y_ref, hout_ref, ex_ref, eb_ref, ec_ref, ht_ref = rest[n_aliased:]
    c = pl.program_id(1)
    lw = SSM_HEADS * L

    @pl.when(c == 0)
    def _():
        hist = hist_ref[0]
        ex_ref[0:SUBLANES, :] = hist[:, :D_SSM]
        eb_ref[0:SUBLANES, :] = hist[:, D_SSM:D_SSM + BC_WIDTH]
        ec_ref[0:SUBLANES, :] = hist[:, D_SSM + BC_WIDTH:]
        ht_ref[...] = h0_ref[0].T

    @pl.when(c > 0)
    def _():
        ex_ref[0:SUBLANES, :] = ex_ref[L:L + SUBLANES, :]
        eb_ref[0:SUBLANES, :] = eb_ref[L:L + SUBLANES, :]
        ec_ref[0:SUBLANES, :] = ec_ref[L:L + SUBLANES, :]

    ex_ref[SUBLANES:SUBLANES + L, :] = x_ref[...]
    eb_ref[SUBLANES:SUBLANES + L, :] = b_ref[...]
    ec_ref[SUBLANES:SUBLANES + L, :] = c_ref[...]
    xs = _conv_silu(ex_ref, cwx_ref[...], cbx_ref[...], L)
    bm = _conv_silu(eb_ref, cwb_ref[...], cbb_ref[...], L).astype(BF16)
    cm = _conv_silu(ec_ref, cwc_ref[...], cbc_ref[...], L).astype(BF16)

    dtr = dt_ref[...] + dtb_ref[...]
    dt = jnp.maximum(dtr, 0.0) + jnp.log1p(jnp.exp(-jnp.abs(dtr)))

    tri = (lax.broadcasted_iota(jnp.int32, (L, L), 1)
           <= lax.broadcasted_iota(jnp.int32, (L, L), 0)).astype(BF16)
    acum = _exact_left(tri, dt * (-jnp.exp(alog_ref[...])))

    dt_p = _exact_right(dt, exp_p_ref[...])
    acum_p = _exact_right(acum, exp_p_ref[...])
    last_p = acum_p[L - 1:L, :]
    if L == SSM_HEAD_DIM:
        dt_l, acum_l = dt_p, acum_p
    else:
        dt_l = _exact_right(dt, exp_l_ref[...])
        acum_l = _exact_right(acum, exp_l_ref[...])

    rowi = lax.broadcasted_iota(jnp.int32, (L, lw), 0)
    colj = lax.broadcasted_iota(jnp.int32, (L, lw), 1) % L
    diag = colj == rowi
    causal = colj <= rowi
    acum_row = jnp.sum(jnp.where(diag, acum_l, 0.0), axis=0, keepdims=True)
    dt_row = jnp.sum(jnp.where(diag, dt_l, 0.0), axis=0, keepdims=True)
    decay = jnp.where(causal, jnp.exp(jnp.where(causal, acum_l - acum_row, 0.0)), 0.0)

    heads_per_group = SSM_HEADS // SSM_GROUPS
    cb = jnp.concatenate(
        [_dot_nt(cm[:, g * SSM_STATE:(g + 1) * SSM_STATE],
                 jnp.concatenate([bm[:, g * SSM_STATE:(g + 1) * SSM_STATE]] * heads_per_group, axis=0))
         for g in range(SSM_GROUPS)], axis=1)

    ht = ht_ref[...]
    ht_b = ht.astype(BF16)
    to_end = jnp.exp(last_p - acum_p) * dt_p
    xw = (xs * to_end).astype(BF16)
    y_inter = []
    st_parts = []
    for g in range(SSM_GROUPS):
        gs = slice(g * SSM_GROUP_WIDTH, (g + 1) * SSM_GROUP_WIDTH)
        ns = slice(g * SSM_STATE, (g + 1) * SSM_STATE)
        y_inter.append(jnp.dot(cm[:, ns], ht_b[:, gs], preferred_element_type=F32))
        st_parts.append(_dot_tn(bm[:, ns], xw[:, gs]))
    ht_new = jnp.exp(last_p) * ht + jnp.concatenate(st_parts, axis=1)
    ht_ref[...] = ht_new

    m_all = (cb * decay * dt_row).astype(BF16)
    hp = LANES // L
    slab = hp * SSM_HEAD_DIM
    xs_b = xs.astype(BF16)
    lane_head = lax.broadcasted_iota(jnp.int32, (L, slab), 1) // SSM_HEAD_DIM
    y_parts = []
    for q in range(lw // LANES):
        xq = xs_b[:, q * slab:(q + 1) * slab]
        bd = jnp.concatenate([jnp.where(lane_head == h, xq, jnp.zeros_like(xq)) for h in range(hp)], axis=0)
        y_parts.append(jnp.dot(m_all[:, q * LANES:(q + 1) * LANES], bd, preferred_element_type=F32))
    y = (jnp.concatenate(y_parts, axis=1) + jnp.concatenate(y_inter, axis=1) * jnp.exp(acum_p)
         + dskip_ref[...] * xs)

    t = y * _silu(z_ref[...])
    outs = []
    for g in range(SSM_GROUPS):
        tg = t[:, g * SSM_GROUP_WIDTH:(g + 1) * SSM_GROUP_WIDTH]
        outs.append(tg * lax.rsqrt(jnp.mean(tg * tg, axis=-1, keepdims=True) + EPS))
    y_ref[...] = (jnp.concatenate(outs, axis=1) * normw_ref[...]).astype(y_ref.dtype)

    @pl.when(c == nc - 1)
    def _():
        hout_ref[0] = ht_new.T


def _ssd_call(u, dt_raw, hist8, h0, layer, lp, *, L, nb, nc, row0, out_layer, prev_out, prev_state):
    rb0 = row0 // L

    def rows(width, colblock):
        return pl.BlockSpec((L, width), lambda b, c: (rb0 + b * nc + c, colblock))

    def const(shape):
        return pl.BlockSpec(shape, lambda b, c: tuple(0 for _ in shape))

    lw = SSM_HEADS * L
    in_specs = [rows(D_SSM, H_X // D_SSM), rows(BC_WIDTH, H_B // BC_WIDTH), rows(BC_WIDTH, H_C // BC_WIDTH),
                rows(D_SSM, H_Z // D_SSM),
                pl.BlockSpec((L, LANES), lambda b, c: (rb0 + b * nc + c, 0)),
                pl.BlockSpec((None, 1, SUBLANES, CONV_DIM), lambda b, c: (layer, b, 0, 0)),
                pl.BlockSpec((None, 1, D_SSM, SSM_STATE), lambda b, c: (layer, b, 0, 0)),
                const((CONV_W, D_SSM)), const((CONV_W, BC_WIDTH)), const((CONV_W, BC_WIDTH)),
                const((1, D_SSM)), const((1, BC_WIDTH)), const((1, BC_WIDTH)),
                const((1, LANES)), const((1, LANES)), const((1, D_SSM)), const((1, D_SSM)),
                const((LANES, D_SSM)), const((LANES, lw))]
    args = [u, u, u, u, dt_raw, hist8, h0,
            lp["conv_wx"], lp["conv_wb"], lp["conv_wc"], lp["conv_bx"], lp["conv_bb"], lp["conv_bc"],
            lp["dt_bias"], lp["a_log"], lp["d_skip_p"], lp["ssm_norm"],
            _expand_onehot(SSM_HEAD_DIM), _expand_onehot(L)]
    aliases = _append_aliased(in_specs, args, (prev_out, prev_state))
    y, hout = pl.pallas_call(
        functools.partial(_ssd_kernel, L=L, nc=nc, n_aliased=len(aliases)),
        grid=(nb, nc),
        in_specs=in_specs,
        out_specs=[pl.BlockSpec((L, D_SSM), lambda b, c: (rb0 + b * nc + c, 0)),
                   pl.BlockSpec((None, 1, D_SSM, SSM_STATE), lambda b, c: (out_layer, b, 0, 0))],
        out_shape=[jax.ShapeDtypeStruct((M_ALL, D_SSM), BF16),
                   jax.ShapeDtypeStruct((DEPTH, nb, D_SSM, SSM_STATE), F32)],
        scratch_shapes=[pltpu.VMEM((SUBLANES + L, D_SSM), F32), pltpu.VMEM((SUBLANES + L, BC_WIDTH), F32),
                        pltpu.VMEM((SUBLANES + L, BC_WIDTH), F32), pltpu.VMEM((SSM_STATE, D_SSM), F32)],
        input_output_aliases=aliases,
        compiler_params=_params(("parallel", "arbitrary")),
        name="ssd_L%d" % L,
    )(*args)
    return y, hout


def _expand_onehot(per_head):
    e = np.zeros((LANES, SSM_HEADS * per_head), np.float32)
    for h in range(SSM_HEADS):
        e[h, h * per_head:(h + 1) * per_head] = 1.0
    return jnp.asarray(e, BF16)


def _ret_consts(L):
    lam = np.log1p(-np.exp2(-5.0 - np.arange(RET_HEADS, dtype=np.float64)))
    i = np.arange(L, dtype=np.float64)
    diff = i[:, None] - i[None, :]
    dmat = np.where(diff >= 0, np.exp(lam[:, None, None] * np.maximum(diff, 0.0)), 0.0)
    to_end = np.exp(lam[:, None] * (L - 1 - i)[None, :])
    q_dec = np.exp(lam[:, None] * (i + 1.0)[None, :])
    chunk_decay = np.exp(lam * L)
    expand = lambda t: np.repeat(t.T[:, :, None], RET_DK, axis=2).reshape(L, RET_HEADS * RET_DK)
    return (jnp.asarray(dmat, F32), jnp.asarray(expand(q_dec), F32), jnp.asarray(expand(to_end), F32),
            [float(v) for v in chunk_decay])


RET_HEADS_AHEAD = 8


def _ret_kernel(q_ref, k_ref, v_ref, g_ref, cos_ref, sin_ref, qdec_ref, kend_ref, dmat_ref, s0_ref,
                *rest, L, nc, chunk_decay, n_aliased):
    y_ref, sout_ref, s_ref = rest[n_aliased:]
    c = pl.program_id(1)

    @pl.when(c == 0)
    def _():
        s_ref[...] = s0_ref[0]

    cos = cos_ref[...]
    sin = sin_ref[...]
    half = RET_DK // 2
    def first_stage(h):
        ks = slice(h * RET_DK, (h + 1) * RET_DK)
        qh = q_ref[:, ks]
        kh = k_ref[:, ks]
        qr = qh * cos + pltpu.roll(qh, half, 1) * sin
        kr = (kh * cos + pltpu.roll(kh, half, 1) * sin) * (RET_DK ** -0.5)
        vh = v_ref[:, h * RET_DV:(h + 1) * RET_DV].astype(BF16)
        s_prev = s_ref[h]
        qk = _dot_nt(qr.astype(BF16), kr.astype(BF16))
        o_state = jnp.dot((qr * qdec_ref[:, ks]).astype(BF16), s_prev.astype(BF16),
                          preferred_element_type=F32)
        kv = _dot_tn((kr * kend_ref[:, ks]).astype(BF16), vh)
        s_ref[h] = chunk_decay[h] * s_prev + kv
        return qk, o_state, vh

    def second_stage(h, qk, o_state, vh):
        vs = slice(h * RET_DV, (h + 1) * RET_DV)
        s = qk * dmat_ref[h]
        o = jnp.dot(s.astype(BF16), vh, preferred_element_type=F32) + o_state
        o = o * lax.rsqrt(jnp.mean(o * o, axis=-1, keepdims=True) + EPS)
        y_ref[:, vs] = (o * _silu(g_ref[:, vs])).astype(y_ref.dtype)

    pending = [first_stage(h) for h in range(RET_HEADS_AHEAD)]
    for h in range(RET_HEADS):
        if h + RET_HEADS_AHEAD < RET_HEADS:
            pending.append(first_stage(h + RET_HEADS_AHEAD))
        second_stage(h, *pending.pop(0))

    @pl.when(c == nc - 1)
    def _():
        sout_ref[0] = s_ref[...]


def _ret_call(u, cos_t, sin_t, s0, layer, *, L, nb, nc, row0, out_layer, prev_out, prev_state):
    rb0 = row0 // L
    dmat, qdec, kend, chunk_decay = _ret_consts(L)
    hk = RET_HEADS * RET_DK
    hv = RET_HEADS * RET_DV

    def rows(width, colblock):
        return pl.BlockSpec((L, width), lambda b, c: (rb0 + b * nc + c, colblock))

    in_specs = [rows(hk, T_RQ // hk), rows(hk, T_RK // hk), rows(hv, T_RV // hv), rows(hv, T_RG // hv),
                pl.BlockSpec((L, RET_DK), lambda b, c: (c, 0)),
                pl.BlockSpec((L, RET_DK), lambda b, c: (c, 0)),
                pl.BlockSpec((L, hk), lambda b, c: (0, 0)),
                pl.BlockSpec((L, hk), lambda b, c: (0, 0)),
                pl.BlockSpec((RET_HEADS, L, L), lambda b, c: (0, 0, 0)),
                pl.BlockSpec((None, 1, RET_HEADS, RET_DK, RET_DV), lambda b, c: (layer, b, 0, 0, 0))]
    args = [u, u, u, u, cos_t, sin_t, qdec, kend, dmat, s0]
    aliases = _append_aliased(in_specs, args, (prev_out, prev_state))
    y, sout = pl.pallas_call(
        functools.partial(_ret_kernel, L=L, nc=nc, chunk_decay=chunk_decay, n_aliased=len(aliases)),
        grid=(nb, nc),
        in_specs=in_specs,
        out_specs=[pl.BlockSpec((L, hv), lambda b, c: (rb0 + b * nc + c, 0)),
                   pl.BlockSpec((None, 1, RET_HEADS, RET_DK, RET_DV), lambda b, c: (out_layer, b, 0, 0, 0))],
        out_shape=[jax.ShapeDtypeStruct((M_ALL, hv), BF16),
                   jax.ShapeDtypeStruct((DEPTH, nb, RET_HEADS, RET_DK, RET_DV), F32)],
        scratch_shapes=[pltpu.VMEM((RET_HEADS, RET_DK, RET_DV), F32)],
        input_output_aliases=aliases,
        compiler_params=_params(("parallel", "arbitrary")),
        name="retention_L%d" % L,
    )(*args)
    return y, sout


def _rope_tables(pos):
    half = RET_DK // 2
    inv = ROPE_BASE ** (-jnp.arange(half, dtype=F32) * 2.0 / RET_DK)
    ang = pos.astype(F32)[:, None] * inv[None, :]
    cos = jnp.cos(ang)
    sin = jnp.sin(ang)
    return jnp.concatenate([cos, cos], axis=1), jnp.concatenate([-sin, sin], axis=1)


def _head_norm(xh, g):
    return xh * lax.rsqrt(jnp.mean(xh * xh, axis=-1, keepdims=True) + EPS) * g


def _kv_prep_kernel(k_ref, v_ref, gk_ref, knf_ref, knb_ref, vb_ref):
    for h in range(ATT_HEADS):
        hs = slice(h * ATT_HEAD_DIM, (h + 1) * ATT_HEAD_DIM)
        kn = _head_norm(k_ref[:, hs], gk_ref[...])
        knf_ref[0, :, hs] = kn
        knb_ref[0, :, hs] = kn.astype(BF16)
    vb_ref[0] = v_ref[...].astype(BF16)


def _kv_prep_sample(u, gk):
    dm = ATT_HEADS * ATT_HEAD_DIM
    rb = M_PROMPT // M_SAMPLE
    return pl.pallas_call(
        _kv_prep_kernel,
        grid=(1,),
        in_specs=[pl.BlockSpec((M_SAMPLE, dm), lambda i: (rb, T_AK // dm)),
                  pl.BlockSpec((M_SAMPLE, dm), lambda i: (rb, T_AV // dm)),
                  pl.BlockSpec((1, ATT_HEAD_DIM), lambda i: (0, 0))],
        out_specs=[pl.BlockSpec((1, M_SAMPLE, dm), lambda i: (0, 0, 0)),
                   pl.BlockSpec((1, M_SAMPLE, dm), lambda i: (0, 0, 0)),
                   pl.BlockSpec((1, M_SAMPLE, dm), lambda i: (0, 0, 0))],
        out_shape=[jax.ShapeDtypeStruct((1, M_SAMPLE, dm), F32),
                   jax.ShapeDtypeStruct((1, M_SAMPLE, dm), BF16),
                   jax.ShapeDtypeStruct((1, M_SAMPLE, dm), BF16)],
        compiler_params=_params(("arbitrary",)),
        name="kv_prep_sample",
    )(u, u, gk)


def _scores(qn, k1, k2, bias1, bias2, valid1):
    scale = ATT_HEAD_DIM ** -0.5
    s1 = _dot_nt(qn, k1) * scale + bias1
    if valid1 is not None:
        s1 = jnp.where(valid1, s1, NEG_INF)
    s2 = _dot_nt(qn, k2) * scale + bias2
    return s1, s2


def _softmax_pv(s1, s2, v1, v2):
    m = jnp.maximum(jnp.max(s1, axis=-1, keepdims=True), jnp.max(s2, axis=-1, keepdims=True))
    p1 = jnp.exp(s1 - m)
    p2 = jnp.exp(s2 - m)
    inv = 1.0 / (jnp.sum(p1, axis=-1, keepdims=True) + jnp.sum(p2, axis=-1, keepdims=True))
    return (jnp.dot((p1 * inv).astype(BF16), v1, preferred_element_type=F32)
            + jnp.dot((p2 * inv).astype(BF16), v2, preferred_element_type=F32))


def _attend(qn, k1, v1, k2, v2, bias1, bias2, valid1):
    s1, s2 = _scores(qn, k1, k2, bias1, bias2, valid1)
    return _softmax_pv(s1, s2, v1, v2)


ATTN_HEADS_AHEAD = 16


def _attn_prompt_kernel(q_ref, k_ref, v_ref, b1_ref, b2_ref, gq_ref, gk_ref, y_ref, knf_ref, kb_ref, vb_ref):
    c = pl.program_id(1)
    base = pl.multiple_of(c * CHUNK, CHUNK)
    own = pl.multiple_of(c * CHUNK + WINDOW, CHUNK)
    heads = [slice(h * ATT_HEAD_DIM, (h + 1) * ATT_HEAD_DIM) for h in range(ATT_HEADS)]

    @pl.when(c == 0)
    def _():
        kb_ref[0:WINDOW, :] = jnp.zeros((WINDOW, kb_ref.shape[1]), BF16)
        vb_ref[0:WINDOW, :] = jnp.zeros((WINDOW, vb_ref.shape[1]), BF16)

    for hs in heads:
        kn = _head_norm(k_ref[:, hs], gk_ref[...])
        knf_ref[0, :, hs] = kn
        kb_ref[pl.ds(own, CHUNK), hs] = kn.astype(BF16)
    vb_ref[pl.ds(own, CHUNK), :] = v_ref[...].astype(BF16)

    col = lax.broadcasted_iota(jnp.int32, (CHUNK, WINDOW), 1)
    valid1 = col + c * CHUNK >= WINDOW

    def scores(h):
        hs = heads[h]
        qn = _head_norm(q_ref[:, hs], gq_ref[...]).astype(BF16)
        return _scores(qn, kb_ref[pl.ds(base, WINDOW), hs], kb_ref[pl.ds(own, CHUNK), hs],
                       b1_ref[h], b2_ref[h], valid1)

    pending = [scores(h) for h in range(ATTN_HEADS_AHEAD)]
    for h in range(ATT_HEADS):
        if h + ATTN_HEADS_AHEAD < ATT_HEADS:
            pending.append(scores(h + ATTN_HEADS_AHEAD))
        s1, s2 = pending.pop(0)
        hs = heads[h]
        o = _softmax_pv(s1, s2, vb_ref[pl.ds(base, WINDOW), hs], vb_ref[pl.ds(own, CHUNK), hs])
        y_ref[:, hs] = o.astype(y_ref.dtype)


def _attn_prompt_call(u, bias1, bias2, gq, gk):
    dm = ATT_HEADS * ATT_HEAD_DIM
    nc = SEQ // CHUNK
    keep_chunks = min(WINDOW, SEQ) // CHUNK
    first_kept = nc - keep_chunks

    def rows(col0):
        return pl.BlockSpec((CHUNK, dm), lambda b, c: (b * nc + c, col0 // dm))

    return pl.pallas_call(
        _attn_prompt_kernel,
        grid=(BATCH, nc),
        in_specs=[rows(T_AQ), rows(T_AK), rows(T_AV),
                  pl.BlockSpec((ATT_HEADS, CHUNK, WINDOW), lambda b, c: (0, 0, 0)),
                  pl.BlockSpec((ATT_HEADS, CHUNK, CHUNK), lambda b, c: (0, 0, 0)),
                  pl.BlockSpec((1, ATT_HEAD_DIM), lambda b, c: (0, 0)),
                  pl.BlockSpec((1, ATT_HEAD_DIM), lambda b, c: (0, 0))],
        out_specs=[pl.BlockSpec((CHUNK, dm), lambda b, c: (b * nc + c, 0)),
                   pl.BlockSpec((1, CHUNK, dm), lambda b, c: (b, jnp.maximum(c - first_kept, 0), 0))],
        out_shape=[jax.ShapeDtypeStruct((M_ALL, dm), BF16),
                   jax.ShapeDtypeStruct((BATCH, keep_chunks * CHUNK, dm), F32)],
        scratch_shapes=[pltpu.VMEM((SEQ + WINDOW, dm), BF16), pltpu.VMEM((SEQ + WINDOW, dm), BF16)],
        compiler_params=_params(("parallel", "arbitrary")),
        name="attn_prompt",
    )(u, u, u, bias1, bias2, gq, gk)


def _attn_sample_kernel(q_ref, kh_ref, vh_ref, kn_ref, vn_ref, b1_ref, b2_ref, gq_ref, prev_ref, y_ref):
    del prev_ref
    heads = [slice(h * ATT_HEAD_DIM, (h + 1) * ATT_HEAD_DIM) for h in range(ATT_HEADS)]
    hist = kh_ref.shape[1]
    qall = jnp.concatenate([_head_norm(q_ref[:, hs], gq_ref[...]).astype(BF16) for hs in heads], axis=0)
    k1 = kh_ref[0].reshape(hist * ATT_HEADS, ATT_HEAD_DIM).astype(BF16)
    v1 = vh_ref[0].reshape(hist * ATT_HEADS, ATT_HEAD_DIM).astype(BF16)
    k2 = jnp.concatenate([kn_ref[:, hs] for hs in heads], axis=0)
    v2 = jnp.concatenate([vn_ref[:, hs] for hs in heads], axis=0)
    o = _attend(qall, k1, v1, k2, v2, b1_ref[...], b2_ref[...], None)
    for h, hs in enumerate(heads):
        y_ref[:, hs] = o[h * DEC_SEQ:(h + 1) * DEC_SEQ].astype(y_ref.dtype)


def _bias_expand_kernel(t1_ref, t2_ref, o1_ref, o2_ref):
    rows, cb = o1_ref.shape
    row_head = lax.broadcasted_iota(jnp.int32, (rows, cb), 0) // DEC_SEQ
    col = lax.broadcasted_iota(jnp.int32, (rows, cb), 1)
    ej = lax.broadcasted_iota(jnp.int32, (LANES, cb), 0)
    ec = lax.broadcasted_iota(jnp.int32, (LANES, cb), 1)
    e1 = jnp.where(ec // ATT_HEADS == ej, 1.0, 0.0).astype(BF16)
    o1_ref[...] = jnp.where(col % ATT_HEADS == row_head, _exact_right(t1_ref[...], e1), NEG_INF)
    n2 = o2_ref.shape[1]
    row_head2 = lax.broadcasted_iota(jnp.int32, (rows, n2), 0) // DEC_SEQ
    col2 = lax.broadcasted_iota(jnp.int32, (rows, n2), 1)
    ej2 = lax.broadcasted_iota(jnp.int32, (LANES, n2), 0)
    ec2 = lax.broadcasted_iota(jnp.int32, (LANES, n2), 1)
    e2 = jnp.where(ec2 % DEC_SEQ == ej2, 1.0, 0.0).astype(BF16)
    o2_ref[...] = jnp.where(col2 // DEC_SEQ == row_head2, _exact_right(t2_ref[...], e2), NEG_INF)


def _head_masked_bias(tbl1, tbl2):
    nh, nq, hist = tbl1.shape
    rows = nh * nq
    t1 = tbl1.reshape(rows, hist)
    t2 = jnp.pad(tbl2.reshape(rows, nq), ((0, 0), (0, LANES - nq)))
    cb = LANES * nh
    return pl.pallas_call(
        _bias_expand_kernel,
        grid=(hist // LANES,),
        in_specs=[pl.BlockSpec((rows, LANES), lambda j: (0, j)),
                  pl.BlockSpec((rows, LANES), lambda j: (0, 0))],
        out_specs=[pl.BlockSpec((rows, cb), lambda j: (0, j)),
                   pl.BlockSpec((rows, nh * nq), lambda j: (0, 0))],
        out_shape=[jax.ShapeDtypeStruct((rows, hist * nh), F32),
                   jax.ShapeDtypeStruct((rows, nh * nq), F32)],
        compiler_params=_params(("arbitrary",)),
        name="bias_expand",
    )(t1, t2)


def _attn_sample_call(u, k_hist, v_hist, layer, kn_new, vn_new, bias1, bias2, gq, prev_out):
    dm = ATT_HEADS * ATT_HEAD_DIM
    rb0 = M_PROMPT // DEC_SEQ
    hist = k_hist.shape[2]
    cache_spec = pl.BlockSpec((None, 1, hist, ATT_HEADS, ATT_HEAD_DIM), lambda b: (layer, b, 0, 0, 0))
    return pl.pallas_call(
        _attn_sample_kernel,
        grid=(DEC_BATCH,),
        in_specs=[pl.BlockSpec((DEC_SEQ, dm), lambda b: (rb0 + b, T_AQ // dm)),
                  cache_spec,
                  cache_spec,
                  pl.BlockSpec((DEC_SEQ, dm), lambda b: (b, 0)),
                  pl.BlockSpec((DEC_SEQ, dm), lambda b: (b, 0)),
                  pl.BlockSpec((ATT_HEADS * DEC_SEQ, hist * ATT_HEADS), lambda b: (0, 0),
                               pipeline_mode=pl.Buffered(1)),
                  pl.BlockSpec((ATT_HEADS * DEC_SEQ, ATT_HEADS * DEC_SEQ), lambda b: (0, 0)),
                  pl.BlockSpec((1, ATT_HEAD_DIM), lambda b: (0, 0)),
                  pl.BlockSpec(memory_space=pl.ANY)],
        out_specs=pl.BlockSpec((DEC_SEQ, dm), lambda b: (rb0 + b, 0)),
        out_shape=jax.ShapeDtypeStruct((M_ALL, dm), BF16),
        input_output_aliases={8: 0},
        compiler_params=_params(("arbitrary",)),
        name="attn_sample",
    )(u, k_hist, v_hist, kn_new, vn_new, bias1, bias2, gq, prev_out)


def _bias_table(rel_bias):
    kbn = WINDOW + CHUNK
    period = kbn + CHUNK
    m = np.arange(period)
    d = np.where(m < kbn, m, m - period)
    idx = np.clip(WINDOW - d, -REL_CLIP, REL_CLIP) + REL_CLIP
    vec = rel_bias[:, idx]
    rows = jnp.tile(vec, (1, CHUNK))[:, :CHUNK * (period - 1)].reshape(ATT_HEADS, CHUNK, period - 1)
    return rows[:, :, :kbn]


def _layer_params(l, norm_mix, w_in, conv_w, conv_b, dt_bias, a_log, d_skip, ssm_norm, q_norm, k_norm,
                  rel_bias, w_br_ssm, w_br_ret, w_br_att, w_out, norm_ffn, w_ffn_in, w_ffn_out):
    assert w_in.shape[-1] == N_IN
    cw = conv_w[l]
    cb = conv_b[l].reshape(1, CONV_DIM)
    lp = {
        "norm_mix": norm_mix[l], "norm_ffn": norm_ffn[l],
        "conv_wx": cw[:, :D_SSM], "conv_wb": cw[:, D_SSM:D_SSM + BC_WIDTH], "conv_wc": cw[:, D_SSM + BC_WIDTH:],
        "conv_bx": cb[:, :D_SSM], "conv_bb": cb[:, D_SSM:D_SSM + BC_WIDTH], "conv_bc": cb[:, D_SSM + BC_WIDTH:],
        "dt_bias": jnp.pad(dt_bias[l], (0, LANES - SSM_HEADS)).reshape(1, LANES),
        "a_log": jnp.pad(a_log[l], (0, LANES - SSM_HEADS)).reshape(1, LANES),
        "d_skip_p": jnp.repeat(d_skip[l], SSM_HEAD_DIM).reshape(1, D_SSM),
        "ssm_norm": ssm_norm[l].reshape(1, D_SSM),
        "q_norm": q_norm[l].reshape(1, ATT_HEAD_DIM), "k_norm": k_norm[l].reshape(1, ATT_HEAD_DIM),
        "rel_bias": rel_bias[l],
    }
    return lp


def _row_block(t, row0, rows, col0, width):
    return lax.slice(t, (row0, col0), (row0 + rows, col0 + width))


def _layer(x, l, lp, w, state, tables, prev):
    cache_k, cache_v, state_ret, state_ssm4, hist8 = state
    hn = _rmsnorm(x, lp["norm_mix"])
    u_head = _matmul(hn, w["w_in_t"], l, n=HEAD_WIDTH, out_dtype=F32, tm=1088, tn=1024, w_transposed=True,
                     name="in_proj_head")
    dt_raw = _matmul(hn, w["w_in_t"], l, n=LANES, out_dtype=F32, tm=1088, tn=LANES, w_col0=DT_COL0,
                     w_transposed=True, name="in_proj_dt")
    u_tail = _matmul(hn, w["w_in_t"], l, n=TAIL_WIDTH, out_dtype=F32, tm=1088, tn=1024, w_col0=TAIL_COL0,
                     w_transposed=True, name="in_proj_tail")

    ys, ssm_p = _ssd_call(u_head, dt_raw, tables["zeros_hist"], tables["zeros_h"], 0, lp, L=CHUNK, nb=BATCH,
                          nc=SEQ // CHUNK, row0=0, out_layer=l, prev_out=None, prev_state=prev["ssm_p"])
    ys, ssm_s = _ssd_call(u_head, dt_raw, hist8, state_ssm4, l, lp, L=DEC_SEQ, nb=DEC_BATCH, nc=1,
                          row0=M_PROMPT, out_layer=l, prev_out=ys, prev_state=prev["ssm_s"])

    yr, ret_p = _ret_call(u_tail, tables["cos_p"], tables["sin_p"], tables["zeros_s"], 0, L=CHUNK, nb=BATCH,
                          nc=SEQ // CHUNK, row0=0, out_layer=l, prev_out=None, prev_state=prev["ret_p"])
    yr, ret_s = _ret_call(u_tail, tables["cos_s"], tables["sin_s"], state_ret, l, L=DEC_SEQ, nb=DEC_BATCH,
                          nc=1, row0=M_PROMPT, out_layer=l, prev_out=yr, prev_state=prev["ret_s"])

    dm = ATT_HEADS * ATT_HEAD_DIM
    knf_s, knb_s, vb_s = _kv_prep_sample(u_tail, lp["k_norm"])
    hist = cache_k.shape[2]
    assert hist == WINDOW
    tbl = _bias_table(lp["rel_bias"])
    ya, knf_p = _attn_prompt_call(u_tail, tbl[:, :, :WINDOW], tbl[:, :, WINDOW:], lp["q_norm"], lp["k_norm"])
    bias1_s, bias2_s = _head_masked_bias(tbl[:, :DEC_SEQ, :hist], tbl[:, :DEC_SEQ, hist:hist + DEC_SEQ])
    ya = _attn_sample_call(u_tail, cache_k, cache_v, l, knb_s[0], vb_s[0], bias1_s, bias2_s, lp["q_norm"], ya)

    mm = _gated_merge((ys, yr, ya), (w["w_br_ssm"], w["w_br_ret"], w["w_br_att"]), l, u_tail,
                      (T_GA, T_GB, T_GC), tm=544, tn=512)
    x1 = _matmul(mm, w["w_out"], l, n=D_MODEL, out_dtype=F32, tm=M_SAMPLE, tn=1024, addend=x, name="out_proj")

    h2 = _rmsnorm(x1, lp["norm_ffn"])
    hh = _swiglu_matmul(h2, w["w_ffn_in"], l, tm=1088, tn=512)
    x2 = _matmul(hh, w["w_ffn_out"], l, n=D_MODEL, out_dtype=F32, tm=M_SAMPLE, tn=512, addend=x1,
                 split_out=(l == DEPTH - 1), name="ffn_out")

    keep = min(WINDOW, SEQ)
    nconv = CONV_W - 1

    def conv_rows(row0):
        return jnp.concatenate([_row_block(u_head, row0, nconv, H_X, D_SSM),
                                _row_block(u_head, row0, nconv, H_B, BC_WIDTH),
                                _row_block(u_head, row0, nconv, H_C, BC_WIDTH)], axis=-1)

    pv = jnp.stack([_row_block(u_tail, (b + 1) * SEQ - keep, keep, T_AV, dm) for b in range(BATCH)])
    pconv = jnp.stack([conv_rows((b + 1) * SEQ - nconv) for b in range(BATCH)])
    sv = _row_block(u_tail, M_PROMPT, M_SAMPLE, T_AV, dm)
    s_xbc = jnp.concatenate([_row_block(u_head, M_PROMPT, M_SAMPLE, H_X, D_SSM),
                             _row_block(u_head, M_PROMPT, M_SAMPLE, H_B, BC_WIDTH),
                             _row_block(u_head, M_PROMPT, M_SAMPLE, H_C, BC_WIDTH)], axis=-1)
    sconv = s_xbc.reshape(DEC_BATCH, DEC_SEQ, CONV_DIM)[:, DEC_SEQ - nconv:]

    outs_p = (knf_p.reshape(BATCH, keep, ATT_HEADS, ATT_HEAD_DIM),
              pv.reshape(BATCH, keep, ATT_HEADS, ATT_HEAD_DIM),
              pconv)
    outs_s = (knf_s.reshape(DEC_BATCH, DEC_SEQ, ATT_HEADS, ATT_HEAD_DIM),
              sv.reshape(DEC_BATCH, DEC_SEQ, ATT_HEADS, ATT_HEAD_DIM),
              sconv)
    new_states = {"ssm_p": ssm_p, "ssm_s": ssm_s, "ret_p": ret_p, "ret_s": ret_s}
    return x2, outs_p, outs_s, new_states


def kernel(x_prompt, x_sample, cache_attn_k, cache_attn_v, state_ret, state_ssm, state_conv,
           norm_mix, w_in, conv_w, conv_b, dt_bias, a_log, d_skip, ssm_norm, q_norm, k_norm,
           rel_bias, w_br_ssm, w_br_ret, w_br_att, w_out, norm_ffn, w_ffn_in, w_ffn_out):
    x = (x_prompt.reshape(M_PROMPT, D_MODEL), x_sample.reshape(M_SAMPLE, D_MODEL))
    cos_p, sin_p = _rope_tables(jnp.arange(SEQ))
    cos_s, sin_s = _rope_tables(PAST_LEN + jnp.arange(DEC_SEQ))
    tables = {"cos_p": cos_p, "sin_p": sin_p, "cos_s": cos_s, "sin_s": sin_s,
              "zeros_hist": jnp.zeros((1, BATCH, SUBLANES, CONV_DIM), F32),
              "zeros_h": jnp.zeros((1, BATCH, D_SSM, SSM_STATE), F32),
              "zeros_s": jnp.zeros((1, BATCH, RET_HEADS, RET_DK, RET_DV), F32)}
    w = {"w_in_t": jnp.swapaxes(w_in, 1, 2), "w_br_ssm": w_br_ssm, "w_br_ret": w_br_ret, "w_br_att": w_br_att, "w_out": w_out,
         "w_ffn_in": w_ffn_in, "w_ffn_out": w_ffn_out}
    hist8 = jnp.pad(state_conv, ((0, 0), (0, 0), (SUBLANES - (CONV_W - 1), 0), (0, 0)))
    state = (cache_attn_k, cache_attn_v, state_ret,
             state_ssm.reshape(DEPTH, DEC_BATCH, D_SSM, SSM_STATE), hist8)
    acc_p = ([], [], [])
    acc_s = ([], [], [])
    prev = {"ssm_p": None, "ssm_s": None, "ret_p": None, "ret_s": None}
    for l in range(DEPTH):
        lp = _layer_params(l, norm_mix, w_in, conv_w, conv_b, dt_bias, a_log, d_skip, ssm_norm, q_norm,
                           k_norm, rel_bias, w_br_ssm, w_br_ret, w_br_att, w_out, norm_ffn, w_ffn_in,
                           w_ffn_out)
        x, outs_p, outs_s, prev = _layer(x, l, lp, w, state, tables, prev)
        for i in range(3):
            acc_p[i].append(outs_p[i])
            acc_s[i].append(outs_s[i])
    pk, pv, pconv = [jnp.stack(a) for a in acc_p]
    sk, sv, sconv = [jnp.stack(a) for a in acc_s]
    pr, sr = prev["ret_p"], prev["ret_s"]
    pssm = prev["ssm_p"].reshape(DEPTH, BATCH, SSM_HEADS, SSM_HEAD_DIM, SSM_STATE)
    sssm = prev["ssm_s"].reshape(DEPTH, DEC_BATCH, SSM_HEADS, SSM_HEAD_DIM, SSM_STATE)
    yp = x[0].reshape(BATCH, SEQ, D_MODEL)
    ys = x[1].reshape(DEC_BATCH, DEC_SEQ, D_MODEL)
    return (yp, ys, pk, pv, pr, pssm, pconv, sk, sv, sr, sssm, sconv)
```

```python
import functools
import math

import numpy as np
import jax
import jax.numpy as jnp
from jax import lax
from jax.experimental import pallas as pl
from jax.experimental.pallas import tpu as pltpu

F32 = jnp.float32
BF16 = jnp.bfloat16

D_MODEL = 2048
BATCH = 4
SEQ = 2048
DEPTH = 2
DEC_BATCH = 32
DEC_SEQ = 16
PAST_LEN = 1024
CHUNK = 64
EPS = 1e-6
NEG_INF = -1e30

D_SSM = 2048
SSM_HEAD_DIM = 64
SSM_HEADS = 32
SSM_GROUPS = 4
SSM_STATE = 128
SSM_GROUP_WIDTH = D_SSM // SSM_GROUPS
CONV_W = 4
CONV_DIM = D_SSM + 2 * SSM_GROUPS * SSM_STATE
BC_WIDTH = SSM_GROUPS * SSM_STATE

RET_HEADS = 8
RET_DK = 128
RET_DV = 256
ROPE_BASE = 10000.0

ATT_HEADS = 16
ATT_HEAD_DIM = 128
BAND_CHUNKS = 8
WINDOW = BAND_CHUNKS * CHUNK
REL_CLIP = 256

FFN_HIDDEN = 5632

M_PROMPT = BATCH * SEQ
M_SAMPLE = DEC_BATCH * DEC_SEQ
M_ALL = M_PROMPT + M_SAMPLE

LANES = 128
SUBLANES = 8
V7X_VMEM_LIMIT_BYTES = 56 * 1024 * 1024

H_Z, H_X, H_B, H_C = 0, 2048, 4096, 4608
HEAD_WIDTH = 5120
DT_COL0 = 5120
TAIL_COL0 = DT_COL0 + SSM_HEADS
T_RQ, T_RK, T_RV, T_RG, T_AQ, T_AK, T_AV, T_GA, T_GB, T_GC = (
    0, 1024, 2048, 4096, 6144, 8192, 10240, 12288, 14336, 16384)
TAIL_WIDTH = 18432
N_IN = TAIL_COL0 + TAIL_WIDTH


def _params(sem):
    return pltpu.CompilerParams(dimension_semantics=sem, vmem_limit_bytes=V7X_VMEM_LIMIT_BYTES)


def _append_aliased(in_specs, args, out_buffers):
    aliases = {}
    for k, buf in enumerate(out_buffers):
        if buf is not None:
            in_specs.append(pl.BlockSpec(memory_space=pl.ANY))
            args.append(buf)
            aliases[len(args) - 1] = k
    return aliases


def _sigmoid(x):
    return 1.0 / (1.0 + jnp.exp(-x))


def _silu(x):
    return x * _sigmoid(x)


def _rmsnorm_kernel(x_ref, g_ref, o_ref):
    x = x_ref[...]
    ms = jnp.mean(x * x, axis=-1, keepdims=True)
    o_ref[...] = (x * lax.rsqrt(ms + EPS) * g_ref[...]).astype(o_ref.dtype)


def _rmsnorm_split_kernel(xp_ref, xs_ref, g_ref, o_ref):
    last = pl.num_programs(0) - 1

    @pl.when(pl.program_id(0) < last)
    def _():
        _rmsnorm_kernel(xp_ref, g_ref, o_ref)

    @pl.when(pl.program_id(0) == last)
    def _():
        _rmsnorm_kernel(xs_ref, g_ref, o_ref)


def _rmsnorm(x, g, tm=544):
    d = g.shape[-1]
    gspec = pl.BlockSpec((1, d), lambda i: (0, 0))
    if isinstance(x, (tuple, list)):
        tm = M_SAMPLE
        last_prompt_tile = M_PROMPT // tm - 1
        return pl.pallas_call(
            _rmsnorm_split_kernel,
            grid=(M_ALL // tm,),
            in_specs=[pl.BlockSpec((tm, d), lambda i: (jnp.minimum(i, last_prompt_tile), 0)),
                      pl.BlockSpec((tm, d), lambda i: (0, 0)), gspec],
            out_specs=pl.BlockSpec((tm, d), lambda i: (i, 0)),
            out_shape=jax.ShapeDtypeStruct((M_ALL, d), BF16),
            compiler_params=_params(("arbitrary",)),
            name="rmsnorm_split",
        )(x[0], x[1], g.reshape(1, d))
    m = x.shape[0]
    return pl.pallas_call(
        _rmsnorm_kernel,
        grid=(m // tm,),
        in_specs=[pl.BlockSpec((tm, d), lambda i: (i, 0)), gspec],
        out_specs=pl.BlockSpec((tm, d), lambda i: (i, 0)),
        out_shape=jax.ShapeDtypeStruct((m, d), BF16),
        compiler_params=_params(("parallel",)),
        name="rmsnorm",
    )(x, g.reshape(1, d))


W_CAST_ROWS = 256


def _cast_weight_tile(w_ref, wx_ref, wb_ref, shift):
    rows = wb_ref.shape[0]
    for r in range(0, rows - shift, W_CAST_ROWS):
        n = min(W_CAST_ROWS, rows - shift - r)
        wb_ref[r:r + n, :] = w_ref[r + shift:r + shift + n, :].astype(BF16)
    if shift:
        wb_ref[rows - shift:, :] = wx_ref[:shift, :].astype(BF16)


def _mm_kernel(*refs, shift, n_add, n_out, w_transposed):
    a_ref, w_ref = refs[0], refs[1]
    pos = 2
    wx_ref = None
    if shift:
        wx_ref = refs[pos]
        pos += 1
    add_refs = refs[pos:pos + n_add]
    pos += n_add
    o_refs = refs[pos:pos + n_out]
    wb_ref = refs[pos + n_out]
    i = pl.program_id(1)

    @pl.when(i == 0)
    def _():
        _cast_weight_tile(w_ref, wx_ref, wb_ref, shift)

    if w_transposed:
        r = _dot_nt(a_ref[...], wb_ref[...])
    else:
        r = jnp.dot(a_ref[...], wb_ref[...], preferred_element_type=F32)

    def finish(add_ref, o_ref):
        res = r if add_ref is None else add_ref[...] + r
        o_ref[...] = res.astype(o_ref.dtype)

    if n_add < 2 and n_out < 2:
        finish(add_refs[0] if n_add else None, o_refs[0])
    else:
        last = pl.num_programs(1) - 1

        @pl.when(i < last)
        def _():
            finish(add_refs[0] if n_add else None, o_refs[0])

        @pl.when(i == last)
        def _():
            finish(add_refs[-1] if n_add else None, o_refs[-1])


def _matmul(a, w, layer, *, n, out_dtype, tm, tn, w_col0=0, w_transposed=False, addend=None, split_out=False,
            name="matmul"):
    m, kdim = a.shape
    add_parts = isinstance(addend, (tuple, list))
    if add_parts or split_out:
        assert tm == M_SAMPLE and M_PROMPT % tm == 0 and m == M_ALL
    last_prompt_tile = M_PROMPT // tm - 1
    wj0 = w_col0 // tn
    shift = w_col0 - wj0 * tn
    extra = LANES
    assert shift == 0 or (w_transposed and shift % SUBLANES == 0 and shift <= extra and tn % extra == 0)
    if w_transposed:
        w_spec = pl.BlockSpec((None, tn, kdim), lambda j, i: (layer, j + wj0, 0))
        wb_shape = (tn, kdim)
    else:
        w_spec = pl.BlockSpec((None, kdim, tn), lambda j, i: (layer, 0, j + wj0))
        wb_shape = (kdim, tn)
    in_specs = [pl.BlockSpec((tm, kdim), lambda j, i: (i, 0)), w_spec]
    args = [a, w]
    if shift:
        in_specs.append(pl.BlockSpec((None, extra, kdim), lambda j, i: (layer, (j + wj0 + 1) * (tn // extra), 0)))
        args.append(w)
    prompt_rows = pl.BlockSpec((tm, tn), lambda j, i: (jnp.minimum(i, last_prompt_tile), j))
    sample_rows = pl.BlockSpec((tm, tn), lambda j, i: (0, j))
    all_rows = pl.BlockSpec((tm, tn), lambda j, i: (i, j))
    if add_parts:
        in_specs += [prompt_rows, sample_rows]
        args += list(addend)
    elif addend is not None:
        in_specs.append(all_rows)
        args.append(addend)
    n_add = 2 if add_parts else int(addend is not None)
    if split_out:
        out_specs = [prompt_rows, sample_rows]
        out_shape = [jax.ShapeDtypeStruct((M_PROMPT, n), out_dtype), jax.ShapeDtypeStruct((M_SAMPLE, n), out_dtype)]
    else:
        out_specs = all_rows
        out_shape = jax.ShapeDtypeStruct((m, n), out_dtype)
    return pl.pallas_call(
        functools.partial(_mm_kernel, shift=shift, n_add=n_add, n_out=2 if split_out else 1,
                          w_transposed=w_transposed),
        grid=(n // tn, m // tm),
        in_specs=in_specs,
        out_specs=out_specs,
        out_shape=out_shape,
        scratch_shapes=[pltpu.VMEM(wb_shape, BF16)],
        compiler_params=_params(("parallel", "arbitrary")),
        name=name,
    )(*args)


def _out_proj_norm_kernel(*refs, n_add, n_prompt_tiles):
    a_ref, w_ref = refs[0], refs[1]
    add_refs = refs[2:2 + n_add]
    g_ref, x_ref, hn_ref, wb_ref = refs[2 + n_add:]
    i = pl.program_id(0)

    @pl.when(i == 0)
    def _():
        _cast_weight_tile(w_ref, None, wb_ref, 0)

    r = jnp.dot(a_ref[...], wb_ref[...], preferred_element_type=F32)

    def finish(add_ref):
        x_ref[...] = add_ref[...] + r
        _rmsnorm_kernel(x_ref, g_ref, hn_ref)

    if n_add == 1:
        finish(add_refs[0])
    else:
        @pl.when(i < n_prompt_tiles)
        def _():
            finish(add_refs[0])

        @pl.when(i >= n_prompt_tiles)
        def _():
            finish(add_refs[1])


def _out_proj_norm(a, w, layer, addend, g, *, tm):
    m, kdim = a.shape
    d = w.shape[-1]
    add_parts = isinstance(addend, (tuple, list))
    assert M_PROMPT % tm == 0 and M_SAMPLE % tm == 0
    n_prompt_tiles = M_PROMPT // tm
    row_tile = lambda index: pl.BlockSpec((tm, d), index)
    in_specs = [pl.BlockSpec((tm, kdim), lambda i: (i, 0)),
                pl.BlockSpec((None, kdim, d), lambda i: (layer, 0, 0), pipeline_mode=pl.Buffered(1))]
    args = [a, w]
    if add_parts:
        in_specs += [row_tile(lambda i: (jnp.minimum(i, n_prompt_tiles - 1), 0)),
                     row_tile(lambda i: (jnp.maximum(i - n_prompt_tiles, 0), 0))]
        args += list(addend)
    else:
        in_specs.append(row_tile(lambda i: (i, 0)))
        args.append(addend)
    in_specs.append(pl.BlockSpec((1, d), lambda i: (0, 0)))
    args.append(g.reshape(1, d))
    return pl.pallas_call(
        functools.partial(_out_proj_norm_kernel, n_add=2 if add_parts else 1, n_prompt_tiles=n_prompt_tiles),
        grid=(m // tm,),
        in_specs=in_specs,
        out_specs=[row_tile(lambda i: (i, 0)), row_tile(lambda i: (i, 0))],
        out_shape=[jax.ShapeDtypeStruct((m, d), F32), jax.ShapeDtypeStruct((m, d), BF16)],
        scratch_shapes=[pltpu.VMEM((kdim, d), BF16)],
        compiler_params=_params(("arbitrary",)),
        name="out_proj_norm",
    )(*args)


def _merge_kernel(a0_ref, a1_ref, a2_ref, w0_ref, w1_ref, w2_ref, g0_ref, g1_ref, g2_ref, o_ref,
                  wb0_ref, wb1_ref, wb2_ref):
    @pl.when(pl.program_id(1) == 0)
    def _():
        for w_ref, wb_ref in ((w0_ref, wb0_ref), (w1_ref, wb1_ref), (w2_ref, wb2_ref)):
            _cast_weight_tile(w_ref, None, wb_ref, 0)

    acc = None
    for a_ref, wb_ref, g_ref in ((a0_ref, wb0_ref, g0_ref), (a1_ref, wb1_ref, g1_ref), (a2_ref, wb2_ref, g2_ref)):
        term = _sigmoid(g_ref[...]) * jnp.dot(a_ref[...], wb_ref[...], preferred_element_type=F32)
        acc = term if acc is None else acc + term
    o_ref[...] = acc.astype(o_ref.dtype)


def _gated_merge(branches, weights, layer, gates, gate_cols, *, tm, tn):
    m, kdim = branches[0].shape
    a_spec = pl.BlockSpec((tm, kdim), lambda j, i: (i, 0))
    w_spec = pl.BlockSpec((None, kdim, tn), lambda j, i: (layer, 0, j), pipeline_mode=pl.Buffered(1))
    g_specs = [pl.BlockSpec((tm, tn), functools.partial(lambda j, i, j0: (i, j + j0), j0=col // tn))
               for col in gate_cols]
    return pl.pallas_call(
        _merge_kernel,
        grid=(D_MODEL // tn, m // tm),
        in_specs=[a_spec] * 3 + [w_spec] * 3 + g_specs,
        out_specs=pl.BlockSpec((tm, tn), lambda j, i: (i, j)),
        out_shape=jax.ShapeDtypeStruct((m, D_MODEL), BF16),
        scratch_shapes=[pltpu.VMEM((kdim, tn), BF16)] * 3,
        compiler_params=_params(("parallel", "arbitrary")),
        name="gated_merge",
    )(*branches, *weights, gates, gates, gates)


def _swiglu_kernel(a_ref, wa_ref, wc_ref, o_ref, wab_ref, wcb_ref):
    @pl.when(pl.program_id(1) == 0)
    def _():
        _cast_weight_tile(wa_ref, None, wab_ref, 0)
        _cast_weight_tile(wc_ref, None, wcb_ref, 0)

    a = a_ref[...]
    fa = jnp.dot(a, wab_ref[...], preferred_element_type=F32)
    fc = jnp.dot(a, wcb_ref[...], preferred_element_type=F32)
    o_ref[...] = (_silu(fa) * fc).astype(o_ref.dtype)


def _swiglu_matmul(a, w, layer, *, tm, tn):
    m, kdim = a.shape
    nj = FFN_HIDDEN // tn
    return pl.pallas_call(
        _swiglu_kernel,
        grid=(nj, m // tm),
        in_specs=[pl.BlockSpec((tm, kdim), lambda j, i: (i, 0)),
                  pl.BlockSpec((None, kdim, tn), lambda j, i: (layer, 0, j)),
                  pl.BlockSpec((None, kdim, tn), lambda j, i: (layer, 0, j + nj))],
        out_specs=pl.BlockSpec((tm, tn), lambda j, i: (i, j)),
        out_shape=jax.ShapeDtypeStruct((m, FFN_HIDDEN), BF16),
        scratch_shapes=[pltpu.VMEM((kdim, tn), BF16), pltpu.VMEM((kdim, tn), BF16)],
        compiler_params=_params(("parallel", "arbitrary")),
        name="ffn_in_swiglu",
    )(a, w, w)


def _split3(a):
    hi = a.astype(BF16)
    r1 = a - hi.astype(F32)
    mid = r1.astype(BF16)
    lo = (r1 - mid.astype(F32)).astype(BF16)
    return hi, mid, lo


def _exact_right(a, onehot):
    return sum(jnp.dot(p, onehot, preferred_element_type=F32) for p in _split3(a))


def _exact_right_many(arrays, onehot):
    n = arrays[0].shape[0]
    r = jnp.dot(jnp.concatenate([p for a in arrays for p in _split3(a)], axis=0), onehot,
                preferred_element_type=F32)
    return [r[3 * k * n:(3 * k + 1) * n] + r[(3 * k + 1) * n:(3 * k + 2) * n] + r[(3 * k + 2) * n:(3 * k + 3) * n]
            for k in range(len(arrays))]


def _exact_left(onehot, b):
    return sum(jnp.dot(onehot, p, preferred_element_type=F32) for p in _split3(b))


def _dot_nt(a, b):
    return lax.dot_general(a, b, (((1,), (1,)), ((), ())), preferred_element_type=F32)


def _dot_tn(a, b):
    return lax.dot_general(a, b, (((0,), (0,)), ((), ())), preferred_element_type=F32)


def _conv_silu(ext_ref, cw, cb, L):
    acc = ext_ref[SUBLANES:SUBLANES + L, :] * cw[CONV_W - 1:CONV_W, :] + cb
    for s in range(1, CONV_W):
        acc = acc + ext_ref[SUBLANES - s:SUBLANES - s + L, :] * cw[CONV_W - 1 - s:CONV_W - s, :]
    return _silu(acc)


def _ssd_kernel(x_ref, b_ref, c_ref, z_ref, dt_ref, hist_ref, h0_ref,
                cwx_ref, cwb_ref, cwc_ref, cbx_ref, cbb_ref, cbc_ref,
                dtb_ref, alog_ref, dskip_ref, normw_ref, exp_p_ref, exp_l_ref,
                *rest, L, nc, n_aliased):
    y_ref, hout_ref, ex_ref, eb_ref, ec_ref, ht_ref = rest[n_aliased:]
    c = pl.program_id(1)
    lw = SSM_HEADS * L

    @pl.when(c == 0)
    def _():
        hist = hist_ref[0]
        ex_ref[0:SUBLANES, :] = hist[:, :D_SSM]
        eb_ref[0:SUBLANES, :] = hist[:, D_SSM:D_SSM + BC_WIDTH]
        ec_ref[0:SUBLANES, :] = hist[:, D_SSM + BC_WIDTH:]
        ht_ref[...] = h0_ref[0].T

    @pl.when(c > 0)
    def _():
        ex_ref[0:SUBLANES, :] = ex_ref[L:L + SUBLANES, :]
        eb_ref[0:SUBLANES, :] = eb_ref[L:L + SUBLANES, :]
        ec_ref[0:SUBLANES, :] = ec_ref[L:L + SUBLANES, :]

    ex_ref[SUBLANES:SUBLANES + L, :] = x_ref[...]
    eb_ref[SUBLANES:SUBLANES + L, :] = b_ref[...]
    ec_ref[SUBLANES:SUBLANES + L, :] = c_ref[...]
    xs = _conv_silu(ex_ref, cwx_ref[...], cbx_ref[...], L)
    bm = _conv_silu(eb_ref, cwb_ref[...], cbb_ref[...], L).astype(BF16)
    cm = _conv_silu(ec_ref, cwc_ref[...], cbc_ref[...], L).astype(BF16)

    dtr = dt_ref[...] + dtb_ref[...]
    dt = jnp.maximum(dtr, 0.0) + jnp.log1p(jnp.exp(-jnp.abs(dtr)))

    tri = (lax.broadcasted_iota(jnp.int32, (L, L), 1)
           <= lax.broadcasted_iota(jnp.int32, (L, L), 0)).astype(BF16)
    acum = _exact_left(tri, dt * (-jnp.exp(alog_ref[...])))

    dt_p, acum_p = _exact_right_many([dt, acum], exp_p_ref[...])
    last_p = acum_p[L - 1:L, :]
    if L == SSM_HEAD_DIM:
        dt_l, acum_l = dt_p, acum_p
    else:
        dt_l, acum_l = _exact_right_many([dt, acum], exp_l_ref[...])

    rowi = lax.broadcasted_iota(jnp.int32, (L, lw), 0)
    colj = lax.broadcasted_iota(jnp.int32, (L, lw), 1) % L
    diag = colj == rowi
    causal = colj <= rowi
    acum_row = jnp.sum(jnp.where(diag, acum_l, 0.0), axis=0, keepdims=True)
    dt_row = jnp.sum(jnp.where(diag, dt_l, 0.0), axis=0, keepdims=True)
    decay = jnp.where(causal, jnp.exp(jnp.where(causal, acum_l - acum_row, 0.0)), 0.0)

    heads_per_group = SSM_HEADS // SSM_GROUPS
    cb = jnp.concatenate(
        [_dot_nt(cm[:, g * SSM_STATE:(g + 1) * SSM_STATE],
                 jnp.concatenate([bm[:, g * SSM_STATE:(g + 1) * SSM_STATE]] * heads_per_group, axis=0))
         for g in range(SSM_GROUPS)], axis=1)

    ht = ht_ref[...]
    ht_b = ht.astype(BF16)
    to_end = jnp.exp(last_p - acum_p) * dt_p
    xw = (xs * to_end).astype(BF16)
    y_inter = []
    st_parts = []
    for g in range(SSM_GROUPS):
        gs = slice(g * SSM_GROUP_WIDTH, (g + 1) * SSM_GROUP_WIDTH)
        ns = slice(g * SSM_STATE, (g + 1) * SSM_STATE)
        y_inter.append(jnp.dot(cm[:, ns], ht_b[:, gs], preferred_element_type=F32))
        st_parts.append(_dot_tn(bm[:, ns], xw[:, gs]))
    ht_new = jnp.exp(last_p) * ht + jnp.concatenate(st_parts, axis=1)
    ht_ref[...] = ht_new

    m_all = (cb * decay * dt_row).astype(BF16)
    hp = LANES // L
    slab = hp * SSM_HEAD_DIM
    xs_b = xs.astype(BF16)
    lane_head = lax.broadcasted_iota(jnp.int32, (L, slab), 1) // SSM_HEAD_DIM
    y_parts = []
    for q in range(lw // LANES):
        xq = xs_b[:, q * slab:(q + 1) * slab]
        bd = jnp.concatenate([jnp.where(lane_head == h, xq, jnp.zeros_like(xq)) for h in range(hp)], axis=0)
        y_parts.append(jnp.dot(m_all[:, q * LANES:(q + 1) * LANES], bd, preferred_element_type=F32))
    y = (jnp.concatenate(y_parts, axis=1) + jnp.concatenate(y_inter, axis=1) * jnp.exp(acum_p)
         + dskip_ref[...] * xs)

    t = y * _silu(z_ref[...])
    outs = []
    for g in range(SSM_GROUPS):
        tg = t[:, g * SSM_GROUP_WIDTH:(g + 1) * SSM_GROUP_WIDTH]
        outs.append(tg * lax.rsqrt(jnp.mean(tg * tg, axis=-1, keepdims=True) + EPS))
    y_ref[...] = (jnp.concatenate(outs, axis=1) * normw_ref[...]).astype(y_ref.dtype)

    @pl.when(c == nc - 1)
    def _():
        hout_ref[0] = ht_new.T


def _ssd_call(u, dt_raw, hist8, h0, layer, lp, *, L, nb, nc, row0, out_layer, prev_out, prev_state):
    rb0 = row0 // L

    def rows(width, colblock):
        return pl.BlockSpec((L, width), lambda b, c: (rb0 + b * nc + c, colblock))

    def const(shape):
        return pl.BlockSpec(shape, lambda b, c: tuple(0 for _ in shape))

    lw = SSM_HEADS * L
    in_specs = [rows(D_SSM, H_X // D_SSM), rows(BC_WIDTH, H_B // BC_WIDTH), rows(BC_WIDTH, H_C // BC_WIDTH),
                rows(D_SSM, H_Z // D_SSM),
                pl.BlockSpec((L, LANES), lambda b, c: (rb0 + b * nc + c, 0)),
                pl.BlockSpec((None, 1, SUBLANES, CONV_DIM), lambda b, c: (layer, b, 0, 0)),
                pl.BlockSpec((None, 1, D_SSM, SSM_STATE), lambda b, c: (layer, b, 0, 0)),
                const((CONV_W, D_SSM)), const((CONV_W, BC_WIDTH)), const((CONV_W, BC_WIDTH)),
                const((1, D_SSM)), const((1, BC_WIDTH)), const((1, BC_WIDTH)),
                const((1, LANES)), const((1, LANES)), const((1, D_SSM)), const((1, D_SSM)),
                const((LANES, D_SSM)), const((LANES, lw))]
    args = [u, u, u, u, dt_raw, hist8, h0,
            lp["conv_wx"], lp["conv_wb"], lp["conv_wc"], lp["conv_bx"], lp["conv_bb"], lp["conv_bc"],
            lp["dt_bias"], lp["a_log"], lp["d_skip_p"], lp["ssm_norm"],
            _expand_onehot(SSM_HEAD_DIM), _expand_onehot(L)]
    aliases = _append_aliased(in_specs, args, (prev_out, prev_state))
    y, hout = pl.pallas_call(
        functools.partial(_ssd_kernel, L=L, nc=nc, n_aliased=len(aliases)),
        grid=(nb, nc),
        in_specs=in_specs,
        out_specs=[pl.BlockSpec((L, D_SSM), lambda b, c: (rb0 + b * nc + c, 0)),
                   pl.BlockSpec((None, 1, D_SSM, SSM_STATE), lambda b, c: (out_layer, b, 0, 0))],
        out_shape=[jax.ShapeDtypeStruct((M_ALL, D_SSM), BF16),
                   jax.ShapeDtypeStruct((DEPTH, nb, D_SSM, SSM_STATE), F32)],
        scratch_shapes=[pltpu.VMEM((SUBLANES + L, D_SSM), F32), pltpu.VMEM((SUBLANES + L, BC_WIDTH), F32),
                        pltpu.VMEM((SUBLANES + L, BC_WIDTH), F32), pltpu.VMEM((SSM_STATE, D_SSM), F32)],
        input_output_aliases=aliases,
        compiler_params=_params(("parallel", "arbitrary")),
        name="ssd_L%d" % L,
    )(*args)
    return y, hout


def _expand_onehot(per_head):
    e = np.zeros((LANES, SSM_HEADS * per_head), np.float32)
    for h in range(SSM_HEADS):
        e[h, h * per_head:(h + 1) * per_head] = 1.0
    return jnp.asarray(e, BF16)


def _ret_consts(L):
    lam = np.log1p(-np.exp2(-5.0 - np.arange(RET_HEADS, dtype=np.float64)))
    i = np.arange(L, dtype=np.float64)
    diff = i[:, None] - i[None, :]
    dmat = np.where(diff >= 0, np.exp(lam[:, None, None] * np.maximum(diff, 0.0)), 0.0)
    to_end = np.exp(lam[:, None] * (L - 1 - i)[None, :])
    q_dec = np.exp(lam[:, None] * (i + 1.0)[None, :])
    chunk_decay = np.exp(lam * L)
    expand = lambda t: np.repeat(t.T[:, :, None], RET_DK, axis=2).reshape(L, RET_HEADS * RET_DK)
    return (jnp.asarray(dmat, F32), jnp.asarray(expand(q_dec), F32), jnp.asarray(expand(to_end), F32),
            [float(v) for v in chunk_decay])


RET_HEADS_AHEAD = 8
RET_CHUNKS_PER_STEP = 2


def _ret_kernel(q_ref, k_ref, v_ref, g_ref, cos_ref, sin_ref, qdec_ref, kend_ref, dmat_ref, s0_ref,
                *rest, L, cps, chunk_decay, n_aliased):
    y_ref, sout_ref, s_ref = rest[n_aliased:]
    step = pl.program_id(1)

    @pl.when(step == 0)
    def _():
        s_ref[...] = s0_ref[0]

    half = RET_DK // 2

    def one_chunk(sub, carry):
        rows = slice(0, L) if cps == 1 else pl.ds(pl.multiple_of(sub * L, L), L)
        cos = cos_ref[rows, :]
        sin = sin_ref[rows, :]

        def first_stage(h):
            ks = slice(h * RET_DK, (h + 1) * RET_DK)
            qh = q_ref[rows, ks]
            kh = k_ref[rows, ks]
            qr = qh * cos + pltpu.roll(qh, half, 1) * sin
            kr = (kh * cos + pltpu.roll(kh, half, 1) * sin) * (RET_DK ** -0.5)
            vh = v_ref[rows, h * RET_DV:(h + 1) * RET_DV].astype(BF16)
            s_prev = s_ref[h]
            qk = _dot_nt(qr.astype(BF16), kr.astype(BF16))
            o_state = jnp.dot((qr * qdec_ref[:, ks]).astype(BF16), s_prev.astype(BF16),
                              preferred_element_type=F32)
            kv = _dot_tn((kr * kend_ref[:, ks]).astype(BF16), vh)
            s_ref[h] = chunk_decay[h] * s_prev + kv
            return qk, o_state, vh

        def second_stage(h, qk, o_state, vh):
            vs = slice(h * RET_DV, (h + 1) * RET_DV)
            s = qk * dmat_ref[h]
            o = jnp.dot(s.astype(BF16), vh, preferred_element_type=F32) + o_state
            o = o * lax.rsqrt(jnp.mean(o * o, axis=-1, keepdims=True) + EPS)
            y_ref[rows, vs] = (o * _silu(g_ref[rows, vs])).astype(y_ref.dtype)

        pending = [first_stage(h) for h in range(RET_HEADS_AHEAD)]
        for h in range(RET_HEADS):
            if h + RET_HEADS_AHEAD < RET_HEADS:
                pending.append(first_stage(h + RET_HEADS_AHEAD))
            second_stage(h, *pending.pop(0))
        return carry

    if cps == 1:
        one_chunk(0, 0)
    else:
        lax.fori_loop(0, cps, one_chunk, 0)

    @pl.when(step == pl.num_programs(1) - 1)
    def _():
        sout_ref[0] = s_ref[...]


def _ret_call(u, cos_t, sin_t, s0, layer, *, L, nb, nc, row0, out_layer, prev_out, prev_state):
    cps = RET_CHUNKS_PER_STEP if nc % RET_CHUNKS_PER_STEP == 0 else 1
    blk = cps * L
    nsteps = nc // cps
    rb0 = row0 // blk
    dmat, qdec, kend, chunk_decay = _ret_consts(L)
    hk = RET_HEADS * RET_DK
    hv = RET_HEADS * RET_DV

    def rows(width, colblock):
        return pl.BlockSpec((blk, width), lambda b, c: (rb0 + b * nsteps + c, colblock))

    in_specs = [rows(hk, T_RQ // hk), rows(hk, T_RK // hk), rows(hv, T_RV // hv), rows(hv, T_RG // hv),
                pl.BlockSpec((blk, RET_DK), lambda b, c: (c, 0)),
                pl.BlockSpec((blk, RET_DK), lambda b, c: (c, 0)),
                pl.BlockSpec((L, hk), lambda b, c: (0, 0)),
                pl.BlockSpec((L, hk), lambda b, c: (0, 0)),
                pl.BlockSpec((RET_HEADS, L, L), lambda b, c: (0, 0, 0)),
                pl.BlockSpec((None, 1, RET_HEADS, RET_DK, RET_DV), lambda b, c: (layer, b, 0, 0, 0))]
    args = [u, u, u, u, cos_t, sin_t, qdec, kend, dmat, s0]
    aliases = _append_aliased(in_specs, args, (prev_out, prev_state))
    y, sout = pl.pallas_call(
        functools.partial(_ret_kernel, L=L, cps=cps, chunk_decay=chunk_decay, n_aliased=len(aliases)),
        grid=(nb, nsteps),
        in_specs=in_specs,
        out_specs=[pl.BlockSpec((blk, hv), lambda b, c: (rb0 + b * nsteps + c, 0)),
                   pl.BlockSpec((None, 1, RET_HEADS, RET_DK, RET_DV), lambda b, c: (out_layer, b, 0, 0, 0))],
        out_shape=[jax.ShapeDtypeStruct((M_ALL, hv), BF16),
                   jax.ShapeDtypeStruct((DEPTH, nb, RET_HEADS, RET_DK, RET_DV), F32)],
        scratch_shapes=[pltpu.VMEM((RET_HEADS, RET_DK, RET_DV), F32)],
        input_output_aliases=aliases,
        compiler_params=_params(("parallel", "arbitrary")),
        name="retention_L%d" % L,
    )(*args)
    return y, sout


def _rope_tables(pos):
    half = RET_DK // 2
    inv = ROPE_BASE ** (-jnp.arange(half, dtype=F32) * 2.0 / RET_DK)
    ang = pos.astype(F32)[:, None] * inv[None, :]
    cos = jnp.cos(ang)
    sin = jnp.sin(ang)
    return jnp.concatenate([cos, cos], axis=1), jnp.concatenate([-sin, sin], axis=1)


def _head_norm(xh, g):
    return xh * lax.rsqrt(jnp.mean(xh * xh, axis=-1, keepdims=True) + EPS) * g


def _kv_prep_kernel(k_ref, v_ref, gk_ref, knf_ref, knb_ref, vb_ref):
    for h in range(ATT_HEADS):
        hs = slice(h * ATT_HEAD_DIM, (h + 1) * ATT_HEAD_DIM)
        kn = _head_norm(k_ref[:, hs], gk_ref[...])
        knf_ref[0, :, hs] = kn
        knb_ref[0, :, hs] = kn.astype(BF16)
    vb_ref[0] = v_ref[...].astype(BF16)


def _kv_prep_sample(u, gk):
    dm = ATT_HEADS * ATT_HEAD_DIM
    rb = M_PROMPT // M_SAMPLE
    return pl.pallas_call(
        _kv_prep_kernel,
        grid=(1,),
        in_specs=[pl.BlockSpec((M_SAMPLE, dm), lambda i: (rb, T_AK // dm)),
                  pl.BlockSpec((M_SAMPLE, dm), lambda i: (rb, T_AV // dm)),
                  pl.BlockSpec((1, ATT_HEAD_DIM), lambda i: (0, 0))],
        out_specs=[pl.BlockSpec((1, M_SAMPLE, dm), lambda i: (0, 0, 0)),
                   pl.BlockSpec((1, M_SAMPLE, dm), lambda i: (0, 0, 0)),
                   pl.BlockSpec((1, M_SAMPLE, dm), lambda i: (0, 0, 0))],
        out_shape=[jax.ShapeDtypeStruct((1, M_SAMPLE, dm), F32),
                   jax.ShapeDtypeStruct((1, M_SAMPLE, dm), BF16),
                   jax.ShapeDtypeStruct((1, M_SAMPLE, dm), BF16)],
        compiler_params=_params(("arbitrary",)),
        name="kv_prep_sample",
    )(u, u, gk)


def _scores(qn, k1, k2, bias1, bias2, valid1):
    scale = ATT_HEAD_DIM ** -0.5
    s1 = _dot_nt(qn, k1) * scale + bias1
    if valid1 is not None:
        s1 = jnp.where(valid1, s1, NEG_INF)
    s2 = _dot_nt(qn, k2) * scale + bias2
    return s1, s2


def _softmax_pv(s1, s2, v1, v2):
    m = jnp.maximum(jnp.max(s1, axis=-1, keepdims=True), jnp.max(s2, axis=-1, keepdims=True))
    p1 = jnp.exp(s1 - m)
    p2 = jnp.exp(s2 - m)
    inv = 1.0 / (jnp.sum(p1, axis=-1, keepdims=True) + jnp.sum(p2, axis=-1, keepdims=True))
    return (jnp.dot((p1 * inv).astype(BF16), v1, preferred_element_type=F32)
            + jnp.dot((p2 * inv).astype(BF16), v2, preferred_element_type=F32))


def _attend(qn, k1, v1, k2, v2, bias1, bias2, valid1):
    s1, s2 = _scores(qn, k1, k2, bias1, bias2, valid1)
    return _softmax_pv(s1, s2, v1, v2)


ATTN_HEADS_AHEAD = 16


ATTN_CHUNKS_PER_STEP = 2


def _attn_prompt_kernel(q_ref, k_ref, v_ref, b1_ref, b2_ref, gq_ref, gk_ref, y_ref, knf_ref, kb_ref, vb_ref):
    step = pl.program_id(1)
    heads = [slice(h * ATT_HEAD_DIM, (h + 1) * ATT_HEAD_DIM) for h in range(ATT_HEADS)]

    @pl.when(step == 0)
    def _():
        kb_ref[0:WINDOW, :] = jnp.zeros((WINDOW, kb_ref.shape[1]), BF16)
        vb_ref[0:WINDOW, :] = jnp.zeros((WINDOW, vb_ref.shape[1]), BF16)

    def one_chunk(sub, carry):
        c = step * ATTN_CHUNKS_PER_STEP + sub
        rows = pl.ds(pl.multiple_of(sub * CHUNK, CHUNK), CHUNK)
        base = pl.multiple_of(c * CHUNK, CHUNK)
        own = pl.multiple_of(c * CHUNK + WINDOW, CHUNK)

        for hs in heads:
            kn = _head_norm(k_ref[rows, hs], gk_ref[...])
            knf_ref[0, rows, hs] = kn
            kb_ref[pl.ds(own, CHUNK), hs] = kn.astype(BF16)
        vb_ref[pl.ds(own, CHUNK), :] = v_ref[rows, :].astype(BF16)

        col = lax.broadcasted_iota(jnp.int32, (CHUNK, WINDOW), 1)
        valid1 = col + c * CHUNK >= WINDOW

        def scores(h):
            hs = heads[h]
            qn = _head_norm(q_ref[rows, hs], gq_ref[...]).astype(BF16)
            return _scores(qn, kb_ref[pl.ds(base, WINDOW), hs], kb_ref[pl.ds(own, CHUNK), hs],
                           b1_ref[h], b2_ref[h], valid1)

        pending = [scores(h) for h in range(ATTN_HEADS_AHEAD)]
        for h in range(ATT_HEADS):
            if h + ATTN_HEADS_AHEAD < ATT_HEADS:
                pending.append(scores(h + ATTN_HEADS_AHEAD))
            s1, s2 = pending.pop(0)
            hs = heads[h]
            o = _softmax_pv(s1, s2, vb_ref[pl.ds(base, WINDOW), hs], vb_ref[pl.ds(own, CHUNK), hs])
            y_ref[rows, hs] = o.astype(y_ref.dtype)
        return carry

    lax.fori_loop(0, ATTN_CHUNKS_PER_STEP, one_chunk, 0)


def _attn_prompt_call(u, bias1, bias2, gq, gk):
    dm = ATT_HEADS * ATT_HEAD_DIM
    blk = ATTN_CHUNKS_PER_STEP * CHUNK
    nsteps = SEQ // blk
    keep = min(WINDOW, SEQ)
    assert SEQ % blk == 0 and keep % blk == 0
    first_kept = nsteps - keep // blk

    def rows(col0):
        return pl.BlockSpec((blk, dm), lambda b, c: (b * nsteps + c, col0 // dm))

    return pl.pallas_call(
        _attn_prompt_kernel,
        grid=(BATCH, nsteps),
        in_specs=[rows(T_AQ), rows(T_AK), rows(T_AV),
                  pl.BlockSpec((ATT_HEADS, CHUNK, WINDOW), lambda b, c: (0, 0, 0)),
                  pl.BlockSpec((ATT_HEADS, CHUNK, CHUNK), lambda b, c: (0, 0, 0)),
                  pl.BlockSpec((1, ATT_HEAD_DIM), lambda b, c: (0, 0)),
                  pl.BlockSpec((1, ATT_HEAD_DIM), lambda b, c: (0, 0))],
        out_specs=[pl.BlockSpec((blk, dm), lambda b, c: (b * nsteps + c, 0)),
                   pl.BlockSpec((1, blk, dm), lambda b, c: (b, jnp.maximum(c - first_kept, 0), 0))],
        out_shape=[jax.ShapeDtypeStruct((M_ALL, dm), BF16),
                   jax.ShapeDtypeStruct((BATCH, keep, dm), F32)],
        scratch_shapes=[pltpu.VMEM((SEQ + WINDOW, dm), BF16), pltpu.VMEM((SEQ + WINDOW, dm), BF16)],
        compiler_params=_params(("parallel", "arbitrary")),
        name="attn_prompt",
    )(u, u, u, bias1, bias2, gq, gk)


def _attn_sample_kernel(q_ref, kh_ref, vh_ref, kn_ref, vn_ref, b1_ref, b2_ref, gq_ref, prev_ref, y_ref):
    del prev_ref
    heads = [slice(h * ATT_HEAD_DIM, (h + 1) * ATT_HEAD_DIM) for h in range(ATT_HEADS)]
    hist = kh_ref.shape[1]
    qall = jnp.concatenate([_head_norm(q_ref[:, hs], gq_ref[...]).astype(BF16) for hs in heads], axis=0)
    k1 = kh_ref[0].reshape(hist * ATT_HEADS, ATT_HEAD_DIM).astype(BF16)
    v1 = vh_ref[0].reshape(hist * ATT_HEADS, ATT_HEAD_DIM).astype(BF16)
    k2 = jnp.concatenate([kn_ref[:, hs] for hs in heads], axis=0)
    v2 = jnp.concatenate([vn_ref[:, hs] for hs in heads], axis=0)
    o = _attend(qall, k1, v1, k2, v2, b1_ref[...], b2_ref[...], None)
    for h, hs in enumerate(heads):
        y_ref[:, hs] = o[h * DEC_SEQ:(h + 1) * DEC_SEQ].astype(y_ref.dtype)


def _bias_expand_kernel(t1_ref, t2_ref, o1_ref, o2_ref):
    rows, cb = o1_ref.shape
    row_head = lax.broadcasted_iota(jnp.int32, (rows, cb), 0) // DEC_SEQ
    col = lax.broadcasted_iota(jnp.int32, (rows, cb), 1)
    ej = lax.broadcasted_iota(jnp.int32, (LANES, cb), 0)
    ec = lax.broadcasted_iota(jnp.int32, (LANES, cb), 1)
    e1 = jnp.where(ec // ATT_HEADS == ej, 1.0, 0.0).astype(BF16)
    o1_ref[...] = jnp.where(col % ATT_HEADS == row_head, _exact_right(t1_ref[...], e1), NEG_INF)
    n2 = o2_ref.shape[1]
    row_head2 = lax.broadcasted_iota(jnp.int32, (rows, n2), 0) // DEC_SEQ
    col2 = lax.broadcasted_iota(jnp.int32, (rows, n2), 1)
    ej2 = lax.broadcasted_iota(jnp.int32, (LANES, n2), 0)
    ec2 = lax.broadcasted_iota(jnp.int32, (LANES, n2), 1)
    e2 = jnp.where(ec2 % DEC_SEQ == ej2, 1.0, 0.0).astype(BF16)
    o2_ref[...] = jnp.where(col2 // DEC_SEQ == row_head2, _exact_right(t2_ref[...], e2), NEG_INF)


def _head_masked_bias(tbl1, tbl2):
    nh, nq, hist = tbl1.shape
    rows = nh * nq
    t1 = tbl1.reshape(rows, hist)
    t2 = jnp.pad(tbl2.reshape(rows, nq), ((0, 0), (0, LANES - nq)))
    cb = LANES * nh
    return pl.pallas_call(
        _bias_expand_kernel,
        grid=(hist // LANES,),
        in_specs=[pl.BlockSpec((rows, LANES), lambda j: (0, j)),
                  pl.BlockSpec((rows, LANES), lambda j: (0, 0))],
        out_specs=[pl.BlockSpec((rows, cb), lambda j: (0, j)),
                   pl.BlockSpec((rows, nh * nq), lambda j: (0, 0))],
        out_shape=[jax.ShapeDtypeStruct((rows, hist * nh), F32),
                   jax.ShapeDtypeStruct((rows, nh * nq), F32)],
        compiler_params=_params(("arbitrary",)),
        name="bias_expand",
    )(t1, t2)


def _attn_sample_call(u, k_hist, v_hist, layer, kn_new, vn_new, bias1, bias2, gq, prev_out):
    dm = ATT_HEADS * ATT_HEAD_DIM
    rb0 = M_PROMPT // DEC_SEQ
    hist = k_hist.shape[2]
    cache_spec = pl.BlockSpec((None, 1, hist, ATT_HEADS, ATT_HEAD_DIM), lambda b: (layer, b, 0, 0, 0))
    return pl.pallas_call(
        _attn_sample_kernel,
        grid=(DEC_BATCH,),
        in_specs=[pl.BlockSpec((DEC_SEQ, dm), lambda b: (rb0 + b, T_AQ // dm)),
                  cache_spec,
                  cache_spec,
                  pl.BlockSpec((DEC_SEQ, dm), lambda b: (b, 0)),
                  pl.BlockSpec((DEC_SEQ, dm), lambda b: (b, 0)),
                  pl.BlockSpec((ATT_HEADS * DEC_SEQ, hist * ATT_HEADS), lambda b: (0, 0),
                               pipeline_mode=pl.Buffered(1)),
                  pl.BlockSpec((ATT_HEADS * DEC_SEQ, ATT_HEADS * DEC_SEQ), lambda b: (0, 0)),
                  pl.BlockSpec((1, ATT_HEAD_DIM), lambda b: (0, 0)),
                  pl.BlockSpec(memory_space=pl.ANY)],
        out_specs=pl.BlockSpec((DEC_SEQ, dm), lambda b: (rb0 + b, 0)),
        out_shape=jax.ShapeDtypeStruct((M_ALL, dm), BF16),
        input_output_aliases={8: 0},
        compiler_params=_params(("arbitrary",)),
        name="attn_sample",
    )(u, k_hist, v_hist, kn_new, vn_new, bias1, bias2, gq, prev_out)


def _bias_table(rel_bias):
    kbn = WINDOW + CHUNK
    period = kbn + CHUNK
    m = np.arange(period)
    d = np.where(m < kbn, m, m - period)
    idx = np.clip(WINDOW - d, -REL_CLIP, REL_CLIP) + REL_CLIP
    vec = rel_bias[:, idx]
    rows = jnp.tile(vec, (1, CHUNK))[:, :CHUNK * (period - 1)].reshape(ATT_HEADS, CHUNK, period - 1)
    return rows[:, :, :kbn]


def _layer_params(l, norm_mix, w_in, conv_w, conv_b, dt_bias, a_log, d_skip, ssm_norm, q_norm, k_norm,
                  rel_bias, w_br_ssm, w_br_ret, w_br_att, w_out, norm_ffn, w_ffn_in, w_ffn_out):
    assert w_in.shape[-1] == N_IN
    cw = conv_w[l]
    cb = conv_b[l].reshape(1, CONV_DIM)
    lp = {
        "norm_mix": norm_mix[l], "norm_ffn": norm_ffn[l],
        "conv_wx": cw[:, :D_SSM], "conv_wb": cw[:, D_SSM:D_SSM + BC_WIDTH], "conv_wc": cw[:, D_SSM + BC_WIDTH:],
        "conv_bx": cb[:, :D_SSM], "conv_bb": cb[:, D_SSM:D_SSM + BC_WIDTH], "conv_bc": cb[:, D_SSM + BC_WIDTH:],
        "dt_bias": jnp.pad(dt_bias[l], (0, LANES - SSM_HEADS)).reshape(1, LANES),
        "a_log": jnp.pad(a_log[l], (0, LANES - SSM_HEADS)).reshape(1, LANES),
        "d_skip_p": jnp.repeat(d_skip[l], SSM_HEAD_DIM).reshape(1, D_SSM),
        "ssm_norm": ssm_norm[l].reshape(1, D_SSM),
        "q_norm": q_norm[l].reshape(1, ATT_HEAD_DIM), "k_norm": k_norm[l].reshape(1, ATT_HEAD_DIM),
        "rel_bias": rel_bias[l],
    }
    return lp


def _row_block(t, row0, rows, col0, width):
    return lax.slice(t, (row0, col0), (row0 + rows, col0 + width))


def _layer(x, l, lp, w, state, tables, prev):
    cache_k, cache_v, state_ret, state_ssm4, hist8 = state
    hn = _rmsnorm(x, lp["norm_mix"])
    u_head = _matmul(hn, w["w_in_t"], l, n=HEAD_WIDTH, out_dtype=F32, tm=1088, tn=1024, w_transposed=True,
                     name="in_proj_head")
    dt_raw = _matmul(hn, w["w_in_t"], l, n=LANES, out_dtype=F32, tm=1088, tn=LANES, w_col0=DT_COL0,
                     w_transposed=True, name="in_proj_dt")
    u_tail = _matmul(hn, w["w_in_t"], l, n=TAIL_WIDTH, out_dtype=F32, tm=1088, tn=1024, w_col0=TAIL_COL0,
                     w_transposed=True, name="in_proj_tail")

    ys, ssm_p = _ssd_call(u_head, dt_raw, tables["zeros_hist"], tables["zeros_h"], 0, lp, L=CHUNK, nb=BATCH,
                          nc=SEQ // CHUNK, row0=0, out_layer=l, prev_out=None, prev_state=prev["ssm_p"])
    ys, ssm_s = _ssd_call(u_head, dt_raw, hist8, state_ssm4, l, lp, L=DEC_SEQ, nb=DEC_BATCH, nc=1,
                          row0=M_PROMPT, out_layer=l, prev_out=ys, prev_state=prev["ssm_s"])

    yr, ret_p = _ret_call(u_tail, tables["cos_p"], tables["sin_p"], tables["zeros_s"], 0, L=CHUNK, nb=BATCH,
                          nc=SEQ // CHUNK, row0=0, out_layer=l, prev_out=None, prev_state=prev["ret_p"])
    yr, ret_s = _ret_call(u_tail, tables["cos_s"], tables["sin_s"], state_ret, l, L=DEC_SEQ, nb=DEC_BATCH,
                          nc=1, row0=M_PROMPT, out_layer=l, prev_out=yr, prev_state=prev["ret_s"])

    dm = ATT_HEADS * ATT_HEAD_DIM
    knf_s, knb_s, vb_s = _kv_prep_sample(u_tail, lp["k_norm"])
    hist = cache_k.shape[2]
    assert hist == WINDOW
    tbl = _bias_table(lp["rel_bias"])
    ya, knf_p = _attn_prompt_call(u_tail, tbl[:, :, :WINDOW], tbl[:, :, WINDOW:], lp["q_norm"], lp["k_norm"])
    bias1_s, bias2_s = _head_masked_bias(tbl[:, :DEC_SEQ, :hist], tbl[:, :DEC_SEQ, hist:hist + DEC_SEQ])
    ya = _attn_sample_call(u_tail, cache_k, cache_v, l, knb_s[0], vb_s[0], bias1_s, bias2_s, lp["q_norm"], ya)

    mm = _gated_merge((ys, yr, ya), (w["w_br_ssm"], w["w_br_ret"], w["w_br_att"]), l, u_tail,
                      (T_GA, T_GB, T_GC), tm=544, tn=512)
    x1, h2 = _out_proj_norm(mm, w["w_out"], l, x, lp["norm_ffn"], tm=256)

    hh = _swiglu_matmul(h2, w["w_ffn_in"], l, tm=1088, tn=512)
    x2 = _matmul(hh, w["w_ffn_out"], l, n=D_MODEL, out_dtype=F32, tm=M_SAMPLE, tn=512, addend=x1,
                 split_out=(l == DEPTH - 1), name="ffn_out")

    keep = min(WINDOW, SEQ)
    nconv = CONV_W - 1

    def conv_rows(row0):
        return jnp.concatenate([_row_block(u_head, row0, nconv, H_X, D_SSM),
                                _row_block(u_head, row0, nconv, H_B, BC_WIDTH),
                                _row_block(u_head, row0, nconv, H_C, BC_WIDTH)], axis=-1)

    pv = jnp.stack([_row_block(u_tail, (b + 1) * SEQ - keep, keep, T_AV, dm) for b in range(BATCH)])
    pconv = jnp.stack([conv_rows((b + 1) * SEQ - nconv) for b in range(BATCH)])
    sv = _row_block(u_tail, M_PROMPT, M_SAMPLE, T_AV, dm)
    s_xbc = jnp.concatenate([_row_block(u_head, M_PROMPT, M_SAMPLE, H_X, D_SSM),
                             _row_block(u_head, M_PROMPT, M_SAMPLE, H_B, BC_WIDTH),
                             _row_block(u_head, M_PROMPT, M_SAMPLE, H_C, BC_WIDTH)], axis=-1)
    sconv = s_xbc.reshape(DEC_BATCH, DEC_SEQ, CONV_DIM)[:, DEC_SEQ - nconv:]

    outs_p = (knf_p.reshape(BATCH, keep, ATT_HEADS, ATT_HEAD_DIM),
              pv.reshape(BATCH, keep, ATT_HEADS, ATT_HEAD_DIM),
              pconv)
    outs_s = (knf_s.reshape(DEC_BATCH, DEC_SEQ, ATT_HEADS, ATT_HEAD_DIM),
              sv.reshape(DEC_BATCH, DEC_SEQ, ATT_HEADS, ATT_HEAD_DIM),
              sconv)
    new_states = {"ssm_p": ssm_p, "ssm_s": ssm_s, "ret_p": ret_p, "ret_s": ret_s}
    return x2, outs_p, outs_s, new_states


def kernel(x_prompt, x_sample, cache_attn_k, cache_attn_v, state_ret, state_ssm, state_conv,
           norm_mix, w_in, conv_w, conv_b, dt_bias, a_log, d_skip, ssm_norm, q_norm, k_norm,
           rel_bias, w_br_ssm, w_br_ret, w_br_att, w_out, norm_ffn, w_ffn_in, w_ffn_out):
    x = (x_prompt.reshape(M_PROMPT, D_MODEL), x_sample.reshape(M_SAMPLE, D_MODEL))
    cos_p, sin_p = _rope_tables(jnp.arange(SEQ))
    cos_s, sin_s = _rope_tables(PAST_LEN + jnp.arange(DEC_SEQ))
    tables = {"cos_p": cos_p, "sin_p": sin_p, "cos_s": cos_s, "sin_s": sin_s,
              "zeros_hist": jnp.zeros((1, BATCH, SUBLANES, CONV_DIM), F32),
              "zeros_h": jnp.zeros((1, BATCH, D_SSM, SSM_STATE), F32),
              "zeros_s": jnp.zeros((1, BATCH, RET_HEADS, RET_DK, RET_DV), F32)}
    w = {"w_in_t": jnp.swapaxes(w_in, 1, 2), "w_br_ssm": w_br_ssm, "w_br_ret": w_br_ret, "w_br_att": w_br_att, "w_out": w_out,
         "w_ffn_in": w_ffn_in, "w_ffn_out": w_ffn_out}
    hist8 = jnp.pad(state_conv, ((0, 0), (0, 0), (SUBLANES - (CONV_W - 1), 0), (0, 0)))
    state = (cache_attn_k, cache_attn_v, state_ret,
             state_ssm.reshape(DEPTH, DEC_BATCH, D_SSM, SSM_STATE), hist8)
    acc_p = ([], [], [])
    acc_s = ([], [], [])
    prev = {"ssm_p": None, "ssm_s": None, "ret_p": None, "ret_s": None}
    for l in range(DEPTH):
        lp = _layer_params(l, norm_mix, w_in, conv_w, conv_b, dt_bias, a_log, d_skip, ssm_norm, q_norm,
                           k_norm, rel_bias, w_br_ssm, w_br_ret, w_br_att, w_out, norm_ffn, w_ffn_in,
                           w_ffn_out)
        x, outs_p, outs_s, prev = _layer(x, l, lp, w, state, tables, prev)
        for i in range(3):
            acc_p[i].append(outs_p[i])
            acc_s[i].append(outs_s[i])
    pk, pv, pconv = [jnp.stack(a) for a in acc_p]
    sk, sv, sconv = [jnp.stack(a) for a in acc_s]
    pr, sr = prev["ret_p"], prev["ret_s"]
    pssm = prev["ssm_p"].reshape(DEPTH, BATCH, SSM_HEADS, SSM_HEAD_DIM, SSM_STATE)
    sssm = prev["ssm_s"].reshape(DEPTH, DEC_BATCH, SSM_HEADS, SSM_HEAD_DIM, SSM_STATE)
    yp = x[0].reshape(BATCH, SEQ, D_MODEL)
    ys = x[1].reshape(DEC_BATCH, DEC_SEQ, D_MODEL)
    return (yp, ys, pk, pv, pr, pssm, pconv, sk, sv, sr, sssm, sconv)
```

```python
import functools
import math

import numpy as np
import jax
import jax.numpy as jnp
from jax import lax
from jax.experimental import pallas as pl
from jax.experimental.pallas import tpu as pltpu

F32 = jnp.float32
BF16 = jnp.bfloat16

D_MODEL = 2048
BATCH = 4
SEQ = 2048
DEPTH = 2
DEC_BATCH = 32
DEC_SEQ = 16
PAST_LEN = 1024
CHUNK = 64
EPS = 1e-6
NEG_INF = -1e30

D_SSM = 2048
SSM_HEAD_DIM = 64
SSM_HEADS = 32
SSM_GROUPS = 4
SSM_STATE = 128
SSM_GROUP_WIDTH = D_SSM // SSM_GROUPS
CONV_W = 4
CONV_DIM = D_SSM + 2 * SSM_GROUPS * SSM_STATE
BC_WIDTH = SSM_GROUPS * SSM_STATE

RET_HEADS = 8
RET_DK = 128
RET_DV = 256
ROPE_BASE = 10000.0

ATT_HEADS = 16
ATT_HEAD_DIM = 128
BAND_CHUNKS = 8
WINDOW = BAND_CHUNKS * CHUNK
REL_CLIP = 256

FFN_HIDDEN = 5632

M_PROMPT = BATCH * SEQ
M_SAMPLE = DEC_BATCH * DEC_SEQ
M_ALL = M_PROMPT + M_SAMPLE

LANES = 128
SUBLANES = 8
V7X_VMEM_LIMIT_BYTES = 56 * 1024 * 1024

H_Z, H_X, H_B, H_C = 0, 2048, 4096, 4608
HEAD_WIDTH = 5120
DT_COL0 = 5120
TAIL_COL0 = DT_COL0 + SSM_HEADS
T_RQ, T_RK, T_RV, T_RG, T_AQ, T_AK, T_AV, T_GA, T_GB, T_GC = (
    0, 1024, 2048, 4096, 6144, 8192, 10240, 12288, 14336, 16384)
TAIL_WIDTH = 18432
N_IN = TAIL_COL0 + TAIL_WIDTH


DENSE_TILES = {
    "in_proj": (1088, 1024),
    "merge": (544, 512),
    "out_proj": (256, D_MODEL),
    "ffn_in": (1088, 512),
    "ffn_out": (M_SAMPLE, 512),
}


def _params(sem):
    return pltpu.CompilerParams(dimension_semantics=sem, vmem_limit_bytes=V7X_VMEM_LIMIT_BYTES)


def _append_aliased(in_specs, args, out_buffers):
    aliases = {}
    for k, buf in enumerate(out_buffers):
        if buf is not None:
            in_specs.append(pl.BlockSpec(memory_space=pl.ANY))
            args.append(buf)
            aliases[len(args) - 1] = k
    return aliases


def _sigmoid(x):
    return 1.0 / (1.0 + jnp.exp(-x))


def _silu(x):
    return x * _sigmoid(x)


def _rmsnorm_kernel(x_ref, g_ref, o_ref):
    x = x_ref[...]
    ms = jnp.mean(x * x, axis=-1, keepdims=True)
    o_ref[...] = (x * lax.rsqrt(ms + EPS) * g_ref[...]).astype(o_ref.dtype)


def _rmsnorm_split_kernel(xp_ref, xs_ref, g_ref, o_ref):
    last = pl.num_programs(0) - 1

    @pl.when(pl.program_id(0) < last)
    def _():
        _rmsnorm_kernel(xp_ref, g_ref, o_ref)

    @pl.when(pl.program_id(0) == last)
    def _():
        _rmsnorm_kernel(xs_ref, g_ref, o_ref)


def _rmsnorm(x, g, tm=544):
    d = g.shape[-1]
    gspec = pl.BlockSpec((1, d), lambda i: (0, 0))
    if isinstance(x, (tuple, list)):
        tm = M_SAMPLE
        last_prompt_tile = M_PROMPT // tm - 1
        return pl.pallas_call(
            _rmsnorm_split_kernel,
            grid=(M_ALL // tm,),
            in_specs=[pl.BlockSpec((tm, d), lambda i: (jnp.minimum(i, last_prompt_tile), 0)),
                      pl.BlockSpec((tm, d), lambda i: (0, 0)), gspec],
            out_specs=pl.BlockSpec((tm, d), lambda i: (i, 0)),
            out_shape=jax.ShapeDtypeStruct((M_ALL, d), BF16),
            compiler_params=_params(("arbitrary",)),
            name="rmsnorm_split",
        )(x[0], x[1], g.reshape(1, d))
    m = x.shape[0]
    return pl.pallas_call(
        _rmsnorm_kernel,
        grid=(m // tm,),
        in_specs=[pl.BlockSpec((tm, d), lambda i: (i, 0)), gspec],
        out_specs=pl.BlockSpec((tm, d), lambda i: (i, 0)),
        out_shape=jax.ShapeDtypeStruct((m, d), BF16),
        compiler_params=_params(("parallel",)),
        name="rmsnorm",
    )(x, g.reshape(1, d))


W_CAST_ROWS = 256


def _cast_weight_tile(w_ref, wx_ref, wb_ref, shift):
    rows = wb_ref.shape[0]
    for r in range(0, rows - shift, W_CAST_ROWS):
        n = min(W_CAST_ROWS, rows - shift - r)
        wb_ref[r:r + n, :] = w_ref[r + shift:r + shift + n, :].astype(BF16)
    if shift:
        wb_ref[rows - shift:, :] = wx_ref[:shift, :].astype(BF16)


def _mm_kernel(*refs, shift, n_add, n_out, w_transposed):
    a_ref, w_ref = refs[0], refs[1]
    pos = 2
    wx_ref = None
    if shift:
        wx_ref = refs[pos]
        pos += 1
    add_refs = refs[pos:pos + n_add]
    pos += n_add
    o_refs = refs[pos:pos + n_out]
    wb_ref = refs[pos + n_out]
    i = pl.program_id(1)

    @pl.when(i == 0)
    def _():
        _cast_weight_tile(w_ref, wx_ref, wb_ref, shift)

    if w_transposed:
        r = _dot_nt(a_ref[...], wb_ref[...])
    else:
        r = jnp.dot(a_ref[...], wb_ref[...], preferred_element_type=F32)

    def finish(add_ref, o_ref):
        res = r if add_ref is None else add_ref[...] + r
        o_ref[...] = res.astype(o_ref.dtype)

    if n_add < 2 and n_out < 2:
        finish(add_refs[0] if n_add else None, o_refs[0])
    else:
        last = pl.num_programs(1) - 1

        @pl.when(i < last)
        def _():
            finish(add_refs[0] if n_add else None, o_refs[0])

        @pl.when(i == last)
        def _():
            finish(add_refs[-1] if n_add else None, o_refs[-1])


def _matmul(a, w, layer, *, n, out_dtype, tm, tn, w_col0=0, w_transposed=False, addend=None, split_out=False,
            name="matmul"):
    m, kdim = a.shape
    add_parts = isinstance(addend, (tuple, list))
    if add_parts or split_out:
        assert tm == M_SAMPLE and M_PROMPT % tm == 0 and m == M_ALL
    last_prompt_tile = M_PROMPT // tm - 1
    wj0 = w_col0 // tn
    shift = w_col0 - wj0 * tn
    extra = LANES
    assert shift == 0 or (w_transposed and shift % SUBLANES == 0 and shift <= extra and tn % extra == 0)
    if w_transposed:
        w_spec = pl.BlockSpec((None, tn, kdim), lambda j, i: (layer, j + wj0, 0))
        wb_shape = (tn, kdim)
    else:
        w_spec = pl.BlockSpec((None, kdim, tn), lambda j, i: (layer, 0, j + wj0))
        wb_shape = (kdim, tn)
    in_specs = [pl.BlockSpec((tm, kdim), lambda j, i: (i, 0)), w_spec]
    args = [a, w]
    if shift:
        in_specs.append(pl.BlockSpec((None, extra, kdim), lambda j, i: (layer, (j + wj0 + 1) * (tn // extra), 0)))
        args.append(w)
    prompt_rows = pl.BlockSpec((tm, tn), lambda j, i: (jnp.minimum(i, last_prompt_tile), j))
    sample_rows = pl.BlockSpec((tm, tn), lambda j, i: (0, j))
    all_rows = pl.BlockSpec((tm, tn), lambda j, i: (i, j))
    if add_parts:
        in_specs += [prompt_rows, sample_rows]
        args += list(addend)
    elif addend is not None:
        in_specs.append(all_rows)
        args.append(addend)
    n_add = 2 if add_parts else int(addend is not None)
    if split_out:
        out_specs = [prompt_rows, sample_rows]
        out_shape = [jax.ShapeDtypeStruct((M_PROMPT, n), out_dtype), jax.ShapeDtypeStruct((M_SAMPLE, n), out_dtype)]
    else:
        out_specs = all_rows
        out_shape = jax.ShapeDtypeStruct((m, n), out_dtype)
    return pl.pallas_call(
        functools.partial(_mm_kernel, shift=shift, n_add=n_add, n_out=2 if split_out else 1,
                          w_transposed=w_transposed),
        grid=(n // tn, m // tm),
        in_specs=in_specs,
        out_specs=out_specs,
        out_shape=out_shape,
        scratch_shapes=[pltpu.VMEM(wb_shape, BF16)],
        compiler_params=_params(("parallel", "arbitrary")),
        name=name,
    )(*args)


def _out_proj_norm_kernel(*refs, n_add, n_prompt_tiles):
    a_ref, w_ref = refs[0], refs[1]
    add_refs = refs[2:2 + n_add]
    g_ref, x_ref, hn_ref, wb_ref = refs[2 + n_add:]
    i = pl.program_id(0)

    @pl.when(i == 0)
    def _():
        _cast_weight_tile(w_ref, None, wb_ref, 0)

    r = jnp.dot(a_ref[...], wb_ref[...], preferred_element_type=F32)

    def finish(add_ref):
        x_ref[...] = add_ref[...] + r
        _rmsnorm_kernel(x_ref, g_ref, hn_ref)

    if n_add == 1:
        finish(add_refs[0])
    else:
        @pl.when(i < n_prompt_tiles)
        def _():
            finish(add_refs[0])

        @pl.when(i >= n_prompt_tiles)
        def _():
            finish(add_refs[1])


def _out_proj_norm(a, w, layer, addend, g, *, tm):
    m, kdim = a.shape
    d = w.shape[-1]
    add_parts = isinstance(addend, (tuple, list))
    assert M_PROMPT % tm == 0 and M_SAMPLE % tm == 0
    n_prompt_tiles = M_PROMPT // tm
    row_tile = lambda index: pl.BlockSpec((tm, d), index)
    in_specs = [pl.BlockSpec((tm, kdim), lambda i: (i, 0)),
                pl.BlockSpec((None, kdim, d), lambda i: (layer, 0, 0), pipeline_mode=pl.Buffered(1))]
    args = [a, w]
    if add_parts:
        in_specs += [row_tile(lambda i: (jnp.minimum(i, n_prompt_tiles - 1), 0)),
                     row_tile(lambda i: (jnp.maximum(i - n_prompt_tiles, 0), 0))]
        args += list(addend)
    else:
        in_specs.append(row_tile(lambda i: (i, 0)))
        args.append(addend)
    in_specs.append(pl.BlockSpec((1, d), lambda i: (0, 0)))
    args.append(g.reshape(1, d))
    return pl.pallas_call(
        functools.partial(_out_proj_norm_kernel, n_add=2 if add_parts else 1, n_prompt_tiles=n_prompt_tiles),
        grid=(m // tm,),
        in_specs=in_specs,
        out_specs=[row_tile(lambda i: (i, 0)), row_tile(lambda i: (i, 0))],
        out_shape=[jax.ShapeDtypeStruct((m, d), F32), jax.ShapeDtypeStruct((m, d), BF16)],
        scratch_shapes=[pltpu.VMEM((kdim, d), BF16)],
        compiler_params=_params(("arbitrary",)),
        name="out_proj_norm",
    )(*args)


def _merge_kernel(a0_ref, a1_ref, a2_ref, w0_ref, w1_ref, w2_ref, g0_ref, g1_ref, g2_ref, o_ref,
                  wb0_ref, wb1_ref, wb2_ref):
    @pl.when(pl.program_id(1) == 0)
    def _():
        for w_ref, wb_ref in ((w0_ref, wb0_ref), (w1_ref, wb1_ref), (w2_ref, wb2_ref)):
            _cast_weight_tile(w_ref, None, wb_ref, 0)

    acc = None
    for a_ref, wb_ref, g_ref in ((a0_ref, wb0_ref, g0_ref), (a1_ref, wb1_ref, g1_ref), (a2_ref, wb2_ref, g2_ref)):
        term = _sigmoid(g_ref[...]) * jnp.dot(a_ref[...], wb_ref[...], preferred_element_type=F32)
        acc = term if acc is None else acc + term
    o_ref[...] = acc.astype(o_ref.dtype)


def _gated_merge(branches, weights, layer, gates, gate_cols, *, tm, tn):
    m, kdim = branches[0].shape
    a_spec = pl.BlockSpec((tm, kdim), lambda j, i: (i, 0))
    w_spec = pl.BlockSpec((None, kdim, tn), lambda j, i: (layer, 0, j), pipeline_mode=pl.Buffered(1))
    g_specs = [pl.BlockSpec((tm, tn), functools.partial(lambda j, i, j0: (i, j + j0), j0=col // tn))
               for col in gate_cols]
    return pl.pallas_call(
        _merge_kernel,
        grid=(D_MODEL // tn, m // tm),
        in_specs=[a_spec] * 3 + [w_spec] * 3 + g_specs,
        out_specs=pl.BlockSpec((tm, tn), lambda j, i: (i, j)),
        out_shape=jax.ShapeDtypeStruct((m, D_MODEL), BF16),
        scratch_shapes=[pltpu.VMEM((kdim, tn), BF16)] * 3,
        compiler_params=_params(("parallel", "arbitrary")),
        name="gated_merge",
    )(*branches, *weights, gates, gates, gates)


def _swiglu_kernel(a_ref, wa_ref, wc_ref, o_ref, wab_ref, wcb_ref):
    @pl.when(pl.program_id(1) == 0)
    def _():
        _cast_weight_tile(wa_ref, None, wab_ref, 0)
        _cast_weight_tile(wc_ref, None, wcb_ref, 0)

    a = a_ref[...]
    fa = jnp.dot(a, wab_ref[...], preferred_element_type=F32)
    fc = jnp.dot(a, wcb_ref[...], preferred_element_type=F32)
    o_ref[...] = (_silu(fa) * fc).astype(o_ref.dtype)


def _swiglu_matmul(a, w, layer, *, tm, tn):
    m, kdim = a.shape
    nj = FFN_HIDDEN // tn
    return pl.pallas_call(
        _swiglu_kernel,
        grid=(nj, m // tm),
        in_specs=[pl.BlockSpec((tm, kdim), lambda j, i: (i, 0)),
                  pl.BlockSpec((None, kdim, tn), lambda j, i: (layer, 0, j)),
                  pl.BlockSpec((None, kdim, tn), lambda j, i: (layer, 0, j + nj))],
        out_specs=pl.BlockSpec((tm, tn), lambda j, i: (i, j)),
        out_shape=jax.ShapeDtypeStruct((m, FFN_HIDDEN), BF16),
        scratch_shapes=[pltpu.VMEM((kdim, tn), BF16), pltpu.VMEM((kdim, tn), BF16)],
        compiler_params=_params(("parallel", "arbitrary")),
        name="ffn_in_swiglu",
    )(a, w, w)


def _split3(a):
    hi = a.astype(BF16)
    r1 = a - hi.astype(F32)
    mid = r1.astype(BF16)
    lo = (r1 - mid.astype(F32)).astype(BF16)
    return hi, mid, lo


def _exact_right(a, onehot):
    return sum(jnp.dot(p, onehot, preferred_element_type=F32) for p in _split3(a))


def _exact_right_many(arrays, onehot):
    n = arrays[0].shape[0]
    r = jnp.dot(jnp.concatenate([p for a in arrays for p in _split3(a)], axis=0), onehot,
                preferred_element_type=F32)
    return [r[3 * k * n:(3 * k + 1) * n] + r[(3 * k + 1) * n:(3 * k + 2) * n] + r[(3 * k + 2) * n:(3 * k + 3) * n]
            for k in range(len(arrays))]


def _exact_left(onehot, b):
    return sum(jnp.dot(onehot, p, preferred_element_type=F32) for p in _split3(b))


def _dot_nt(a, b):
    return lax.dot_general(a, b, (((1,), (1,)), ((), ())), preferred_element_type=F32)


def _dot_tn(a, b):
    return lax.dot_general(a, b, (((0,), (0,)), ((), ())), preferred_element_type=F32)


def _conv_silu(ext_ref, cw, cb, L):
    acc = ext_ref[SUBLANES:SUBLANES + L, :] * cw[CONV_W - 1:CONV_W, :] + cb
    for s in range(1, CONV_W):
        acc = acc + ext_ref[SUBLANES - s:SUBLANES - s + L, :] * cw[CONV_W - 1 - s:CONV_W - s, :]
    return _silu(acc)


def _ssd_kernel(x_ref, b_ref, c_ref, z_ref, dt_ref, hist_ref, h0_ref,
                cwx_ref, cwb_ref, cwc_ref, cbx_ref, cbb_ref, cbc_ref,
                dtb_ref, alog_ref, dskip_ref, normw_ref, exp_p_ref, exp_l_ref,
                *rest, L, nc, n_aliased):
    y_ref, hout_ref, ex_ref, eb_ref, ec_ref, ht_ref = rest[n_aliased:]
    c = pl.program_id(1)
    lw = SSM_HEADS * L

    @pl.when(c == 0)
    def _():
        hist = hist_ref[0]
        ex_ref[0:SUBLANES, :] = hist[:, :D_SSM]
        eb_ref[0:SUBLANES, :] = hist[:, D_SSM:D_SSM + BC_WIDTH]
        ec_ref[0:SUBLANES, :] = hist[:, D_SSM + BC_WIDTH:]
        ht_ref[...] = h0_ref[0].T

    @pl.when(c > 0)
    def _():
        ex_ref[0:SUBLANES, :] = ex_ref[L:L + SUBLANES, :]
        eb_ref[0:SUBLANES, :] = eb_ref[L:L + SUBLANES, :]
        ec_ref[0:SUBLANES, :] = ec_ref[L:L + SUBLANES, :]

    ex_ref[SUBLANES:SUBLANES + L, :] = x_ref[...]
    eb_ref[SUBLANES:SUBLANES + L, :] = b_ref[...]
    ec_ref[SUBLANES:SUBLANES + L, :] = c_ref[...]
    xs = _conv_silu(ex_ref, cwx_ref[...], cbx_ref[...], L)
    bm = _conv_silu(eb_ref, cwb_ref[...], cbb_ref[...], L).astype(BF16)
    cm = _conv_silu(ec_ref, cwc_ref[...], cbc_ref[...], L).astype(BF16)

    dtr = dt_ref[...] + dtb_ref[...]
    dt = jnp.maximum(dtr, 0.0) + jnp.log1p(jnp.exp(-jnp.abs(dtr)))

    tri = (lax.broadcasted_iota(jnp.int32, (L, L), 1)
           <= lax.broadcasted_iota(jnp.int32, (L, L), 0)).astype(BF16)
    acum = _exact_left(tri, dt * (-jnp.exp(alog_ref[...])))

    dt_p, acum_p = _exact_right_many([dt, acum], exp_p_ref[...])
    last_p = acum_p[L - 1:L, :]
    if L == SSM_HEAD_DIM:
        dt_l, acum_l = dt_p, acum_p
    else:
        dt_l, acum_l = _exact_right_many([dt, acum], exp_l_ref[...])

    rowi = lax.broadcasted_iota(jnp.int32, (L, lw), 0)
    colj = lax.broadcasted_iota(jnp.int32, (L, lw), 1) % L
    diag = colj == rowi
    causal = colj <= rowi
    acum_row = jnp.sum(jnp.where(diag, acum_l, 0.0), axis=0, keepdims=True)
    dt_row = jnp.sum(jnp.where(diag, dt_l, 0.0), axis=0, keepdims=True)
    decay = jnp.where(causal, jnp.exp(jnp.where(causal, acum_l - acum_row, 0.0)), 0.0)

    heads_per_group = SSM_HEADS // SSM_GROUPS
    cb = jnp.concatenate(
        [_dot_nt(cm[:, g * SSM_STATE:(g + 1) * SSM_STATE],
                 jnp.concatenate([bm[:, g * SSM_STATE:(g + 1) * SSM_STATE]] * heads_per_group, axis=0))
         for g in range(SSM_GROUPS)], axis=1)

    ht = ht_ref[...]
    ht_b = ht.astype(BF16)
    to_end = jnp.exp(last_p - acum_p) * dt_p
    xw = (xs * to_end).astype(BF16)
    y_inter = []
    st_parts = []
    for g in range(SSM_GROUPS):
        gs = slice(g * SSM_GROUP_WIDTH, (g + 1) * SSM_GROUP_WIDTH)
        ns = slice(g * SSM_STATE, (g + 1) * SSM_STATE)
        y_inter.append(jnp.dot(cm[:, ns], ht_b[:, gs], preferred_element_type=F32))
        st_parts.append(_dot_tn(bm[:, ns], xw[:, gs]))
    ht_new = jnp.exp(last_p) * ht + jnp.concatenate(st_parts, axis=1)
    ht_ref[...] = ht_new

    m_all = (cb * decay * dt_row).astype(BF16)
    hp = LANES // L
    slab = hp * SSM_HEAD_DIM
    xs_b = xs.astype(BF16)
    lane_head = lax.broadcasted_iota(jnp.int32, (L, slab), 1) // SSM_HEAD_DIM
    y_parts = []
    for q in range(lw // LANES):
        xq = xs_b[:, q * slab:(q + 1) * slab]
        bd = jnp.concatenate([jnp.where(lane_head == h, xq, jnp.zeros_like(xq)) for h in range(hp)], axis=0)
        y_parts.append(jnp.dot(m_all[:, q * LANES:(q + 1) * LANES], bd, preferred_element_type=F32))
    y = (jnp.concatenate(y_parts, axis=1) + jnp.concatenate(y_inter, axis=1) * jnp.exp(acum_p)
         + dskip_ref[...] * xs)

    t = y * _silu(z_ref[...])
    outs = []
    for g in range(SSM_GROUPS):
        tg = t[:, g * SSM_GROUP_WIDTH:(g + 1) * SSM_GROUP_WIDTH]
        outs.append(tg * lax.rsqrt(jnp.mean(tg * tg, axis=-1, keepdims=True) + EPS))
    y_ref[...] = (jnp.concatenate(outs, axis=1) * normw_ref[...]).astype(y_ref.dtype)

    @pl.when(c == nc - 1)
    def _():
        hout_ref[0] = ht_new.T


def _ssd_call(u, dt_raw, hist8, h0, layer, lp, *, L, nb, nc, row0, out_layer, prev_out, prev_state):
    rb0 = row0 // L

    def rows(width, colblock):
        return pl.BlockSpec((L, width), lambda b, c: (rb0 + b * nc + c, colblock))

    def const(shape):
        return pl.BlockSpec(shape, lambda b, c: tuple(0 for _ in shape))

    lw = SSM_HEADS * L
    in_specs = [rows(D_SSM, H_X // D_SSM), rows(BC_WIDTH, H_B // BC_WIDTH), rows(BC_WIDTH, H_C // BC_WIDTH),
                rows(D_SSM, H_Z // D_SSM),
                pl.BlockSpec((L, LANES), lambda b, c: (rb0 + b * nc + c, 0)),
                pl.BlockSpec((None, 1, SUBLANES, CONV_DIM), lambda b, c: (layer, b, 0, 0)),
                pl.BlockSpec((None, 1, D_SSM, SSM_STATE), lambda b, c: (layer, b, 0, 0)),
                const((CONV_W, D_SSM)), const((CONV_W, BC_WIDTH)), const((CONV_W, BC_WIDTH)),
                const((1, D_SSM)), const((1, BC_WIDTH)), const((1, BC_WIDTH)),
                const((1, LANES)), const((1, LANES)), const((1, D_SSM)), const((1, D_SSM)),
                const((LANES, D_SSM)), const((LANES, lw))]
    args = [u, u, u, u, dt_raw, hist8, h0,
            lp["conv_wx"], lp["conv_wb"], lp["conv_wc"], lp["conv_bx"], lp["conv_bb"], lp["conv_bc"],
            lp["dt_bias"], lp["a_log"], lp["d_skip_p"], lp["ssm_norm"],
            _expand_onehot(SSM_HEAD_DIM), _expand_onehot(L)]
    aliases = _append_aliased(in_specs, args, (prev_out, prev_state))
    y, hout = pl.pallas_call(
        functools.partial(_ssd_kernel, L=L, nc=nc, n_aliased=len(aliases)),
        grid=(nb, nc),
        in_specs=in_specs,
        out_specs=[pl.BlockSpec((L, D_SSM), lambda b, c: (rb0 + b * nc + c, 0)),
                   pl.BlockSpec((None, 1, D_SSM, SSM_STATE), lambda b, c: (out_layer, b, 0, 0))],
        out_shape=[jax.ShapeDtypeStruct((M_ALL, D_SSM), BF16),
                   jax.ShapeDtypeStruct((DEPTH, nb, D_SSM, SSM_STATE), F32)],
        scratch_shapes=[pltpu.VMEM((SUBLANES + L, D_SSM), F32), pltpu.VMEM((SUBLANES + L, BC_WIDTH), F32),
                        pltpu.VMEM((SUBLANES + L, BC_WIDTH), F32), pltpu.VMEM((SSM_STATE, D_SSM), F32)],
        input_output_aliases=aliases,
        compiler_params=_params(("parallel", "arbitrary")),
        name="ssd_L%d" % L,
    )(*args)
    return y, hout


def _expand_onehot(per_head):
    e = np.zeros((LANES, SSM_HEADS * per_head), np.float32)
    for h in range(SSM_HEADS):
        e[h, h * per_head:(h + 1) * per_head] = 1.0
    return jnp.asarray(e, BF16)


def _ret_consts(L):
    lam = np.log1p(-np.exp2(-5.0 - np.arange(RET_HEADS, dtype=np.float64)))
    i = np.arange(L, dtype=np.float64)
    diff = i[:, None] - i[None, :]
    dmat = np.where(diff >= 0, np.exp(lam[:, None, None] * np.maximum(diff, 0.0)), 0.0)
    to_end = np.exp(lam[:, None] * (L - 1 - i)[None, :])
    q_dec = np.exp(lam[:, None] * (i + 1.0)[None, :])
    chunk_decay = np.exp(lam * L)
    expand = lambda t: np.repeat(t.T[:, :, None], RET_DK, axis=2).reshape(L, RET_HEADS * RET_DK)
    return (jnp.asarray(dmat, F32), jnp.asarray(expand(q_dec), F32), jnp.asarray(expand(to_end), F32),
            [float(v) for v in chunk_decay])


RET_HEADS_AHEAD = 8
RET_CHUNKS_PER_STEP = 4


def _ret_kernel(q_ref, k_ref, v_ref, g_ref, cos_ref, sin_ref, qdec_ref, kend_ref, dmat_ref, s0_ref,
                *rest, L, cps, chunk_decay, n_aliased):
    y_ref, sout_ref, s_ref = rest[n_aliased:]
    step = pl.program_id(1)

    @pl.when(step == 0)
    def _():
        s_ref[...] = s0_ref[0]

    half = RET_DK // 2

    def one_chunk(sub, carry):
        rows = slice(0, L) if cps == 1 else pl.ds(pl.multiple_of(sub * L, L), L)
        cos = cos_ref[rows, :]
        sin = sin_ref[rows, :]

        def first_stage(h):
            ks = slice(h * RET_DK, (h + 1) * RET_DK)
            qh = q_ref[rows, ks]
            kh = k_ref[rows, ks]
            qr = qh * cos + pltpu.roll(qh, half, 1) * sin
            kr = (kh * cos + pltpu.roll(kh, half, 1) * sin) * (RET_DK ** -0.5)
            vh = v_ref[rows, h * RET_DV:(h + 1) * RET_DV].astype(BF16)
            s_prev = s_ref[h]
            qk = _dot_nt(qr.astype(BF16), kr.astype(BF16))
            o_state = jnp.dot((qr * qdec_ref[:, ks]).astype(BF16), s_prev.astype(BF16),
                              preferred_element_type=F32)
            kv = _dot_tn((kr * kend_ref[:, ks]).astype(BF16), vh)
            s_ref[h] = chunk_decay[h] * s_prev + kv
            return qk, o_state, vh

        def second_stage(h, qk, o_state, vh):
            vs = slice(h * RET_DV, (h + 1) * RET_DV)
            s = qk * dmat_ref[h]
            o = jnp.dot(s.astype(BF16), vh, preferred_element_type=F32) + o_state
            o = o * lax.rsqrt(jnp.mean(o * o, axis=-1, keepdims=True) + EPS)
            y_ref[rows, vs] = (o * _silu(g_ref[rows, vs])).astype(y_ref.dtype)

        pending = [first_stage(h) for h in range(RET_HEADS_AHEAD)]
        for h in range(RET_HEADS):
            if h + RET_HEADS_AHEAD < RET_HEADS:
                pending.append(first_stage(h + RET_HEADS_AHEAD))
            second_stage(h, *pending.pop(0))
        return carry

    if cps == 1:
        one_chunk(0, 0)
    else:
        lax.fori_loop(0, cps, one_chunk, 0)

    @pl.when(step == pl.num_programs(1) - 1)
    def _():
        sout_ref[0] = s_ref[...]


def _ret_call(u, cos_t, sin_t, s0, layer, *, L, nb, nc, row0, out_layer, prev_out, prev_state):
    cps = RET_CHUNKS_PER_STEP if nc % RET_CHUNKS_PER_STEP == 0 else 1
    blk = cps * L
    nsteps = nc // cps
    rb0 = row0 // blk
    dmat, qdec, kend, chunk_decay = _ret_consts(L)
    hk = RET_HEADS * RET_DK
    hv = RET_HEADS * RET_DV

    def rows(width, colblock):
        return pl.BlockSpec((blk, width), lambda b, c: (rb0 + b * nsteps + c, colblock))

    in_specs = [rows(hk, T_RQ // hk), rows(hk, T_RK // hk), rows(hv, T_RV // hv), rows(hv, T_RG // hv),
                pl.BlockSpec((blk, RET_DK), lambda b, c: (c, 0)),
                pl.BlockSpec((blk, RET_DK), lambda b, c: (c, 0)),
                pl.BlockSpec((L, hk), lambda b, c: (0, 0)),
                pl.BlockSpec((L, hk), lambda b, c: (0, 0)),
                pl.BlockSpec((RET_HEADS, L, L), lambda b, c: (0, 0, 0)),
                pl.BlockSpec((None, 1, RET_HEADS, RET_DK, RET_DV), lambda b, c: (layer, b, 0, 0, 0))]
    args = [u, u, u, u, cos_t, sin_t, qdec, kend, dmat, s0]
    aliases = _append_aliased(in_specs, args, (prev_out, prev_state))
    y, sout = pl.pallas_call(
        functools.partial(_ret_kernel, L=L, cps=cps, chunk_decay=chunk_decay, n_aliased=len(aliases)),
        grid=(nb, nsteps),
        in_specs=in_specs,
        out_specs=[pl.BlockSpec((blk, hv), lambda b, c: (rb0 + b * nsteps + c, 0)),
                   pl.BlockSpec((None, 1, RET_HEADS, RET_DK, RET_DV), lambda b, c: (out_layer, b, 0, 0, 0))],
        out_shape=[jax.ShapeDtypeStruct((M_ALL, hv), BF16),
                   jax.ShapeDtypeStruct((DEPTH, nb, RET_HEADS, RET_DK, RET_DV), F32)],
        scratch_shapes=[pltpu.VMEM((RET_HEADS, RET_DK, RET_DV), F32)],
        input_output_aliases=aliases,
        compiler_params=_params(("parallel", "arbitrary")),
        name="retention_L%d" % L,
    )(*args)
    return y, sout


def _rope_tables(pos):
    half = RET_DK // 2
    inv = ROPE_BASE ** (-jnp.arange(half, dtype=F32) * 2.0 / RET_DK)
    ang = pos.astype(F32)[:, None] * inv[None, :]
    cos = jnp.cos(ang)
    sin = jnp.sin(ang)
    return jnp.concatenate([cos, cos], axis=1), jnp.concatenate([-sin, sin], axis=1)


def _head_norm(xh, g):
    return xh * lax.rsqrt(jnp.mean(xh * xh, axis=-1, keepdims=True) + EPS) * g


def _kv_prep_kernel(k_ref, v_ref, gk_ref, knf_ref, knb_ref, vb_ref):
    for h in range(ATT_HEADS):
        hs = slice(h * ATT_HEAD_DIM, (h + 1) * ATT_HEAD_DIM)
        kn = _head_norm(k_ref[:, hs], gk_ref[...])
        knf_ref[0, :, hs] = kn
        knb_ref[0, :, hs] = kn.astype(BF16)
    vb_ref[0] = v_ref[...].astype(BF16)


def _kv_prep_sample(u, gk):
    dm = ATT_HEADS * ATT_HEAD_DIM
    rb = M_PROMPT // M_SAMPLE
    return pl.pallas_call(
        _kv_prep_kernel,
        grid=(1,),
        in_specs=[pl.BlockSpec((M_SAMPLE, dm), lambda i: (rb, T_AK // dm)),
                  pl.BlockSpec((M_SAMPLE, dm), lambda i: (rb, T_AV // dm)),
                  pl.BlockSpec((1, ATT_HEAD_DIM), lambda i: (0, 0))],
        out_specs=[pl.BlockSpec((1, M_SAMPLE, dm), lambda i: (0, 0, 0)),
                   pl.BlockSpec((1, M_SAMPLE, dm), lambda i: (0, 0, 0)),
                   pl.BlockSpec((1, M_SAMPLE, dm), lambda i: (0, 0, 0))],
        out_shape=[jax.ShapeDtypeStruct((1, M_SAMPLE, dm), F32),
                   jax.ShapeDtypeStruct((1, M_SAMPLE, dm), BF16),
                   jax.ShapeDtypeStruct((1, M_SAMPLE, dm), BF16)],
        compiler_params=_params(("arbitrary",)),
        name="kv_prep_sample",
    )(u, u, gk)


def _scores(qn, k1, k2, bias1, bias2, valid1):
    scale = ATT_HEAD_DIM ** -0.5
    s1 = _dot_nt(qn, k1) * scale + bias1
    if valid1 is not None:
        s1 = jnp.where(valid1, s1, NEG_INF)
    s2 = _dot_nt(qn, k2) * scale + bias2
    return s1, s2


def _softmax_pv(s1, s2, v1, v2):
    m = jnp.maximum(jnp.max(s1, axis=-1, keepdims=True), jnp.max(s2, axis=-1, keepdims=True))
    p1 = jnp.exp(s1 - m)
    p2 = jnp.exp(s2 - m)
    inv = 1.0 / (jnp.sum(p1, axis=-1, keepdims=True) + jnp.sum(p2, axis=-1, keepdims=True))
    return (jnp.dot((p1 * inv).astype(BF16), v1, preferred_element_type=F32)
            + jnp.dot((p2 * inv).astype(BF16), v2, preferred_element_type=F32))


def _attend(qn, k1, v1, k2, v2, bias1, bias2, valid1):
    s1, s2 = _scores(qn, k1, k2, bias1, bias2, valid1)
    return _softmax_pv(s1, s2, v1, v2)


ATTN_HEADS_AHEAD = 16


ATTN_CHUNKS_PER_STEP = 4


def _attn_prompt_kernel(q_ref, k_ref, v_ref, b1_ref, b2_ref, gq_ref, gk_ref, y_ref, knf_ref, kb_ref, vb_ref):
    step = pl.program_id(1)
    heads = [slice(h * ATT_HEAD_DIM, (h + 1) * ATT_HEAD_DIM) for h in range(ATT_HEADS)]

    @pl.when(step == 0)
    def _():
        kb_ref[0:WINDOW, :] = jnp.zeros((WINDOW, kb_ref.shape[1]), BF16)
        vb_ref[0:WINDOW, :] = jnp.zeros((WINDOW, vb_ref.shape[1]), BF16)

    def one_chunk(sub, carry):
        c = step * ATTN_CHUNKS_PER_STEP + sub
        rows = pl.ds(pl.multiple_of(sub * CHUNK, CHUNK), CHUNK)
        base = pl.multiple_of(c * CHUNK, CHUNK)
        own = pl.multiple_of(c * CHUNK + WINDOW, CHUNK)

        for hs in heads:
            kn = _head_norm(k_ref[rows, hs], gk_ref[...])
            knf_ref[0, rows, hs] = kn
            kb_ref[pl.ds(own, CHUNK), hs] = kn.astype(BF16)
        vb_ref[pl.ds(own, CHUNK), :] = v_ref[rows, :].astype(BF16)

        col = lax.broadcasted_iota(jnp.int32, (CHUNK, WINDOW), 1)
        valid1 = col + c * CHUNK >= WINDOW

        def scores(h):
            hs = heads[h]
            qn = _head_norm(q_ref[rows, hs], gq_ref[...]).astype(BF16)
            return _scores(qn, kb_ref[pl.ds(base, WINDOW), hs], kb_ref[pl.ds(own, CHUNK), hs],
                           b1_ref[h], b2_ref[h], valid1)

        pending = [scores(h) for h in range(ATTN_HEADS_AHEAD)]
        for h in range(ATT_HEADS):
            if h + ATTN_HEADS_AHEAD < ATT_HEADS:
                pending.append(scores(h + ATTN_HEADS_AHEAD))
            s1, s2 = pending.pop(0)
            hs = heads[h]
            o = _softmax_pv(s1, s2, vb_ref[pl.ds(base, WINDOW), hs], vb_ref[pl.ds(own, CHUNK), hs])
            y_ref[rows, hs] = o.astype(y_ref.dtype)
        return carry

    lax.fori_loop(0, ATTN_CHUNKS_PER_STEP, one_chunk, 0)


def _attn_prompt_call(u, bias1, bias2, gq, gk):
    dm = ATT_HEADS * ATT_HEAD_DIM
    blk = ATTN_CHUNKS_PER_STEP * CHUNK
    nsteps = SEQ // blk
    keep = min(WINDOW, SEQ)
    assert SEQ % blk == 0 and keep % blk == 0
    first_kept = nsteps - keep // blk

    def rows(col0):
        return pl.BlockSpec((blk, dm), lambda b, c: (b * nsteps + c, col0 // dm))

    return pl.pallas_call(
        _attn_prompt_kernel,
        grid=(BATCH, nsteps),
        in_specs=[rows(T_AQ), rows(T_AK), rows(T_AV),
                  pl.BlockSpec((ATT_HEADS, CHUNK, WINDOW), lambda b, c: (0, 0, 0)),
                  pl.BlockSpec((ATT_HEADS, CHUNK, CHUNK), lambda b, c: (0, 0, 0)),
                  pl.BlockSpec((1, ATT_HEAD_DIM), lambda b, c: (0, 0)),
                  pl.BlockSpec((1, ATT_HEAD_DIM), lambda b, c: (0, 0))],
        out_specs=[pl.BlockSpec((blk, dm), lambda b, c: (b * nsteps + c, 0)),
                   pl.BlockSpec((1, blk, dm), lambda b, c: (b, jnp.maximum(c - first_kept, 0), 0))],
        out_shape=[jax.ShapeDtypeStruct((M_ALL, dm), BF16),
                   jax.ShapeDtypeStruct((BATCH, keep, dm), F32)],
        scratch_shapes=[pltpu.VMEM((SEQ + WINDOW, dm), BF16), pltpu.VMEM((SEQ + WINDOW, dm), BF16)],
        compiler_params=_params(("parallel", "arbitrary")),
        name="attn_prompt",
    )(u, u, u, bias1, bias2, gq, gk)


def _attn_sample_kernel(q_ref, kh_ref, vh_ref, kn_ref, vn_ref, b1_ref, b2_ref, gq_ref, prev_ref, y_ref):
    del prev_ref
    heads = [slice(h * ATT_HEAD_DIM, (h + 1) * ATT_HEAD_DIM) for h in range(ATT_HEADS)]
    hist = kh_ref.shape[1]
    qall = jnp.concatenate([_head_norm(q_ref[:, hs], gq_ref[...]).astype(BF16) for hs in heads], axis=0)
    k1 = kh_ref[0].reshape(hist * ATT_HEADS, ATT_HEAD_DIM).astype(BF16)
    v1 = vh_ref[0].reshape(hist * ATT_HEADS, ATT_HEAD_DIM).astype(BF16)
    k2 = jnp.concatenate([kn_ref[:, hs] for hs in heads], axis=0)
    v2 = jnp.concatenate([vn_ref[:, hs] for hs in heads], axis=0)
    o = _attend(qall, k1, v1, k2, v2, b1_ref[...], b2_ref[...], None)
    for h, hs in enumerate(heads):
        y_ref[:, hs] = o[h * DEC_SEQ:(h + 1) * DEC_SEQ].astype(y_ref.dtype)


def _bias_expand_kernel(t1_ref, t2_ref, o1_ref, o2_ref):
    rows, cb = o1_ref.shape
    row_head = lax.broadcasted_iota(jnp.int32, (rows, cb), 0) // DEC_SEQ
    col = lax.broadcasted_iota(jnp.int32, (rows, cb), 1)
    ej = lax.broadcasted_iota(jnp.int32, (LANES, cb), 0)
    ec = lax.broadcasted_iota(jnp.int32, (LANES, cb), 1)
    e1 = jnp.where(ec // ATT_HEADS == ej, 1.0, 0.0).astype(BF16)
    o1_ref[...] = jnp.where(col % ATT_HEADS == row_head, _exact_right(t1_ref[...], e1), NEG_INF)
    n2 = o2_ref.shape[1]
    row_head2 = lax.broadcasted_iota(jnp.int32, (rows, n2), 0) // DEC_SEQ
    col2 = lax.broadcasted_iota(jnp.int32, (rows, n2), 1)
    ej2 = lax.broadcasted_iota(jnp.int32, (LANES, n2), 0)
    ec2 = lax.broadcasted_iota(jnp.int32, (LANES, n2), 1)
    e2 = jnp.where(ec2 % DEC_SEQ == ej2, 1.0, 0.0).astype(BF16)
    o2_ref[...] = jnp.where(col2 // DEC_SEQ == row_head2, _exact_right(t2_ref[...], e2), NEG_INF)


def _head_masked_bias(tbl1, tbl2):
    nh, nq, hist = tbl1.shape
    rows = nh * nq
    t1 = tbl1.reshape(rows, hist)
    t2 = jnp.pad(tbl2.reshape(rows, nq), ((0, 0), (0, LANES - nq)))
    cb = LANES * nh
    return pl.pallas_call(
        _bias_expand_kernel,
        grid=(hist // LANES,),
        in_specs=[pl.BlockSpec((rows, LANES), lambda j: (0, j)),
                  pl.BlockSpec((rows, LANES), lambda j: (0, 0))],
        out_specs=[pl.BlockSpec((rows, cb), lambda j: (0, j)),
                   pl.BlockSpec((rows, nh * nq), lambda j: (0, 0))],
        out_shape=[jax.ShapeDtypeStruct((rows, hist * nh), F32),
                   jax.ShapeDtypeStruct((rows, nh * nq), F32)],
        compiler_params=_params(("arbitrary",)),
        name="bias_expand",
    )(t1, t2)


def _attn_sample_call(u, k_hist, v_hist, layer, kn_new, vn_new, bias1, bias2, gq, prev_out):
    dm = ATT_HEADS * ATT_HEAD_DIM
    rb0 = M_PROMPT // DEC_SEQ
    hist = k_hist.shape[2]
    cache_spec = pl.BlockSpec((None, 1, hist, ATT_HEADS, ATT_HEAD_DIM), lambda b: (layer, b, 0, 0, 0))
    return pl.pallas_call(
        _attn_sample_kernel,
        grid=(DEC_BATCH,),
        in_specs=[pl.BlockSpec((DEC_SEQ, dm), lambda b: (rb0 + b, T_AQ // dm)),
                  cache_spec,
                  cache_spec,
                  pl.BlockSpec((DEC_SEQ, dm), lambda b: (b, 0)),
                  pl.BlockSpec((DEC_SEQ, dm), lambda b: (b, 0)),
                  pl.BlockSpec((ATT_HEADS * DEC_SEQ, hist * ATT_HEADS), lambda b: (0, 0),
                               pipeline_mode=pl.Buffered(1)),
                  pl.BlockSpec((ATT_HEADS * DEC_SEQ, ATT_HEADS * DEC_SEQ), lambda b: (0, 0)),
                  pl.BlockSpec((1, ATT_HEAD_DIM), lambda b: (0, 0)),
                  pl.BlockSpec(memory_space=pl.ANY)],
        out_specs=pl.BlockSpec((DEC_SEQ, dm), lambda b: (rb0 + b, 0)),
        out_shape=jax.ShapeDtypeStruct((M_ALL, dm), BF16),
        input_output_aliases={8: 0},
        compiler_params=_params(("arbitrary",)),
        name="attn_sample",
    )(u, k_hist, v_hist, kn_new, vn_new, bias1, bias2, gq, prev_out)


def _bias_table(rel_bias):
    kbn = WINDOW + CHUNK
    period = kbn + CHUNK
    m = np.arange(period)
    d = np.where(m < kbn, m, m - period)
    idx = np.clip(WINDOW - d, -REL_CLIP, REL_CLIP) + REL_CLIP
    vec = rel_bias[:, idx]
    rows = jnp.tile(vec, (1, CHUNK))[:, :CHUNK * (period - 1)].reshape(ATT_HEADS, CHUNK, period - 1)
    return rows[:, :, :kbn]


def _layer_params(l, norm_mix, w_in, conv_w, conv_b, dt_bias, a_log, d_skip, ssm_norm, q_norm, k_norm,
                  rel_bias, w_br_ssm, w_br_ret, w_br_att, w_out, norm_ffn, w_ffn_in, w_ffn_out):
    assert w_in.shape[-1] == N_IN
    cw = conv_w[l]
    cb = conv_b[l].reshape(1, CONV_DIM)
    lp = {
        "norm_mix": norm_mix[l], "norm_ffn": norm_ffn[l],
        "conv_wx": cw[:, :D_SSM], "conv_wb": cw[:, D_SSM:D_SSM + BC_WIDTH], "conv_wc": cw[:, D_SSM + BC_WIDTH:],
        "conv_bx": cb[:, :D_SSM], "conv_bb": cb[:, D_SSM:D_SSM + BC_WIDTH], "conv_bc": cb[:, D_SSM + BC_WIDTH:],
        "dt_bias": jnp.pad(dt_bias[l], (0, LANES - SSM_HEADS)).reshape(1, LANES),
        "a_log": jnp.pad(a_log[l], (0, LANES - SSM_HEADS)).reshape(1, LANES),
        "d_skip_p": jnp.repeat(d_skip[l], SSM_HEAD_DIM).reshape(1, D_SSM),
        "ssm_norm": ssm_norm[l].reshape(1, D_SSM),
        "q_norm": q_norm[l].reshape(1, ATT_HEAD_DIM), "k_norm": k_norm[l].reshape(1, ATT_HEAD_DIM),
        "rel_bias": rel_bias[l],
    }
    return lp


def _row_block(t, row0, rows, col0, width):
    return lax.slice(t, (row0, col0), (row0 + rows, col0 + width))


def _layer(x, l, lp, w, state, tables, prev):
    cache_k, cache_v, state_ret, state_ssm4, hist8 = state
    hn = _rmsnorm(x, lp["norm_mix"])
    tm, tn = DENSE_TILES["in_proj"]
    u_head = _matmul(hn, w["w_in_t"], l, n=HEAD_WIDTH, out_dtype=F32, tm=tm, tn=tn, w_transposed=True,
                     name="in_proj_head")
    dt_raw = _matmul(hn, w["w_in_t"], l, n=LANES, out_dtype=F32, tm=tm, tn=LANES, w_col0=DT_COL0,
                     w_transposed=True, name="in_proj_dt")
    u_tail = _matmul(hn, w["w_in_t"], l, n=TAIL_WIDTH, out_dtype=F32, tm=tm, tn=tn, w_col0=TAIL_COL0,
                     w_transposed=True, name="in_proj_tail")

    ys, ssm_p = _ssd_call(u_head, dt_raw, tables["zeros_hist"], tables["zeros_h"], 0, lp, L=CHUNK, nb=BATCH,
                          nc=SEQ // CHUNK, row0=0, out_layer=l, prev_out=None, prev_state=prev["ssm_p"])
    ys, ssm_s = _ssd_call(u_head, dt_raw, hist8, state_ssm4, l, lp, L=DEC_SEQ, nb=DEC_BATCH, nc=1,
                          row0=M_PROMPT, out_layer=l, prev_out=ys, prev_state=prev["ssm_s"])

    yr, ret_p = _ret_call(u_tail, tables["cos_p"], tables["sin_p"], tables["zeros_s"], 0, L=CHUNK, nb=BATCH,
                          nc=SEQ // CHUNK, row0=0, out_layer=l, prev_out=None, prev_state=prev["ret_p"])
    yr, ret_s = _ret_call(u_tail, tables["cos_s"], tables["sin_s"], state_ret, l, L=DEC_SEQ, nb=DEC_BATCH,
                          nc=1, row0=M_PROMPT, out_layer=l, prev_out=yr, prev_state=prev["ret_s"])

    dm = ATT_HEADS * ATT_HEAD_DIM
    knf_s, knb_s, vb_s = _kv_prep_sample(u_tail, lp["k_norm"])
    hist = cache_k.shape[2]
    assert hist == WINDOW
    tbl = _bias_table(lp["rel_bias"])
    ya, knf_p = _attn_prompt_call(u_tail, tbl[:, :, :WINDOW], tbl[:, :, WINDOW:], lp["q_norm"], lp["k_norm"])
    bias1_s, bias2_s = _head_masked_bias(tbl[:, :DEC_SEQ, :hist], tbl[:, :DEC_SEQ, hist:hist + DEC_SEQ])
    ya = _attn_sample_call(u_tail, cache_k, cache_v, l, knb_s[0], vb_s[0], bias1_s, bias2_s, lp["q_norm"], ya)

    tm, tn = DENSE_TILES["merge"]
    mm = _gated_merge((ys, yr, ya), (w["w_br_ssm"], w["w_br_ret"], w["w_br_att"]), l, u_tail,
                      (T_GA, T_GB, T_GC), tm=tm, tn=tn)
    x1, h2 = _out_proj_norm(mm, w["w_out"], l, x, lp["norm_ffn"], tm=DENSE_TILES["out_proj"][0])

    tm, tn = DENSE_TILES["ffn_in"]
    hh = _swiglu_matmul(h2, w["w_ffn_in"], l, tm=tm, tn=tn)
    tm, tn = DENSE_TILES["ffn_out"]
    x2 = _matmul(hh, w["w_ffn_out"], l, n=D_MODEL, out_dtype=F32, tm=tm, tn=tn, addend=x1,
                 split_out=(l == DEPTH - 1), name="ffn_out")

    keep = min(WINDOW, SEQ)
    nconv = CONV_W - 1

    def conv_rows(row0):
        return jnp.concatenate([_row_block(u_head, row0, nconv, H_X, D_SSM),
                                _row_block(u_head, row0, nconv, H_B, BC_WIDTH),
                                _row_block(u_head, row0, nconv, H_C, BC_WIDTH)], axis=-1)

    pv = jnp.stack([_row_block(u_tail, (b + 1) * SEQ - keep, keep, T_AV, dm) for b in range(BATCH)])
    pconv = jnp.stack([conv_rows((b + 1) * SEQ - nconv) for b in range(BATCH)])
    sv = _row_block(u_tail, M_PROMPT, M_SAMPLE, T_AV, dm)
    s_xbc = jnp.concatenate([_row_block(u_head, M_PROMPT, M_SAMPLE, H_X, D_SSM),
                             _row_block(u_head, M_PROMPT, M_SAMPLE, H_B, BC_WIDTH),
                             _row_block(u_head, M_PROMPT, M_SAMPLE, H_C, BC_WIDTH)], axis=-1)
    sconv = s_xbc.reshape(DEC_BATCH, DEC_SEQ, CONV_DIM)[:, DEC_SEQ - nconv:]

    outs_p = (knf_p.reshape(BATCH, keep, ATT_HEADS, ATT_HEAD_DIM),
              pv.reshape(BATCH, keep, ATT_HEADS, ATT_HEAD_DIM),
              pconv)
    outs_s = (knf_s.reshape(DEC_BATCH, DEC_SEQ, ATT_HEADS, ATT_HEAD_DIM),
              sv.reshape(DEC_BATCH, DEC_SEQ, ATT_HEADS, ATT_HEAD_DIM),
              sconv)
    new_states = {"ssm_p": ssm_p, "ssm_s": ssm_s, "ret_p": ret_p, "ret_s": ret_s}
    return x2, outs_p, outs_s, new_states


def kernel(x_prompt, x_sample, cache_attn_k, cache_attn_v, state_ret, state_ssm, state_conv,
           norm_mix, w_in, conv_w, conv_b, dt_bias, a_log, d_skip, ssm_norm, q_norm, k_norm,
           rel_bias, w_br_ssm, w_br_ret, w_br_att, w_out, norm_ffn, w_ffn_in, w_ffn_out):
    x = (x_prompt.reshape(M_PROMPT, D_MODEL), x_sample.reshape(M_SAMPLE, D_MODEL))
    cos_p, sin_p = _rope_tables(jnp.arange(SEQ))
    cos_s, sin_s = _rope_tables(PAST_LEN + jnp.arange(DEC_SEQ))
    tables = {"cos_p": cos_p, "sin_p": sin_p, "cos_s": cos_s, "sin_s": sin_s,
              "zeros_hist": jnp.zeros((1, BATCH, SUBLANES, CONV_DIM), F32),
              "zeros_h": jnp.zeros((1, BATCH, D_SSM, SSM_STATE), F32),
              "zeros_s": jnp.zeros((1, BATCH, RET_HEADS, RET_DK, RET_DV), F32)}
    w = {"w_in_t": jnp.swapaxes(w_in, 1, 2), "w_br_ssm": w_br_ssm, "w_br_ret": w_br_ret, "w_br_att": w_br_att, "w_out": w_out,
         "w_ffn_in": w_ffn_in, "w_ffn_out": w_ffn_out}
    hist8 = jnp.pad(state_conv, ((0, 0), (0, 0), (SUBLANES - (CONV_W - 1), 0), (0, 0)))
    state = (cache_attn_k, cache_attn_v, state_ret,
             state_ssm.reshape(DEPTH, DEC_BATCH, D_SSM, SSM_STATE), hist8)
    acc_p = ([], [], [])
    acc_s = ([], [], [])
    prev = {"ssm_p": None, "ssm_s": None, "ret_p": None, "ret_s": None}
    for l in range(DEPTH):
        lp = _layer_params(l, norm_mix, w_in, conv_w, conv_b, dt_bias, a_log, d_skip, ssm_norm, q_norm,
                           k_norm, rel_bias, w_br_ssm, w_br_ret, w_br_att, w_out, norm_ffn, w_ffn_in,
                           w_ffn_out)
        x, outs_p, outs_s, prev = _layer(x, l, lp, w, state, tables, prev)
        for i in range(3):
            acc_p[i].append(outs_p[i])
            acc_s[i].append(outs_s[i])
    pk, pv, pconv = [jnp.stack(a) for a in acc_p]
    sk, sv, sconv = [jnp.stack(a) for a in acc_s]
    pr, sr = prev["ret_p"], prev["ret_s"]
    pssm = prev["ssm_p"].reshape(DEPTH, BATCH, SSM_HEADS, SSM_HEAD_DIM, SSM_STATE)
    sssm = prev["ssm_s"].reshape(DEPTH, DEC_BATCH, SSM_HEADS, SSM_HEAD_DIM, SSM_STATE)
    yp = x[0].reshape(BATCH, SEQ, D_MODEL)
    ys = x[1].reshape(DEC_BATCH, DEC_SEQ, D_MODEL)
    return (yp, ys, pk, pv, pr, pssm, pconv, sk, sv, sr, sssm, sconv)
```

```python
import functools
import math

import numpy as np
import jax
import jax.numpy as jnp
from jax import lax
from jax.experimental import pallas as pl
from jax.experimental.pallas import tpu as pltpu

F32 = jnp.float32
BF16 = jnp.bfloat16

D_MODEL = 2048
BATCH = 4
SEQ = 2048
DEPTH = 2
DEC_BATCH = 32
DEC_SEQ = 16
PAST_LEN = 1024
CHUNK = 64
EPS = 1e-6
NEG_INF = -1e30

D_SSM = 2048
SSM_HEAD_DIM = 64
SSM_HEADS = 32
SSM_GROUPS = 4
SSM_STATE = 128
SSM_GROUP_WIDTH = D_SSM // SSM_GROUPS
CONV_W = 4
CONV_DIM = D_SSM + 2 * SSM_GROUPS * SSM_STATE
BC_WIDTH = SSM_GROUPS * SSM_STATE

RET_HEADS = 8
RET_DK = 128
RET_DV = 256
ROPE_BASE = 10000.0

ATT_HEADS = 16
ATT_HEAD_DIM = 128
BAND_CHUNKS = 8
WINDOW = BAND_CHUNKS * CHUNK
REL_CLIP = 256

FFN_HIDDEN = 5632

M_PROMPT = BATCH * SEQ
M_SAMPLE = DEC_BATCH * DEC_SEQ
M_ALL = M_PROMPT + M_SAMPLE

LANES = 128
SUBLANES = 8
V7X_VMEM_LIMIT_BYTES = 56 * 1024 * 1024

H_Z, H_X, H_B, H_C = 0, 2048, 4096, 4608
HEAD_WIDTH = 5120
DT_COL0 = 5120
TAIL_COL0 = DT_COL0 + SSM_HEADS
T_RQ, T_RK, T_RV, T_RG, T_AQ, T_AK, T_AV, T_GA, T_GB, T_GC = (
    0, 1024, 2048, 4096, 6144, 8192, 10240, 12288, 14336, 16384)
TAIL_WIDTH = 18432
N_IN = TAIL_COL0 + TAIL_WIDTH


DENSE_TILES = {
    "in_proj": (1088, 1024),
    "merge": (544, 512),
    "out_proj": (256, D_MODEL),
    "ffn_in": (1088, 512),
    "ffn_out": (M_SAMPLE, 512),
}


def _params(sem):
    return pltpu.CompilerParams(dimension_semantics=sem, vmem_limit_bytes=V7X_VMEM_LIMIT_BYTES)


def _append_aliased(in_specs, args, out_buffers):
    aliases = {}
    for k, buf in enumerate(out_buffers):
        if buf is not None:
            in_specs.append(pl.BlockSpec(memory_space=pl.ANY))
            args.append(buf)
            aliases[len(args) - 1] = k
    return aliases


def _sigmoid(x):
    return 1.0 / (1.0 + jnp.exp(-x))


def _silu(x):
    return x * _sigmoid(x)


def _rmsnorm_kernel(x_ref, g_ref, o_ref):
    x = x_ref[...]
    ms = jnp.mean(x * x, axis=-1, keepdims=True)
    o_ref[...] = (x * lax.rsqrt(ms + EPS) * g_ref[...]).astype(o_ref.dtype)


def _rmsnorm_split_kernel(xp_ref, xs_ref, g_ref, o_ref):
    last = pl.num_programs(0) - 1

    @pl.when(pl.program_id(0) < last)
    def _():
        _rmsnorm_kernel(xp_ref, g_ref, o_ref)

    @pl.when(pl.program_id(0) == last)
    def _():
        _rmsnorm_kernel(xs_ref, g_ref, o_ref)


def _rmsnorm(x, g, tm=544):
    d = g.shape[-1]
    gspec = pl.BlockSpec((1, d), lambda i: (0, 0))
    if isinstance(x, (tuple, list)):
        tm = M_SAMPLE
        last_prompt_tile = M_PROMPT // tm - 1
        return pl.pallas_call(
            _rmsnorm_split_kernel,
            grid=(M_ALL // tm,),
            in_specs=[pl.BlockSpec((tm, d), lambda i: (jnp.minimum(i, last_prompt_tile), 0)),
                      pl.BlockSpec((tm, d), lambda i: (0, 0)), gspec],
            out_specs=pl.BlockSpec((tm, d), lambda i: (i, 0)),
            out_shape=jax.ShapeDtypeStruct((M_ALL, d), BF16),
            compiler_params=_params(("arbitrary",)),
            name="rmsnorm_split",
        )(x[0], x[1], g.reshape(1, d))
    m = x.shape[0]
    return pl.pallas_call(
        _rmsnorm_kernel,
        grid=(m // tm,),
        in_specs=[pl.BlockSpec((tm, d), lambda i: (i, 0)), gspec],
        out_specs=pl.BlockSpec((tm, d), lambda i: (i, 0)),
        out_shape=jax.ShapeDtypeStruct((m, d), BF16),
        compiler_params=_params(("parallel",)),
        name="rmsnorm",
    )(x, g.reshape(1, d))


W_CAST_ROWS = 256


def _cast_weight_tile(w_ref, wx_ref, wb_ref, shift):
    rows = wb_ref.shape[0]
    for r in range(0, rows - shift, W_CAST_ROWS):
        n = min(W_CAST_ROWS, rows - shift - r)
        wb_ref[r:r + n, :] = w_ref[r + shift:r + shift + n, :].astype(BF16)
    if shift:
        wb_ref[rows - shift:, :] = wx_ref[:shift, :].astype(BF16)


def _mm_kernel(*refs, shift, n_add, n_out, w_transposed):
    a_ref, w_ref = refs[0], refs[1]
    pos = 2
    wx_ref = None
    if shift:
        wx_ref = refs[pos]
        pos += 1
    add_refs = refs[pos:pos + n_add]
    pos += n_add
    o_refs = refs[pos:pos + n_out]
    wb_ref = refs[pos + n_out]
    i = pl.program_id(1)

    @pl.when(i == 0)
    def _():
        _cast_weight_tile(w_ref, wx_ref, wb_ref, shift)

    if w_transposed:
        r = _dot_nt(a_ref[...], wb_ref[...])
    else:
        r = jnp.dot(a_ref[...], wb_ref[...], preferred_element_type=F32)

    def finish(add_ref, o_ref):
        res = r if add_ref is None else add_ref[...] + r
        o_ref[...] = res.astype(o_ref.dtype)

    if n_add < 2 and n_out < 2:
        finish(add_refs[0] if n_add else None, o_refs[0])
    else:
        last = pl.num_programs(1) - 1

        @pl.when(i < last)
        def _():
            finish(add_refs[0] if n_add else None, o_refs[0])

        @pl.when(i == last)
        def _():
            finish(add_refs[-1] if n_add else None, o_refs[-1])


def _matmul(a, w, layer, *, n, out_dtype, tm, tn, w_col0=0, w_transposed=False, addend=None, split_out=False,
            name="matmul"):
    m, kdim = a.shape
    add_parts = isinstance(addend, (tuple, list))
    if add_parts or split_out:
        assert tm == M_SAMPLE and M_PROMPT % tm == 0 and m == M_ALL
    last_prompt_tile = M_PROMPT // tm - 1
    wj0 = w_col0 // tn
    shift = w_col0 - wj0 * tn
    extra = LANES
    assert shift == 0 or (w_transposed and shift % SUBLANES == 0 and shift <= extra and tn % extra == 0)
    if w_transposed:
        w_spec = pl.BlockSpec((None, tn, kdim), lambda j, i: (layer, j + wj0, 0))
        wb_shape = (tn, kdim)
    else:
        w_spec = pl.BlockSpec((None, kdim, tn), lambda j, i: (layer, 0, j + wj0))
        wb_shape = (kdim, tn)
    in_specs = [pl.BlockSpec((tm, kdim), lambda j, i: (i, 0)), w_spec]
    args = [a, w]
    if shift:
        in_specs.append(pl.BlockSpec((None, extra, kdim), lambda j, i: (layer, (j + wj0 + 1) * (tn // extra), 0)))
        args.append(w)
    prompt_rows = pl.BlockSpec((tm, tn), lambda j, i: (jnp.minimum(i, last_prompt_tile), j))
    sample_rows = pl.BlockSpec((tm, tn), lambda j, i: (0, j))
    all_rows = pl.BlockSpec((tm, tn), lambda j, i: (i, j))
    if add_parts:
        in_specs += [prompt_rows, sample_rows]
        args += list(addend)
    elif addend is not None:
        in_specs.append(all_rows)
        args.append(addend)
    n_add = 2 if add_parts else int(addend is not None)
    if split_out:
        out_specs = [prompt_rows, sample_rows]
        out_shape = [jax.ShapeDtypeStruct((M_PROMPT, n), out_dtype), jax.ShapeDtypeStruct((M_SAMPLE, n), out_dtype)]
    else:
        out_specs = all_rows
        out_shape = jax.ShapeDtypeStruct((m, n), out_dtype)
    return pl.pallas_call(
        functools.partial(_mm_kernel, shift=shift, n_add=n_add, n_out=2 if split_out else 1,
                          w_transposed=w_transposed),
        grid=(n // tn, m // tm),
        in_specs=in_specs,
        out_specs=out_specs,
        out_shape=out_shape,
        scratch_shapes=[pltpu.VMEM(wb_shape, BF16)],
        compiler_params=_params(("parallel", "arbitrary")),
        name=name,
    )(*args)


def _out_proj_norm_kernel(*refs, n_add, n_prompt_tiles):
    a_ref, w_ref = refs[0], refs[1]
    add_refs = refs[2:2 + n_add]
    g_ref, x_ref, hn_ref, wb_ref = refs[2 + n_add:]
    i = pl.program_id(0)

    @pl.when(i == 0)
    def _():
        _cast_weight_tile(w_ref, None, wb_ref, 0)

    r = jnp.dot(a_ref[...], wb_ref[...], preferred_element_type=F32)

    def finish(add_ref):
        x_ref[...] = add_ref[...] + r
        _rmsnorm_kernel(x_ref, g_ref, hn_ref)

    if n_add == 1:
        finish(add_refs[0])
    else:
        @pl.when(i < n_prompt_tiles)
        def _():
            finish(add_refs[0])

        @pl.when(i >= n_prompt_tiles)
        def _():
            finish(add_refs[1])


def _out_proj_norm(a, w, layer, addend, g, *, tm):
    m, kdim = a.shape
    d = w.shape[-1]
    add_parts = isinstance(addend, (tuple, list))
    assert M_PROMPT % tm == 0 and M_SAMPLE % tm == 0
    n_prompt_tiles = M_PROMPT // tm
    row_tile = lambda index: pl.BlockSpec((tm, d), index)
    in_specs = [pl.BlockSpec((tm, kdim), lambda i: (i, 0)),
                pl.BlockSpec((None, kdim, d), lambda i: (layer, 0, 0), pipeline_mode=pl.Buffered(1))]
    args = [a, w]
    if add_parts:
        in_specs += [row_tile(lambda i: (jnp.minimum(i, n_prompt_tiles - 1), 0)),
                     row_tile(lambda i: (jnp.maximum(i - n_prompt_tiles, 0), 0))]
        args += list(addend)
    else:
        in_specs.append(row_tile(lambda i: (i, 0)))
        args.append(addend)
    in_specs.append(pl.BlockSpec((1, d), lambda i: (0, 0)))
    args.append(g.reshape(1, d))
    return pl.pallas_call(
        functools.partial(_out_proj_norm_kernel, n_add=2 if add_parts else 1, n_prompt_tiles=n_prompt_tiles),
        grid=(m // tm,),
        in_specs=in_specs,
        out_specs=[row_tile(lambda i: (i, 0)), row_tile(lambda i: (i, 0))],
        out_shape=[jax.ShapeDtypeStruct((m, d), F32), jax.ShapeDtypeStruct((m, d), BF16)],
        scratch_shapes=[pltpu.VMEM((kdim, d), BF16)],
        compiler_params=_params(("arbitrary",)),
        name="out_proj_norm",
    )(*args)


def _merge_kernel(a0_ref, a1_ref, a2_ref, w0_ref, w1_ref, w2_ref, g0_ref, g1_ref, g2_ref, o_ref,
                  wb0_ref, wb1_ref, wb2_ref):
    @pl.when(pl.program_id(1) == 0)
    def _():
        for w_ref, wb_ref in ((w0_ref, wb0_ref), (w1_ref, wb1_ref), (w2_ref, wb2_ref)):
            _cast_weight_tile(w_ref, None, wb_ref, 0)

    acc = None
    for a_ref, wb_ref, g_ref in ((a0_ref, wb0_ref, g0_ref), (a1_ref, wb1_ref, g1_ref), (a2_ref, wb2_ref, g2_ref)):
        term = _sigmoid(g_ref[...]) * jnp.dot(a_ref[...], wb_ref[...], preferred_element_type=F32)
        acc = term if acc is None else acc + term
    o_ref[...] = acc.astype(o_ref.dtype)


def _gated_merge(branches, weights, layer, gates, gate_cols, *, tm, tn):
    m, kdim = branches[0].shape
    a_spec = pl.BlockSpec((tm, kdim), lambda j, i: (i, 0))
    w_spec = pl.BlockSpec((None, kdim, tn), lambda j, i: (layer, 0, j), pipeline_mode=pl.Buffered(1))
    g_specs = [pl.BlockSpec((tm, tn), functools.partial(lambda j, i, j0: (i, j + j0), j0=col // tn))
               for col in gate_cols]
    return pl.pallas_call(
        _merge_kernel,
        grid=(D_MODEL // tn, m // tm),
        in_specs=[a_spec] * 3 + [w_spec] * 3 + g_specs,
        out_specs=pl.BlockSpec((tm, tn), lambda j, i: (i, j)),
        out_shape=jax.ShapeDtypeStruct((m, D_MODEL), BF16),
        scratch_shapes=[pltpu.VMEM((kdim, tn), BF16)] * 3,
        compiler_params=_params(("parallel", "arbitrary")),
        name="gated_merge",
    )(*branches, *weights, gates, gates, gates)


def _swiglu_kernel(a_ref, wa_ref, wc_ref, o_ref, wab_ref, wcb_ref):
    @pl.when(pl.program_id(1) == 0)
    def _():
        _cast_weight_tile(wa_ref, None, wab_ref, 0)
        _cast_weight_tile(wc_ref, None, wcb_ref, 0)

    a = a_ref[...]
    fa = jnp.dot(a, wab_ref[...], preferred_element_type=F32)
    fc = jnp.dot(a, wcb_ref[...], preferred_element_type=F32)
    o_ref[...] = (_silu(fa) * fc).astype(o_ref.dtype)


def _swiglu_matmul(a, w, layer, *, tm, tn):
    m, kdim = a.shape
    nj = FFN_HIDDEN // tn
    return pl.pallas_call(
        _swiglu_kernel,
        grid=(nj, m // tm),
        in_specs=[pl.BlockSpec((tm, kdim), lambda j, i: (i, 0)),
                  pl.BlockSpec((None, kdim, tn), lambda j, i: (layer, 0, j)),
                  pl.BlockSpec((None, kdim, tn), lambda j, i: (layer, 0, j + nj))],
        out_specs=pl.BlockSpec((tm, tn), lambda j, i: (i, j)),
        out_shape=jax.ShapeDtypeStruct((m, FFN_HIDDEN), BF16),
        scratch_shapes=[pltpu.VMEM((kdim, tn), BF16), pltpu.VMEM((kdim, tn), BF16)],
        compiler_params=_params(("parallel", "arbitrary")),
        name="ffn_in_swiglu",
    )(a, w, w)


def _split3(a):
    hi = a.astype(BF16)
    r1 = a - hi.astype(F32)
    mid = r1.astype(BF16)
    lo = (r1 - mid.astype(F32)).astype(BF16)
    return hi, mid, lo


def _exact_right(a, onehot):
    return sum(jnp.dot(p, onehot, preferred_element_type=F32) for p in _split3(a))


def _exact_right_many(arrays, onehot):
    n = arrays[0].shape[0]
    r = jnp.dot(jnp.concatenate([p for a in arrays for p in _split3(a)], axis=0), onehot,
                preferred_element_type=F32)
    return [r[3 * k * n:(3 * k + 1) * n] + r[(3 * k + 1) * n:(3 * k + 2) * n] + r[(3 * k + 2) * n:(3 * k + 3) * n]
            for k in range(len(arrays))]


def _exact_left(onehot, b):
    return sum(jnp.dot(onehot, p, preferred_element_type=F32) for p in _split3(b))


def _dot_nt(a, b):
    return lax.dot_general(a, b, (((1,), (1,)), ((), ())), preferred_element_type=F32)


def _dot_tn(a, b):
    return lax.dot_general(a, b, (((0,), (0,)), ((), ())), preferred_element_type=F32)


def _conv_silu(ext_ref, cw, cb, L):
    acc = ext_ref[SUBLANES:SUBLANES + L, :] * cw[CONV_W - 1:CONV_W, :] + cb
    for s in range(1, CONV_W):
        acc = acc + ext_ref[SUBLANES - s:SUBLANES - s + L, :] * cw[CONV_W - 1 - s:CONV_W - s, :]
    return _silu(acc)


def _ssd_kernel(x_ref, b_ref, c_ref, z_ref, dt_ref, hist_ref, h0_ref,
                cwx_ref, cwb_ref, cwc_ref, cbx_ref, cbb_ref, cbc_ref,
                dtb_ref, alog_ref, dskip_ref, normw_ref, exp_p_ref, exp_l_ref,
                *rest, L, nc, n_aliased):
    y_ref, hout_ref, ex_ref, eb_ref, ec_ref, ht_ref = rest[n_aliased:]
    c = pl.program_id(1)
    lw = SSM_HEADS * L

    @pl.when(c == 0)
    def _():
        hist = hist_ref[0]
        ex_ref[0:SUBLANES, :] = hist[:, :D_SSM]
        eb_ref[0:SUBLANES, :] = hist[:, D_SSM:D_SSM + BC_WIDTH]
        ec_ref[0:SUBLANES, :] = hist[:, D_SSM + BC_WIDTH:]
        ht_ref[...] = h0_ref[0].T

    @pl.when(c > 0)
    def _():
        ex_ref[0:SUBLANES, :] = ex_ref[L:L + SUBLANES, :]
        eb_ref[0:SUBLANES, :] = eb_ref[L:L + SUBLANES, :]
        ec_ref[0:SUBLANES, :] = ec_ref[L:L + SUBLANES, :]

    ex_ref[SUBLANES:SUBLANES + L, :] = x_ref[...]
    eb_ref[SUBLANES:SUBLANES + L, :] = b_ref[...]
    ec_ref[SUBLANES:SUBLANES + L, :] = c_ref[...]
    xs = _conv_silu(ex_ref, cwx_ref[...], cbx_ref[...], L)
    bm = _conv_silu(eb_ref, cwb_ref[...], cbb_ref[...], L).astype(BF16)
    cm = _conv_silu(ec_ref, cwc_ref[...], cbc_ref[...], L).astype(BF16)

    dtr = dt_ref[...] + dtb_ref[...]
    dt = jnp.maximum(dtr, 0.0) + jnp.log1p(jnp.exp(-jnp.abs(dtr)))

    tri = (lax.broadcasted_iota(jnp.int32, (L, L), 1)
           <= lax.broadcasted_iota(jnp.int32, (L, L), 0)).astype(BF16)
    acum = _exact_left(tri, dt * (-jnp.exp(alog_ref[...])))

    if L == SSM_HEAD_DIM:
        dt_p = _exact_right(dt, exp_p_ref[...])
        acum_p = _exact_right(acum, exp_p_ref[...])
    else:
        dt_p, acum_p = _exact_right_many([dt, acum], exp_p_ref[...])
    last_p = acum_p[L - 1:L, :]
    if L == SSM_HEAD_DIM:
        dt_l, acum_l = dt_p, acum_p
    else:
        dt_l, acum_l = _exact_right_many([dt, acum], exp_l_ref[...])

    rowi = lax.broadcasted_iota(jnp.int32, (L, lw), 0)
    colj = lax.broadcasted_iota(jnp.int32, (L, lw), 1) % L
    diag = colj == rowi
    causal = colj <= rowi
    acum_row = jnp.sum(jnp.where(diag, acum_l, 0.0), axis=0, keepdims=True)
    dt_row = jnp.sum(jnp.where(diag, dt_l, 0.0), axis=0, keepdims=True)
    decay = jnp.where(causal, jnp.exp(jnp.where(causal, acum_l - acum_row, 0.0)), 0.0)

    heads_per_group = SSM_HEADS // SSM_GROUPS
    cb = jnp.concatenate(
        [_dot_nt(cm[:, g * SSM_STATE:(g + 1) * SSM_STATE],
                 jnp.concatenate([bm[:, g * SSM_STATE:(g + 1) * SSM_STATE]] * heads_per_group, axis=0))
         for g in range(SSM_GROUPS)], axis=1)

    ht = ht_ref[...]
    ht_b = ht.astype(BF16)
    to_end = jnp.exp(last_p - acum_p) * dt_p
    xw = (xs * to_end).astype(BF16)
    y_inter = []
    st_parts = []
    for g in range(SSM_GROUPS):
        gs = slice(g * SSM_GROUP_WIDTH, (g + 1) * SSM_GROUP_WIDTH)
        ns = slice(g * SSM_STATE, (g + 1) * SSM_STATE)
        y_inter.append(jnp.dot(cm[:, ns], ht_b[:, gs], preferred_element_type=F32))
        st_parts.append(_dot_tn(bm[:, ns], xw[:, gs]))
    ht_new = jnp.exp(last_p) * ht + jnp.concatenate(st_parts, axis=1)
    ht_ref[...] = ht_new

    m_all = (cb * decay * dt_row).astype(BF16)
    hp = LANES // L
    slab = hp * SSM_HEAD_DIM
    xs_b = xs.astype(BF16)
    lane_head = lax.broadcasted_iota(jnp.int32, (L, slab), 1) // SSM_HEAD_DIM
    y_parts = []
    for q in range(lw // LANES):
        xq = xs_b[:, q * slab:(q + 1) * slab]
        bd = jnp.concatenate([jnp.where(lane_head == h, xq, jnp.zeros_like(xq)) for h in range(hp)], axis=0)
        y_parts.append(jnp.dot(m_all[:, q * LANES:(q + 1) * LANES], bd, preferred_element_type=F32))
    y = (jnp.concatenate(y_parts, axis=1) + jnp.concatenate(y_inter, axis=1) * jnp.exp(acum_p)
         + dskip_ref[...] * xs)

    t = y * _silu(z_ref[...])
    outs = []
    for g in range(SSM_GROUPS):
        tg = t[:, g * SSM_GROUP_WIDTH:(g + 1) * SSM_GROUP_WIDTH]
        outs.append(tg * lax.rsqrt(jnp.mean(tg * tg, axis=-1, keepdims=True) + EPS))
    y_ref[...] = (jnp.concatenate(outs, axis=1) * normw_ref[...]).astype(y_ref.dtype)

    @pl.when(c == nc - 1)
    def _():
        hout_ref[0] = ht_new.T


def _ssd_call(u, dt_raw, hist8, h0, layer, lp, *, L, nb, nc, row0, out_layer, prev_out, prev_state):
    rb0 = row0 // L

    def rows(width, colblock):
        return pl.BlockSpec((L, width), lambda b, c: (rb0 + b * nc + c, colblock))

    def const(shape):
        return pl.BlockSpec(shape, lambda b, c: tuple(0 for _ in shape))

    lw = SSM_HEADS * L
    in_specs = [rows(D_SSM, H_X // D_SSM), rows(BC_WIDTH, H_B // BC_WIDTH), rows(BC_WIDTH, H_C // BC_WIDTH),
                rows(D_SSM, H_Z // D_SSM),
                pl.BlockSpec((L, LANES), lambda b, c: (rb0 + b * nc + c, 0)),
                pl.BlockSpec((None, 1, SUBLANES, CONV_DIM), lambda b, c: (layer, b, 0, 0)),
                pl.BlockSpec((None, 1, D_SSM, SSM_STATE), lambda b, c: (layer, b, 0, 0)),
                const((CONV_W, D_SSM)), const((CONV_W, BC_WIDTH)), const((CONV_W, BC_WIDTH)),
                const((1, D_SSM)), const((1, BC_WIDTH)), const((1, BC_WIDTH)),
                const((1, LANES)), const((1, LANES)), const((1, D_SSM)), const((1, D_SSM)),
                const((LANES, D_SSM)), const((LANES, lw))]
    args = [u, u, u, u, dt_raw, hist8, h0,
            lp["conv_wx"], lp["conv_wb"], lp["conv_wc"], lp["conv_bx"], lp["conv_bb"], lp["conv_bc"],
            lp["dt_bias"], lp["a_log"], lp["d_skip_p"], lp["ssm_norm"],
            _expand_onehot(SSM_HEAD_DIM), _expand_onehot(L)]
    aliases = _append_aliased(in_specs, args, (prev_out, prev_state))
    y, hout = pl.pallas_call(
        functools.partial(_ssd_kernel, L=L, nc=nc, n_aliased=len(aliases)),
        grid=(nb, nc),
        in_specs=in_specs,
        out_specs=[pl.BlockSpec((L, D_SSM), lambda b, c: (rb0 + b * nc + c, 0)),
                   pl.BlockSpec((None, 1, D_SSM, SSM_STATE), lambda b, c: (out_layer, b, 0, 0))],
        out_shape=[jax.ShapeDtypeStruct((M_ALL, D_SSM), BF16),
                   jax.ShapeDtypeStruct((DEPTH, nb, D_SSM, SSM_STATE), F32)],
        scratch_shapes=[pltpu.VMEM((SUBLANES + L, D_SSM), F32), pltpu.VMEM((SUBLANES + L, BC_WIDTH), F32),
                        pltpu.VMEM((SUBLANES + L, BC_WIDTH), F32), pltpu.VMEM((SSM_STATE, D_SSM), F32)],
        input_output_aliases=aliases,
        compiler_params=_params(("parallel", "arbitrary")),
        name="ssd_L%d" % L,
    )(*args)
    return y, hout


def _expand_onehot(per_head):
    e = np.zeros((LANES, SSM_HEADS * per_head), np.float32)
    for h in range(SSM_HEADS):
        e[h, h * per_head:(h + 1) * per_head] = 1.0
    return jnp.asarray(e, BF16)


def _ret_consts(L):
    lam = np.log1p(-np.exp2(-5.0 - np.arange(RET_HEADS, dtype=np.float64)))
    i = np.arange(L, dtype=np.float64)
    diff = i[:, None] - i[None, :]
    dmat = np.where(diff >= 0, np.exp(lam[:, None, None] * np.maximum(diff, 0.0)), 0.0)
    to_end = np.exp(lam[:, None] * (L - 1 - i)[None, :])
    q_dec = np.exp(lam[:, None] * (i + 1.0)[None, :])
    chunk_decay = np.exp(lam * L)
    expand = lambda t: np.repeat(t.T[:, :, None], RET_DK, axis=2).reshape(L, RET_HEADS * RET_DK)
    return (jnp.asarray(dmat, F32), jnp.asarray(expand(q_dec), F32), jnp.asarray(expand(to_end), F32),
            [float(v) for v in chunk_decay])


RET_HEADS_AHEAD = 8
RET_CHUNKS_PER_STEP = 4


def _ret_kernel(q_ref, k_ref, v_ref, g_ref, cos_ref, sin_ref, qdec_ref, kend_ref, dmat_ref, s0_ref,
                *rest, L, cps, chunk_decay, n_aliased):
    y_ref, sout_ref, s_ref = rest[n_aliased:]
    step = pl.program_id(1)

    @pl.when(step == 0)
    def _():
        s_ref[...] = s0_ref[0]

    half = RET_DK // 2

    def one_chunk(sub, carry):
        rows = slice(0, L) if cps == 1 else pl.ds(pl.multiple_of(sub * L, L), L)
        cos = cos_ref[rows, :]
        sin = sin_ref[rows, :]

        def first_stage(h):
            ks = slice(h * RET_DK, (h + 1) * RET_DK)
            qh = q_ref[rows, ks]
            kh = k_ref[rows, ks]
            qr = qh * cos + pltpu.roll(qh, half, 1) * sin
            kr = (kh * cos + pltpu.roll(kh, half, 1) * sin) * (RET_DK ** -0.5)
            vh = v_ref[rows, h * RET_DV:(h + 1) * RET_DV].astype(BF16)
            s_prev = s_ref[h]
            qk = _dot_nt(qr.astype(BF16), kr.astype(BF16))
            o_state = jnp.dot((qr * qdec_ref[:, ks]).astype(BF16), s_prev.astype(BF16),
                              preferred_element_type=F32)
            kv = _dot_tn((kr * kend_ref[:, ks]).astype(BF16), vh)
            s_ref[h] = chunk_decay[h] * s_prev + kv
            return qk, o_state, vh

        def second_stage(h, qk, o_state, vh):
            vs = slice(h * RET_DV, (h + 1) * RET_DV)
            s = qk * dmat_ref[h]
            o = jnp.dot(s.astype(BF16), vh, preferred_element_type=F32) + o_state
            o = o * lax.rsqrt(jnp.mean(o * o, axis=-1, keepdims=True) + EPS)
            y_ref[rows, vs] = (o * _silu(g_ref[rows, vs])).astype(y_ref.dtype)

        pending = [first_stage(h) for h in range(RET_HEADS_AHEAD)]
        for h in range(RET_HEADS):
            if h + RET_HEADS_AHEAD < RET_HEADS:
                pending.append(first_stage(h + RET_HEADS_AHEAD))
            second_stage(h, *pending.pop(0))
        return carry

    if cps == 1:
        one_chunk(0, 0)
    else:
        lax.fori_loop(0, cps, one_chunk, 0)

    @pl.when(step == pl.num_programs(1) - 1)
    def _():
        sout_ref[0] = s_ref[...]


def _ret_call(u, cos_t, sin_t, s0, layer, *, L, nb, nc, row0, out_layer, prev_out, prev_state):
    cps = RET_CHUNKS_PER_STEP if nc % RET_CHUNKS_PER_STEP == 0 else 1
    blk = cps * L
    nsteps = nc // cps
    rb0 = row0 // blk
    dmat, qdec, kend, chunk_decay = _ret_consts(L)
    hk = RET_HEADS * RET_DK
    hv = RET_HEADS * RET_DV

    def rows(width, colblock):
        return pl.BlockSpec((blk, width), lambda b, c: (rb0 + b * nsteps + c, colblock))

    in_specs = [rows(hk, T_RQ // hk), rows(hk, T_RK // hk), rows(hv, T_RV // hv), rows(hv, T_RG // hv),
                pl.BlockSpec((blk, RET_DK), lambda b, c: (c, 0)),
                pl.BlockSpec((blk, RET_DK), lambda b, c: (c, 0)),
                pl.BlockSpec((L, hk), lambda b, c: (0, 0)),
                pl.BlockSpec((L, hk), lambda b, c: (0, 0)),
                pl.BlockSpec((RET_HEADS, L, L), lambda b, c: (0, 0, 0)),
                pl.BlockSpec((None, 1, RET_HEADS, RET_DK, RET_DV), lambda b, c: (layer, b, 0, 0, 0))]
    args = [u, u, u, u, cos_t, sin_t, qdec, kend, dmat, s0]
    aliases = _append_aliased(in_specs, args, (prev_out, prev_state))
    y, sout = pl.pallas_call(
        functools.partial(_ret_kernel, L=L, cps=cps, chunk_decay=chunk_decay, n_aliased=len(aliases)),
        grid=(nb, nsteps),
        in_specs=in_specs,
        out_specs=[pl.BlockSpec((blk, hv), lambda b, c: (rb0 + b * nsteps + c, 0)),
                   pl.BlockSpec((None, 1, RET_HEADS, RET_DK, RET_DV), lambda b, c: (out_layer, b, 0, 0, 0))],
        out_shape=[jax.ShapeDtypeStruct((M_ALL, hv), BF16),
                   jax.ShapeDtypeStruct((DEPTH, nb, RET_HEADS, RET_DK, RET_DV), F32)],
        scratch_shapes=[pltpu.VMEM((RET_HEADS, RET_DK, RET_DV), F32)],
        input_output_aliases=aliases,
        compiler_params=_params(("parallel", "arbitrary")),
        name="retention_L%d" % L,
    )(*args)
    return y, sout


def _rope_tables(pos):
    half = RET_DK // 2
    inv = ROPE_BASE ** (-jnp.arange(half, dtype=F32) * 2.0 / RET_DK)
    ang = pos.astype(F32)[:, None] * inv[None, :]
    cos = jnp.cos(ang)
    sin = jnp.sin(ang)
    return jnp.concatenate([cos, cos], axis=1), jnp.concatenate([-sin, sin], axis=1)


def _head_norm(xh, g):
    return xh * lax.rsqrt(jnp.mean(xh * xh, axis=-1, keepdims=True) + EPS) * g


def _kv_prep_kernel(k_ref, v_ref, gk_ref, knf_ref, knb_ref, vb_ref):
    for h in range(ATT_HEADS):
        hs = slice(h * ATT_HEAD_DIM, (h + 1) * ATT_HEAD_DIM)
        kn = _head_norm(k_ref[:, hs], gk_ref[...])
        knf_ref[0, :, hs] = kn
        knb_ref[0, :, hs] = kn.astype(BF16)
    vb_ref[0] = v_ref[...].astype(BF16)


def _kv_prep_sample(u, gk):
    dm = ATT_HEADS * ATT_HEAD_DIM
    rb = M_PROMPT // M_SAMPLE
    return pl.pallas_call(
        _kv_prep_kernel,
        grid=(1,),
        in_specs=[pl.BlockSpec((M_SAMPLE, dm), lambda i: (rb, T_AK // dm)),
                  pl.BlockSpec((M_SAMPLE, dm), lambda i: (rb, T_AV // dm)),
                  pl.BlockSpec((1, ATT_HEAD_DIM), lambda i: (0, 0))],
        out_specs=[pl.BlockSpec((1, M_SAMPLE, dm), lambda i: (0, 0, 0)),
                   pl.BlockSpec((1, M_SAMPLE, dm), lambda i: (0, 0, 0)),
                   pl.BlockSpec((1, M_SAMPLE, dm), lambda i: (0, 0, 0))],
        out_shape=[jax.ShapeDtypeStruct((1, M_SAMPLE, dm), F32),
                   jax.ShapeDtypeStruct((1, M_SAMPLE, dm), BF16),
                   jax.ShapeDtypeStruct((1, M_SAMPLE, dm), BF16)],
        compiler_params=_params(("arbitrary",)),
        name="kv_prep_sample",
    )(u, u, gk)


def _scores(qn, k1, k2, bias1, bias2, valid1):
    scale = ATT_HEAD_DIM ** -0.5
    s1 = _dot_nt(qn, k1) * scale + bias1
    if valid1 is not None:
        s1 = jnp.where(valid1, s1, NEG_INF)
    s2 = _dot_nt(qn, k2) * scale + bias2
    return s1, s2


def _softmax_pv(s1, s2, v1, v2):
    m = jnp.maximum(jnp.max(s1, axis=-1, keepdims=True), jnp.max(s2, axis=-1, keepdims=True))
    p1 = jnp.exp(s1 - m)
    p2 = jnp.exp(s2 - m)
    inv = 1.0 / (jnp.sum(p1, axis=-1, keepdims=True) + jnp.sum(p2, axis=-1, keepdims=True))
    return (jnp.dot((p1 * inv).astype(BF16), v1, preferred_element_type=F32)
            + jnp.dot((p2 * inv).astype(BF16), v2, preferred_element_type=F32))


def _attend(qn, k1, v1, k2, v2, bias1, bias2, valid1):
    s1, s2 = _scores(qn, k1, k2, bias1, bias2, valid1)
    return _softmax_pv(s1, s2, v1, v2)


ATTN_HEADS_AHEAD = 16


ATTN_CHUNKS_PER_STEP = 4


def _attn_prompt_kernel(q_ref, k_ref, v_ref, b1_ref, b2_ref, gq_ref, gk_ref, y_ref, knf_ref, kb_ref, vb_ref):
    step = pl.program_id(1)
    heads = [slice(h * ATT_HEAD_DIM, (h + 1) * ATT_HEAD_DIM) for h in range(ATT_HEADS)]

    @pl.when(step == 0)
    def _():
        kb_ref[0:WINDOW, :] = jnp.zeros((WINDOW, kb_ref.shape[1]), BF16)
        vb_ref[0:WINDOW, :] = jnp.zeros((WINDOW, vb_ref.shape[1]), BF16)

    def one_chunk(sub, carry):
        c = step * ATTN_CHUNKS_PER_STEP + sub
        rows = pl.ds(pl.multiple_of(sub * CHUNK, CHUNK), CHUNK)
        base = pl.multiple_of(c * CHUNK, CHUNK)
        own = pl.multiple_of(c * CHUNK + WINDOW, CHUNK)

        for hs in heads:
            kn = _head_norm(k_ref[rows, hs], gk_ref[...])
            knf_ref[0, rows, hs] = kn
            kb_ref[pl.ds(own, CHUNK), hs] = kn.astype(BF16)
        vb_ref[pl.ds(own, CHUNK), :] = v_ref[rows, :].astype(BF16)

        col = lax.broadcasted_iota(jnp.int32, (CHUNK, WINDOW), 1)
        valid1 = col + c * CHUNK >= WINDOW

        def scores(h):
            hs = heads[h]
            qn = _head_norm(q_ref[rows, hs], gq_ref[...]).astype(BF16)
            return _scores(qn, kb_ref[pl.ds(base, WINDOW), hs], kb_ref[pl.ds(own, CHUNK), hs],
                           b1_ref[h], b2_ref[h], valid1)

        pending = [scores(h) for h in range(ATTN_HEADS_AHEAD)]
        for h in range(ATT_HEADS):
            if h + ATTN_HEADS_AHEAD < ATT_HEADS:
                pending.append(scores(h + ATTN_HEADS_AHEAD))
            s1, s2 = pending.pop(0)
            hs = heads[h]
            o = _softmax_pv(s1, s2, vb_ref[pl.ds(base, WINDOW), hs], vb_ref[pl.ds(own, CHUNK), hs])
            y_ref[rows, hs] = o.astype(y_ref.dtype)
        return carry

    lax.fori_loop(0, ATTN_CHUNKS_PER_STEP, one_chunk, 0)


def _attn_prompt_call(u, bias1, bias2, gq, gk):
    dm = ATT_HEADS * ATT_HEAD_DIM
    blk = ATTN_CHUNKS_PER_STEP * CHUNK
    nsteps = SEQ // blk
    keep = min(WINDOW, SEQ)
    assert SEQ % blk == 0 and keep % blk == 0
    first_kept = nsteps - keep // blk

    def rows(col0):
        return pl.BlockSpec((blk, dm), lambda b, c: (b * nsteps + c, col0 // dm))

    return pl.pallas_call(
        _attn_prompt_kernel,
        grid=(BATCH, nsteps),
        in_specs=[rows(T_AQ), rows(T_AK), rows(T_AV),
                  pl.BlockSpec((ATT_HEADS, CHUNK, WINDOW), lambda b, c: (0, 0, 0)),
                  pl.BlockSpec((ATT_HEADS, CHUNK, CHUNK), lambda b, c: (0, 0, 0)),
                  pl.BlockSpec((1, ATT_HEAD_DIM), lambda b, c: (0, 0)),
                  pl.BlockSpec((1, ATT_HEAD_DIM), lambda b, c: (0, 0))],
        out_specs=[pl.BlockSpec((blk, dm), lambda b, c: (b * nsteps + c, 0)),
                   pl.BlockSpec((1, blk, dm), lambda b, c: (b, jnp.maximum(c - first_kept, 0), 0))],
        out_shape=[jax.ShapeDtypeStruct((M_ALL, dm), BF16),
                   jax.ShapeDtypeStruct((BATCH, keep, dm), F32)],
        scratch_shapes=[pltpu.VMEM((SEQ + WINDOW, dm), BF16), pltpu.VMEM((SEQ + WINDOW, dm), BF16)],
        compiler_params=_params(("parallel", "arbitrary")),
        name="attn_prompt",
    )(u, u, u, bias1, bias2, gq, gk)


ATTN_SAMPLE_ROW_GROUPS = 2


def _attn_sample_kernel(q_ref, kh_ref, vh_ref, kn_ref, vn_ref, b1_ref, b2_ref, gq_ref, prev_ref, y_ref):
    del prev_ref
    heads = [slice(h * ATT_HEAD_DIM, (h + 1) * ATT_HEAD_DIM) for h in range(ATT_HEADS)]
    hist = kh_ref.shape[1]
    qall = jnp.concatenate([_head_norm(q_ref[:, hs], gq_ref[...]).astype(BF16) for hs in heads], axis=0)
    k1 = kh_ref[0].reshape(hist * ATT_HEADS, ATT_HEAD_DIM).astype(BF16)
    v1 = vh_ref[0].reshape(hist * ATT_HEADS, ATT_HEAD_DIM).astype(BF16)
    k2 = jnp.concatenate([kn_ref[:, hs] for hs in heads], axis=0)
    v2 = jnp.concatenate([vn_ref[:, hs] for hs in heads], axis=0)
    nrows = qall.shape[0]
    step = nrows // ATTN_SAMPLE_ROW_GROUPS
    groups = [slice(r, r + step) for r in range(0, nrows, step)]
    sc = [_scores(qall[rs], k1, k2, b1_ref[rs, :], b2_ref[rs, :], None) for rs in groups]
    o = jnp.concatenate([_softmax_pv(s1, s2, v1, v2) for s1, s2 in sc], axis=0)
    for h, hs in enumerate(heads):
        y_ref[:, hs] = o[h * DEC_SEQ:(h + 1) * DEC_SEQ].astype(y_ref.dtype)


def _bias_expand_kernel(t1_ref, t2_ref, o1_ref, o2_ref):
    rows, cb = o1_ref.shape
    row_head = lax.broadcasted_iota(jnp.int32, (rows, cb), 0) // DEC_SEQ
    col = lax.broadcasted_iota(jnp.int32, (rows, cb), 1)
    ej = lax.broadcasted_iota(jnp.int32, (LANES, cb), 0)
    ec = lax.broadcasted_iota(jnp.int32, (LANES, cb), 1)
    e1 = jnp.where(ec // ATT_HEADS == ej, 1.0, 0.0).astype(BF16)
    o1_ref[...] = jnp.where(col % ATT_HEADS == row_head, _exact_right(t1_ref[...], e1), NEG_INF)
    n2 = o2_ref.shape[1]
    row_head2 = lax.broadcasted_iota(jnp.int32, (rows, n2), 0) // DEC_SEQ
    col2 = lax.broadcasted_iota(jnp.int32, (rows, n2), 1)
    ej2 = lax.broadcasted_iota(jnp.int32, (LANES, n2), 0)
    ec2 = lax.broadcasted_iota(jnp.int32, (LANES, n2), 1)
    e2 = jnp.where(ec2 % DEC_SEQ == ej2, 1.0, 0.0).astype(BF16)
    o2_ref[...] = jnp.where(col2 // DEC_SEQ == row_head2, _exact_right(t2_ref[...], e2), NEG_INF)


def _head_masked_bias(tbl1, tbl2):
    nh, nq, hist = tbl1.shape
    rows = nh * nq
    t1 = tbl1.reshape(rows, hist)
    t2 = jnp.pad(tbl2.reshape(rows, nq), ((0, 0), (0, LANES - nq)))
    cb = LANES * nh
    return pl.pallas_call(
        _bias_expand_kernel,
        grid=(hist // LANES,),
        in_specs=[pl.BlockSpec((rows, LANES), lambda j: (0, j)),
                  pl.BlockSpec((rows, LANES), lambda j: (0, 0))],
        out_specs=[pl.BlockSpec((rows, cb), lambda j: (0, j)),
                   pl.BlockSpec((rows, nh * nq), lambda j: (0, 0))],
        out_shape=[jax.ShapeDtypeStruct((rows, hist * nh), F32),
                   jax.ShapeDtypeStruct((rows, nh * nq), F32)],
        compiler_params=_params(("arbitrary",)),
        name="bias_expand",
    )(t1, t2)


def _attn_sample_call(u, k_hist, v_hist, layer, kn_new, vn_new, bias1, bias2, gq, prev_out):
    dm = ATT_HEADS * ATT_HEAD_DIM
    rb0 = M_PROMPT // DEC_SEQ
    hist = k_hist.shape[2]
    cache_spec = pl.BlockSpec((None, 1, hist, ATT_HEADS, ATT_HEAD_DIM), lambda b: (layer, b, 0, 0, 0))
    return pl.pallas_call(
        _attn_sample_kernel,
        grid=(DEC_BATCH,),
        in_specs=[pl.BlockSpec((DEC_SEQ, dm), lambda b: (rb0 + b, T_AQ // dm)),
                  cache_spec,
                  cache_spec,
                  pl.BlockSpec((DEC_SEQ, dm), lambda b: (b, 0)),
                  pl.BlockSpec((DEC_SEQ, dm), lambda b: (b, 0)),
                  pl.BlockSpec((ATT_HEADS * DEC_SEQ, hist * ATT_HEADS), lambda b: (0, 0),
                               pipeline_mode=pl.Buffered(1)),
                  pl.BlockSpec((ATT_HEADS * DEC_SEQ, ATT_HEADS * DEC_SEQ), lambda b: (0, 0)),
                  pl.BlockSpec((1, ATT_HEAD_DIM), lambda b: (0, 0)),
                  pl.BlockSpec(memory_space=pl.ANY)],
        out_specs=pl.BlockSpec((DEC_SEQ, dm), lambda b: (rb0 + b, 0)),
        out_shape=jax.ShapeDtypeStruct((M_ALL, dm), BF16),
        input_output_aliases={8: 0},
        compiler_params=_params(("arbitrary",)),
        name="attn_sample",
    )(u, k_hist, v_hist, kn_new, vn_new, bias1, bias2, gq, prev_out)


def _bias_table(rel_bias):
    kbn = WINDOW + CHUNK
    period = kbn + CHUNK
    m = np.arange(period)
    d = np.where(m < kbn, m, m - period)
    idx = np.clip(WINDOW - d, -REL_CLIP, REL_CLIP) + REL_CLIP
    vec = rel_bias[:, idx]
    rows = jnp.tile(vec, (1, CHUNK))[:, :CHUNK * (period - 1)].reshape(ATT_HEADS, CHUNK, period - 1)
    return rows[:, :, :kbn]


def _layer_params(l, norm_mix, w_in, conv_w, conv_b, dt_bias, a_log, d_skip, ssm_norm, q_norm, k_norm,
                  rel_bias, w_br_ssm, w_br_ret, w_br_att, w_out, norm_ffn, w_ffn_in, w_ffn_out):
    assert w_in.shape[-1] == N_IN
    cw = conv_w[l]
    cb = conv_b[l].reshape(1, CONV_DIM)
    lp = {
        "norm_mix": norm_mix[l], "norm_ffn": norm_ffn[l],
        "conv_wx": cw[:, :D_SSM], "conv_wb": cw[:, D_SSM:D_SSM + BC_WIDTH], "conv_wc": cw[:, D_SSM + BC_WIDTH:],
        "conv_bx": cb[:, :D_SSM], "conv_bb": cb[:, D_SSM:D_SSM + BC_WIDTH], "conv_bc": cb[:, D_SSM + BC_WIDTH:],
        "dt_bias": jnp.pad(dt_bias[l], (0, LANES - SSM_HEADS)).reshape(1, LANES),
        "a_log": jnp.pad(a_log[l], (0, LANES - SSM_HEADS)).reshape(1, LANES),
        "d_skip_p": jnp.repeat(d_skip[l], SSM_HEAD_DIM).reshape(1, D_SSM),
        "ssm_norm": ssm_norm[l].reshape(1, D_SSM),
        "q_norm": q_norm[l].reshape(1, ATT_HEAD_DIM), "k_norm": k_norm[l].reshape(1, ATT_HEAD_DIM),
        "rel_bias": rel_bias[l],
    }
    return lp


def _row_block(t, row0, rows, col0, width):
    return lax.slice(t, (row0, col0), (row0 + rows, col0 + width))


def _layer(x, l, lp, w, state, tables, prev):
    cache_k, cache_v, state_ret, state_ssm4, hist8 = state
    hn = _rmsnorm(x, lp["norm_mix"])
    tm, tn = DENSE_TILES["in_proj"]
    u_head = _matmul(hn, w["w_in_t"], l, n=HEAD_WIDTH, out_dtype=F32, tm=tm, tn=tn, w_transposed=True,
                     name="in_proj_head")
    dt_raw = _matmul(hn, w["w_in_t"], l, n=LANES, out_dtype=F32, tm=tm, tn=LANES, w_col0=DT_COL0,
                     w_transposed=True, name="in_proj_dt")
    u_tail = _matmul(hn, w["w_in_t"], l, n=TAIL_WIDTH, out_dtype=F32, tm=tm, tn=tn, w_col0=TAIL_COL0,
                     w_transposed=True, name="in_proj_tail")

    ys, ssm_p = _ssd_call(u_head, dt_raw, tables["zeros_hist"], tables["zeros_h"], 0, lp, L=CHUNK, nb=BATCH,
                          nc=SEQ // CHUNK, row0=0, out_layer=l, prev_out=None, prev_state=prev["ssm_p"])
    ys, ssm_s = _ssd_call(u_head, dt_raw, hist8, state_ssm4, l, lp, L=DEC_SEQ, nb=DEC_BATCH, nc=1,
                          row0=M_PROMPT, out_layer=l, prev_out=ys, prev_state=prev["ssm_s"])

    yr, ret_p = _ret_call(u_tail, tables["cos_p"], tables["sin_p"], tables["zeros_s"], 0, L=CHUNK, nb=BATCH,
                          nc=SEQ // CHUNK, row0=0, out_layer=l, prev_out=None, prev_state=prev["ret_p"])
    yr, ret_s = _ret_call(u_tail, tables["cos_s"], tables["sin_s"], state_ret, l, L=DEC_SEQ, nb=DEC_BATCH,
                          nc=1, row0=M_PROMPT, out_layer=l, prev_out=yr, prev_state=prev["ret_s"])

    dm = ATT_HEADS * ATT_HEAD_DIM
    knf_s, knb_s, vb_s = _kv_prep_sample(u_tail, lp["k_norm"])
    hist = cache_k.shape[2]
    assert hist == WINDOW
    tbl = _bias_table(lp["rel_bias"])
    ya, knf_p = _attn_prompt_call(u_tail, tbl[:, :, :WINDOW], tbl[:, :, WINDOW:], lp["q_norm"], lp["k_norm"])
    bias1_s, bias2_s = _head_masked_bias(tbl[:, :DEC_SEQ, :hist], tbl[:, :DEC_SEQ, hist:hist + DEC_SEQ])
    ya = _attn_sample_call(u_tail, cache_k, cache_v, l, knb_s[0], vb_s[0], bias1_s, bias2_s, lp["q_norm"], ya)

    tm, tn = DENSE_TILES["merge"]
    mm = _gated_merge((ys, yr, ya), (w["w_br_ssm"], w["w_br_ret"], w["w_br_att"]), l, u_tail,
                      (T_GA, T_GB, T_GC), tm=tm, tn=tn)
    x1, h2 = _out_proj_norm(mm, w["w_out"], l, x, lp["norm_ffn"], tm=DENSE_TILES["out_proj"][0])

    tm, tn = DENSE_TILES["ffn_in"]
    hh = _swiglu_matmul(h2, w["w_ffn_in"], l, tm=tm, tn=tn)
    tm, tn = DENSE_TILES["ffn_out"]
    x2 = _matmul(hh, w["w_ffn_out"], l, n=D_MODEL, out_dtype=F32, tm=tm, tn=tn, addend=x1,
                 split_out=(l == DEPTH - 1), name="ffn_out")

    keep = min(WINDOW, SEQ)
    nconv = CONV_W - 1

    def conv_rows(row0):
        return jnp.concatenate([_row_block(u_head, row0, nconv, H_X, D_SSM),
                                _row_block(u_head, row0, nconv, H_B, BC_WIDTH),
                                _row_block(u_head, row0, nconv, H_C, BC_WIDTH)], axis=-1)

    pv = jnp.stack([_row_block(u_tail, (b + 1) * SEQ - keep, keep, T_AV, dm) for b in range(BATCH)])
    pconv = jnp.stack([conv_rows((b + 1) * SEQ - nconv) for b in range(BATCH)])
    sv = _row_block(u_tail, M_PROMPT, M_SAMPLE, T_AV, dm)
    s_xbc = jnp.concatenate([_row_block(u_head, M_PROMPT, M_SAMPLE, H_X, D_SSM),
                             _row_block(u_head, M_PROMPT, M_SAMPLE, H_B, BC_WIDTH),
                             _row_block(u_head, M_PROMPT, M_SAMPLE, H_C, BC_WIDTH)], axis=-1)
    sconv = s_xbc.reshape(DEC_BATCH, DEC_SEQ, CONV_DIM)[:, DEC_SEQ - nconv:]

    outs_p = (knf_p.reshape(BATCH, keep, ATT_HEADS, ATT_HEAD_DIM),
              pv.reshape(BATCH, keep, ATT_HEADS, ATT_HEAD_DIM),
              pconv)
    outs_s = (knf_s.reshape(DEC_BATCH, DEC_SEQ, ATT_HEADS, ATT_HEAD_DIM),
              sv.reshape(DEC_BATCH, DEC_SEQ, ATT_HEADS, ATT_HEAD_DIM),
              sconv)
    new_states = {"ssm_p": ssm_p, "ssm_s": ssm_s, "ret_p": ret_p, "ret_s": ret_s}
    return x2, outs_p, outs_s, new_states


def kernel(x_prompt, x_sample, cache_attn_k, cache_attn_v, state_ret, state_ssm, state_conv,
           norm_mix, w_in, conv_w, conv_b, dt_bias, a_log, d_skip, ssm_norm, q_norm, k_norm,
           rel_bias, w_br_ssm, w_br_ret, w_br_att, w_out, norm_ffn, w_ffn_in, w_ffn_out):
    x = (x_prompt.reshape(M_PROMPT, D_MODEL), x_sample.reshape(M_SAMPLE, D_MODEL))
    cos_p, sin_p = _rope_tables(jnp.arange(SEQ))
    cos_s, sin_s = _rope_tables(PAST_LEN + jnp.arange(DEC_SEQ))
    tables = {"cos_p": cos_p, "sin_p": sin_p, "cos_s": cos_s, "sin_s": sin_s,
              "zeros_hist": jnp.zeros((1, BATCH, SUBLANES, CONV_DIM), F32),
              "zeros_h": jnp.zeros((1, BATCH, D_SSM, SSM_STATE), F32),
              "zeros_s": jnp.zeros((1, BATCH, RET_HEADS, RET_DK, RET_DV), F32)}
    w = {"w_in_t": jnp.swapaxes(w_in, 1, 2), "w_br_ssm": w_br_ssm, "w_br_ret": w_br_ret, "w_br_att": w_br_att, "w_out": w_out,
         "w_ffn_in": w_ffn_in, "w_ffn_out": w_ffn_out}
    hist8 = jnp.pad(state_conv, ((0, 0), (0, 0), (SUBLANES - (CONV_W - 1), 0), (0, 0)))
    state = (cache_attn_k, cache_attn_v, state_ret,
             state_ssm.reshape(DEPTH, DEC_BATCH, D_SSM, SSM_STATE), hist8)
    acc_p = ([], [], [])
    acc_s = ([], [], [])
    prev = {"ssm_p": None, "ssm_s": None, "ret_p": None, "ret_s": None}
    for l in range(DEPTH):
        lp = _layer_params(l, norm_mix, w_in, conv_w, conv_b, dt_bias, a_log, d_skip, ssm_norm, q_norm,
                           k_norm, rel_bias, w_br_ssm, w_br_ret, w_br_att, w_out, norm_ffn, w_ffn_in,
                           w_ffn_out)
        x, outs_p, outs_s, prev = _layer(x, l, lp, w, state, tables, prev)
        for i in range(3):
            acc_p[i].append(outs_p[i])
            acc_s[i].append(outs_s[i])
    pk, pv, pconv = [jnp.stack(a) for a in acc_p]
    sk, sv, sconv = [jnp.stack(a) for a in acc_s]
    pr, sr = prev["ret_p"], prev["ret_s"]
    pssm = prev["ssm_p"].reshape(DEPTH, BATCH, SSM_HEADS, SSM_HEAD_DIM, SSM_STATE)
    sssm = prev["ssm_s"].reshape(DEPTH, DEC_BATCH, SSM_HEADS, SSM_HEAD_DIM, SSM_STATE)
    yp = x[0].reshape(BATCH, SEQ, D_MODEL)
    ys = x[1].reshape(DEC_BATCH, DEC_SEQ, D_MODEL)
    return (yp, ys, pk, pv, pr, pssm, pconv, sk, sv, sr, sssm, sconv)
```

```python
import functools
import math

import numpy as np
import jax
import jax.numpy as jnp
from jax import lax
from jax.experimental import pallas as pl
from jax.experimental.pallas import tpu as pltpu

F32 = jnp.float32
BF16 = jnp.bfloat16

D_MODEL = 2048
BATCH = 4
SEQ = 2048
DEPTH = 2
DEC_BATCH = 32
DEC_SEQ = 16
PAST_LEN = 1024
CHUNK = 64
EPS = 1e-6
NEG_INF = -1e30

D_SSM = 2048
SSM_HEAD_DIM = 64
SSM_HEADS = 32
SSM_GROUPS = 4
SSM_STATE = 128
SSM_GROUP_WIDTH = D_SSM // SSM_GROUPS
CONV_W = 4
CONV_DIM = D_SSM + 2 * SSM_GROUPS * SSM_STATE
BC_WIDTH = SSM_GROUPS * SSM_STATE

RET_HEADS = 8
RET_DK = 128
RET_DV = 256
ROPE_BASE = 10000.0

ATT_HEADS = 16
ATT_HEAD_DIM = 128
BAND_CHUNKS = 8
WINDOW = BAND_CHUNKS * CHUNK
REL_CLIP = 256

FFN_HIDDEN = 5632

M_PROMPT = BATCH * SEQ
M_SAMPLE = DEC_BATCH * DEC_SEQ
M_ALL = M_PROMPT + M_SAMPLE

LANES = 128
SUBLANES = 8
V7X_VMEM_LIMIT_BYTES = 56 * 1024 * 1024

H_Z, H_X, H_B, H_C = 0, 2048, 4096, 4608
HEAD_WIDTH = 5120
DT_COL0 = 5120
TAIL_COL0 = DT_COL0 + SSM_HEADS
T_RQ, T_RK, T_RV, T_RG, T_AQ, T_AK, T_AV, T_GA, T_GB, T_GC = (
    0, 1024, 2048, 4096, 6144, 8192, 10240, 12288, 14336, 16384)
TAIL_WIDTH = 18432
N_IN = TAIL_COL0 + TAIL_WIDTH


DENSE_TILES = {
    "in_proj": (1088, 1024),
    "merge": (544, 512),
    "out_proj": (256, D_MODEL),
    "ffn_in": (1088, 512),
    "ffn_out": (M_SAMPLE, 512),
}


def _params(sem):
    return pltpu.CompilerParams(dimension_semantics=sem, vmem_limit_bytes=V7X_VMEM_LIMIT_BYTES)


def _append_aliased(in_specs, args, out_buffers):
    aliases = {}
    for k, buf in enumerate(out_buffers):
        if buf is not None:
            in_specs.append(pl.BlockSpec(memory_space=pl.ANY))
            args.append(buf)
            aliases[len(args) - 1] = k
    return aliases


def _sigmoid(x):
    return 1.0 / (1.0 + jnp.exp(-x))


def _silu(x):
    return x * _sigmoid(x)


def _rmsnorm_kernel(x_ref, g_ref, o_ref):
    x = x_ref[...]
    ms = jnp.mean(x * x, axis=-1, keepdims=True)
    o_ref[...] = (x * lax.rsqrt(ms + EPS) * g_ref[...]).astype(o_ref.dtype)


def _rmsnorm_split_kernel(xp_ref, xs_ref, g_ref, o_ref):
    last = pl.num_programs(0) - 1

    @pl.when(pl.program_id(0) < last)
    def _():
        _rmsnorm_kernel(xp_ref, g_ref, o_ref)

    @pl.when(pl.program_id(0) == last)
    def _():
        _rmsnorm_kernel(xs_ref, g_ref, o_ref)


def _rmsnorm(x, g, tm=544):
    d = g.shape[-1]
    gspec = pl.BlockSpec((1, d), lambda i: (0, 0))
    if isinstance(x, (tuple, list)):
        tm = M_SAMPLE
        last_prompt_tile = M_PROMPT // tm - 1
        return pl.pallas_call(
            _rmsnorm_split_kernel,
            grid=(M_ALL // tm,),
            in_specs=[pl.BlockSpec((tm, d), lambda i: (jnp.minimum(i, last_prompt_tile), 0)),
                      pl.BlockSpec((tm, d), lambda i: (0, 0)), gspec],
            out_specs=pl.BlockSpec((tm, d), lambda i: (i, 0)),
            out_shape=jax.ShapeDtypeStruct((M_ALL, d), BF16),
            compiler_params=_params(("arbitrary",)),
            name="rmsnorm_split",
        )(x[0], x[1], g.reshape(1, d))
    m = x.shape[0]
    return pl.pallas_call(
        _rmsnorm_kernel,
        grid=(m // tm,),
        in_specs=[pl.BlockSpec((tm, d), lambda i: (i, 0)), gspec],
        out_specs=pl.BlockSpec((tm, d), lambda i: (i, 0)),
        out_shape=jax.ShapeDtypeStruct((m, d), BF16),
        compiler_params=_params(("parallel",)),
        name="rmsnorm",
    )(x, g.reshape(1, d))


W_CAST_ROWS = 256


def _cast_weight_tile(w_ref, wx_ref, wb_ref, shift):
    rows = wb_ref.shape[0]
    for r in range(0, rows - shift, W_CAST_ROWS):
        n = min(W_CAST_ROWS, rows - shift - r)
        wb_ref[r:r + n, :] = w_ref[r + shift:r + shift + n, :].astype(BF16)
    if shift:
        wb_ref[rows - shift:, :] = wx_ref[:shift, :].astype(BF16)


def _mm_kernel(*refs, shift, n_add, n_out, w_transposed):
    a_ref, w_ref = refs[0], refs[1]
    pos = 2
    wx_ref = None
    if shift:
        wx_ref = refs[pos]
        pos += 1
    add_refs = refs[pos:pos + n_add]
    pos += n_add
    o_refs = refs[pos:pos + n_out]
    wb_ref = refs[pos + n_out]
    i = pl.program_id(1)

    @pl.when(i == 0)
    def _():
        _cast_weight_tile(w_ref, wx_ref, wb_ref, shift)

    if w_transposed:
        r = _dot_nt(a_ref[...], wb_ref[...])
    else:
        r = jnp.dot(a_ref[...], wb_ref[...], preferred_element_type=F32)

    def finish(add_ref, o_ref):
        res = r if add_ref is None else add_ref[...] + r
        o_ref[...] = res.astype(o_ref.dtype)

    if n_add < 2 and n_out < 2:
        finish(add_refs[0] if n_add else None, o_refs[0])
    else:
        last = pl.num_programs(1) - 1

        @pl.when(i < last)
        def _():
            finish(add_refs[0] if n_add else None, o_refs[0])

        @pl.when(i == last)
        def _():
            finish(add_refs[-1] if n_add else None, o_refs[-1])


def _matmul(a, w, layer, *, n, out_dtype, tm, tn, w_col0=0, w_transposed=False, addend=None, split_out=False,
            name="matmul"):
    m, kdim = a.shape
    add_parts = isinstance(addend, (tuple, list))
    if add_parts or split_out:
        assert tm == M_SAMPLE and M_PROMPT % tm == 0 and m == M_ALL
    last_prompt_tile = M_PROMPT // tm - 1
    wj0 = w_col0 // tn
    shift = w_col0 - wj0 * tn
    extra = LANES
    assert shift == 0 or (w_transposed and shift % SUBLANES == 0 and shift <= extra and tn % extra == 0)
    if w_transposed:
        w_spec = pl.BlockSpec((None, tn, kdim), lambda j, i: (layer, j + wj0, 0))
        wb_shape = (tn, kdim)
    else:
        w_spec = pl.BlockSpec((None, kdim, tn), lambda j, i: (layer, 0, j + wj0))
        wb_shape = (kdim, tn)
    in_specs = [pl.BlockSpec((tm, kdim), lambda j, i: (i, 0)), w_spec]
    args = [a, w]
    if shift:
        in_specs.append(pl.BlockSpec((None, extra, kdim), lambda j, i: (layer, (j + wj0 + 1) * (tn // extra), 0)))
        args.append(w)
    prompt_rows = pl.BlockSpec((tm, tn), lambda j, i: (jnp.minimum(i, last_prompt_tile), j))
    sample_rows = pl.BlockSpec((tm, tn), lambda j, i: (0, j))
    all_rows = pl.BlockSpec((tm, tn), lambda j, i: (i, j))
    if add_parts:
        in_specs += [prompt_rows, sample_rows]
        args += list(addend)
    elif addend is not None:
        in_specs.append(all_rows)
        args.append(addend)
    n_add = 2 if add_parts else int(addend is not None)
    if split_out:
        out_specs = [prompt_rows, sample_rows]
        out_shape = [jax.ShapeDtypeStruct((M_PROMPT, n), out_dtype), jax.ShapeDtypeStruct((M_SAMPLE, n), out_dtype)]
    else:
        out_specs = all_rows
        out_shape = jax.ShapeDtypeStruct((m, n), out_dtype)
    return pl.pallas_call(
        functools.partial(_mm_kernel, shift=shift, n_add=n_add, n_out=2 if split_out else 1,
                          w_transposed=w_transposed),
        grid=(n // tn, m // tm),
        in_specs=in_specs,
        out_specs=out_specs,
        out_shape=out_shape,
        scratch_shapes=[pltpu.VMEM(wb_shape, BF16)],
        compiler_params=_params(("parallel", "arbitrary")),
        name=name,
    )(*args)


def _out_proj_norm_kernel(*refs, n_add, n_prompt_tiles):
    a_ref, w_ref = refs[0], refs[1]
    add_refs = refs[2:2 + n_add]
    g_ref, x_ref, hn_ref, wb_ref = refs[2 + n_add:]
    i = pl.program_id(0)

    @pl.when(i == 0)
    def _():
        _cast_weight_tile(w_ref, None, wb_ref, 0)

    r = jnp.dot(a_ref[...], wb_ref[...], preferred_element_type=F32)

    def finish(add_ref):
        x_ref[...] = add_ref[...] + r
        _rmsnorm_kernel(x_ref, g_ref, hn_ref)

    if n_add == 1:
        finish(add_refs[0])
    else:
        @pl.when(i < n_prompt_tiles)
        def _():
            finish(add_refs[0])

        @pl.when(i >= n_prompt_tiles)
        def _():
            finish(add_refs[1])


def _out_proj_norm(a, w, layer, addend, g, *, tm):
    m, kdim = a.shape
    d = w.shape[-1]
    add_parts = isinstance(addend, (tuple, list))
    assert M_PROMPT % tm == 0 and M_SAMPLE % tm == 0
    n_prompt_tiles = M_PROMPT // tm
    row_tile = lambda index: pl.BlockSpec((tm, d), index)
    in_specs = [pl.BlockSpec((tm, kdim), lambda i: (i, 0)),
                pl.BlockSpec((None, kdim, d), lambda i: (layer, 0, 0), pipeline_mode=pl.Buffered(1))]
    args = [a, w]
    if add_parts:
        in_specs += [row_tile(lambda i: (jnp.minimum(i, n_prompt_tiles - 1), 0)),
                     row_tile(lambda i: (jnp.maximum(i - n_prompt_tiles, 0), 0))]
        args += list(addend)
    else:
        in_specs.append(row_tile(lambda i: (i, 0)))
        args.append(addend)
    in_specs.append(pl.BlockSpec((1, d), lambda i: (0, 0)))
    args.append(g.reshape(1, d))
    return pl.pallas_call(
        functools.partial(_out_proj_norm_kernel, n_add=2 if add_parts else 1, n_prompt_tiles=n_prompt_tiles),
        grid=(m // tm,),
        in_specs=in_specs,
        out_specs=[row_tile(lambda i: (i, 0)), row_tile(lambda i: (i, 0))],
        out_shape=[jax.ShapeDtypeStruct((m, d), F32), jax.ShapeDtypeStruct((m, d), BF16)],
        scratch_shapes=[pltpu.VMEM((kdim, d), BF16)],
        compiler_params=_params(("arbitrary",)),
        name="out_proj_norm",
    )(*args)


def _merge_kernel(a0_ref, a1_ref, a2_ref, w0_ref, w1_ref, w2_ref, g0_ref, g1_ref, g2_ref, o_ref,
                  wb0_ref, wb1_ref, wb2_ref):
    @pl.when(pl.program_id(1) == 0)
    def _():
        for w_ref, wb_ref in ((w0_ref, wb0_ref), (w1_ref, wb1_ref), (w2_ref, wb2_ref)):
            _cast_weight_tile(w_ref, None, wb_ref, 0)

    acc = None
    for a_ref, wb_ref, g_ref in ((a0_ref, wb0_ref, g0_ref), (a1_ref, wb1_ref, g1_ref), (a2_ref, wb2_ref, g2_ref)):
        term = _sigmoid(g_ref[...]) * jnp.dot(a_ref[...], wb_ref[...], preferred_element_type=F32)
        acc = term if acc is None else acc + term
    o_ref[...] = acc.astype(o_ref.dtype)


def _gated_merge(branches, weights, layer, gates, gate_cols, *, tm, tn):
    m, kdim = branches[0].shape
    a_spec = pl.BlockSpec((tm, kdim), lambda j, i: (i, 0))
    w_spec = pl.BlockSpec((None, kdim, tn), lambda j, i: (layer, 0, j), pipeline_mode=pl.Buffered(1))
    g_specs = [pl.BlockSpec((tm, tn), functools.partial(lambda j, i, j0: (i, j + j0), j0=col // tn))
               for col in gate_cols]
    return pl.pallas_call(
        _merge_kernel,
        grid=(D_MODEL // tn, m // tm),
        in_specs=[a_spec] * 3 + [w_spec] * 3 + g_specs,
        out_specs=pl.BlockSpec((tm, tn), lambda j, i: (i, j)),
        out_shape=jax.ShapeDtypeStruct((m, D_MODEL), BF16),
        scratch_shapes=[pltpu.VMEM((kdim, tn), BF16)] * 3,
        compiler_params=_params(("parallel", "arbitrary")),
        name="gated_merge",
    )(*branches, *weights, gates, gates, gates)


def _swiglu_kernel(a_ref, wa_ref, wc_ref, o_ref, wab_ref, wcb_ref):
    @pl.when(pl.program_id(1) == 0)
    def _():
        _cast_weight_tile(wa_ref, None, wab_ref, 0)
        _cast_weight_tile(wc_ref, None, wcb_ref, 0)

    a = a_ref[...]
    fa = jnp.dot(a, wab_ref[...], preferred_element_type=F32)
    fc = jnp.dot(a, wcb_ref[...], preferred_element_type=F32)
    o_ref[...] = (_silu(fa) * fc).astype(o_ref.dtype)


def _swiglu_matmul(a, w, layer, *, tm, tn):
    m, kdim = a.shape
    nj = FFN_HIDDEN // tn
    return pl.pallas_call(
        _swiglu_kernel,
        grid=(nj, m // tm),
        in_specs=[pl.BlockSpec((tm, kdim), lambda j, i: (i, 0)),
                  pl.BlockSpec((None, kdim, tn), lambda j, i: (layer, 0, j)),
                  pl.BlockSpec((None, kdim, tn), lambda j, i: (layer, 0, j + nj))],
        out_specs=pl.BlockSpec((tm, tn), lambda j, i: (i, j)),
        out_shape=jax.ShapeDtypeStruct((m, FFN_HIDDEN), BF16),
        scratch_shapes=[pltpu.VMEM((kdim, tn), BF16), pltpu.VMEM((kdim, tn), BF16)],
        compiler_params=_params(("parallel", "arbitrary")),
        name="ffn_in_swiglu",
    )(a, w, w)


def _split3(a):
    hi = a.astype(BF16)
    r1 = a - hi.astype(F32)
    mid = r1.astype(BF16)
    lo = (r1 - mid.astype(F32)).astype(BF16)
    return hi, mid, lo


def _exact_right(a, onehot):
    return sum(jnp.dot(p, onehot, preferred_element_type=F32) for p in _split3(a))


def _exact_right_many(arrays, onehot):
    n = arrays[0].shape[0]
    r = jnp.dot(jnp.concatenate([p for a in arrays for p in _split3(a)], axis=0), onehot,
                preferred_element_type=F32)
    return [r[3 * k * n:(3 * k + 1) * n] + r[(3 * k + 1) * n:(3 * k + 2) * n] + r[(3 * k + 2) * n:(3 * k + 3) * n]
            for k in range(len(arrays))]


def _exact_left(onehot, b):
    return sum(jnp.dot(onehot, p, preferred_element_type=F32) for p in _split3(b))


def _dot_nt(a, b):
    return lax.dot_general(a, b, (((1,), (1,)), ((), ())), preferred_element_type=F32)


def _dot_tn(a, b):
    return lax.dot_general(a, b, (((0,), (0,)), ((), ())), preferred_element_type=F32)


def _conv_silu(ext_ref, cw, cb, L):
    acc = ext_ref[SUBLANES:SUBLANES + L, :] * cw[CONV_W - 1:CONV_W, :] + cb
    for s in range(1, CONV_W):
        acc = acc + ext_ref[SUBLANES - s:SUBLANES - s + L, :] * cw[CONV_W - 1 - s:CONV_W - s, :]
    return _silu(acc)


def _ssd_kernel(x_ref, b_ref, c_ref, z_ref, dt_ref, hist_ref, h0_ref,
                cwx_ref, cwb_ref, cwc_ref, cbx_ref, cbb_ref, cbc_ref,
                dtb_ref, alog_ref, dskip_ref, normw_ref, exp_p_ref, exp_l_ref,
                *rest, L, nc, n_aliased):
    y_ref, hout_ref, ex_ref, eb_ref, ec_ref, ht_ref = rest[n_aliased:]
    c = pl.program_id(1)
    lw = SSM_HEADS * L

    @pl.when(c == 0)
    def _():
        hist = hist_ref[0]
        ex_ref[0:SUBLANES, :] = hist[:, :D_SSM]
        eb_ref[0:SUBLANES, :] = hist[:, D_SSM:D_SSM + BC_WIDTH]
        ec_ref[0:SUBLANES, :] = hist[:, D_SSM + BC_WIDTH:]
        ht_ref[...] = h0_ref[0].T

    @pl.when(c > 0)
    def _():
        ex_ref[0:SUBLANES, :] = ex_ref[L:L + SUBLANES, :]
        eb_ref[0:SUBLANES, :] = eb_ref[L:L + SUBLANES, :]
        ec_ref[0:SUBLANES, :] = ec_ref[L:L + SUBLANES, :]

    ex_ref[SUBLANES:SUBLANES + L, :] = x_ref[...]
    eb_ref[SUBLANES:SUBLANES + L, :] = b_ref[...]
    ec_ref[SUBLANES:SUBLANES + L, :] = c_ref[...]
    xs = _conv_silu(ex_ref, cwx_ref[...], cbx_ref[...], L)
    bm = _conv_silu(eb_ref, cwb_ref[...], cbb_ref[...], L).astype(BF16)
    cm = _conv_silu(ec_ref, cwc_ref[...], cbc_ref[...], L).astype(BF16)

    dtr = dt_ref[...] + dtb_ref[...]
    dt = jnp.maximum(dtr, 0.0) + jnp.log1p(jnp.exp(-jnp.abs(dtr)))

    tri = (lax.broadcasted_iota(jnp.int32, (L, L), 1)
           <= lax.broadcasted_iota(jnp.int32, (L, L), 0)).astype(BF16)
    acum = _exact_left(tri, dt * (-jnp.exp(alog_ref[...])))

    if L == SSM_HEAD_DIM:
        dt_p = _exact_right(dt, exp_p_ref[...])
        acum_p = _exact_right(acum, exp_p_ref[...])
    else:
        dt_p, acum_p = _exact_right_many([dt, acum], exp_p_ref[...])
    last_p = acum_p[L - 1:L, :]
    if L == SSM_HEAD_DIM:
        dt_l, acum_l = dt_p, acum_p
    else:
        dt_l, acum_l = _exact_right_many([dt, acum], exp_l_ref[...])

    rowi = lax.broadcasted_iota(jnp.int32, (L, lw), 0)
    colj = lax.broadcasted_iota(jnp.int32, (L, lw), 1) % L
    diag = colj == rowi
    causal = colj <= rowi
    acum_row = jnp.sum(jnp.where(diag, acum_l, 0.0), axis=0, keepdims=True)
    dt_row = jnp.sum(jnp.where(diag, dt_l, 0.0), axis=0, keepdims=True)
    decay = jnp.where(causal, jnp.exp(jnp.where(causal, acum_l - acum_row, 0.0)), 0.0)

    heads_per_group = SSM_HEADS // SSM_GROUPS
    cb = jnp.concatenate(
        [_dot_nt(cm[:, g * SSM_STATE:(g + 1) * SSM_STATE],
                 jnp.concatenate([bm[:, g * SSM_STATE:(g + 1) * SSM_STATE]] * heads_per_group, axis=0))
         for g in range(SSM_GROUPS)], axis=1)

    ht = ht_ref[...]
    ht_b = ht.astype(BF16)
    to_end = jnp.exp(last_p - acum_p) * dt_p
    xw = (xs * to_end).astype(BF16)
    y_inter = []
    st_parts = []
    for g in range(SSM_GROUPS):
        gs = slice(g * SSM_GROUP_WIDTH, (g + 1) * SSM_GROUP_WIDTH)
        ns = slice(g * SSM_STATE, (g + 1) * SSM_STATE)
        y_inter.append(jnp.dot(cm[:, ns], ht_b[:, gs], preferred_element_type=F32))
        st_parts.append(_dot_tn(bm[:, ns], xw[:, gs]))
    ht_new = jnp.exp(last_p) * ht + jnp.concatenate(st_parts, axis=1)
    ht_ref[...] = ht_new

    m_all = (cb * decay * dt_row).astype(BF16)
    hp = LANES // L
    slab = hp * SSM_HEAD_DIM
    xs_b = xs.astype(BF16)
    lane_head = lax.broadcasted_iota(jnp.int32, (L, slab), 1) // SSM_HEAD_DIM
    y_parts = []
    for q in range(lw // LANES):
        xq = xs_b[:, q * slab:(q + 1) * slab]
        bd = jnp.concatenate([jnp.where(lane_head == h, xq, jnp.zeros_like(xq)) for h in range(hp)], axis=0)
        y_parts.append(jnp.dot(m_all[:, q * LANES:(q + 1) * LANES], bd, preferred_element_type=F32))
    y = (jnp.concatenate(y_parts, axis=1) + jnp.concatenate(y_inter, axis=1) * jnp.exp(acum_p)
         + dskip_ref[...] * xs)

    t = y * _silu(z_ref[...])
    outs = []
    for g in range(SSM_GROUPS):
        tg = t[:, g * SSM_GROUP_WIDTH:(g + 1) * SSM_GROUP_WIDTH]
        outs.append(tg * lax.rsqrt(jnp.mean(tg * tg, axis=-1, keepdims=True) + EPS))
    y_ref[...] = (jnp.concatenate(outs, axis=1) * normw_ref[...]).astype(y_ref.dtype)

    @pl.when(c == nc - 1)
    def _():
        hout_ref[0] = ht_new.T


def _ssd_call(u, dt_raw, hist8, h0, layer, lp, *, L, nb, nc, row0, out_layer, prev_out, prev_state):
    rb0 = row0 // L

    def rows(width, colblock):
        return pl.BlockSpec((L, width), lambda b, c: (rb0 + b * nc + c, colblock))

    def const(shape):
        return pl.BlockSpec(shape, lambda b, c: tuple(0 for _ in shape))

    lw = SSM_HEADS * L
    in_specs = [rows(D_SSM, H_X // D_SSM), rows(BC_WIDTH, H_B // BC_WIDTH), rows(BC_WIDTH, H_C // BC_WIDTH),
                rows(D_SSM, H_Z // D_SSM),
                pl.BlockSpec((L, LANES), lambda b, c: (rb0 + b * nc + c, 0)),
                pl.BlockSpec((None, 1, SUBLANES, CONV_DIM), lambda b, c: (layer, b, 0, 0)),
                pl.BlockSpec((None, 1, D_SSM, SSM_STATE), lambda b, c: (layer, b, 0, 0)),
                const((CONV_W, D_SSM)), const((CONV_W, BC_WIDTH)), const((CONV_W, BC_WIDTH)),
                const((1, D_SSM)), const((1, BC_WIDTH)), const((1, BC_WIDTH)),
                const((1, LANES)), const((1, LANES)), const((1, D_SSM)), const((1, D_SSM)),
                const((LANES, D_SSM)), const((LANES, lw))]
    args = [u, u, u, u, dt_raw, hist8, h0,
            lp["conv_wx"], lp["conv_wb"], lp["conv_wc"], lp["conv_bx"], lp["conv_bb"], lp["conv_bc"],
            lp["dt_bias"], lp["a_log"], lp["d_skip_p"], lp["ssm_norm"],
            _expand_onehot(SSM_HEAD_DIM), _expand_onehot(L)]
    aliases = _append_aliased(in_specs, args, (prev_out, prev_state))
    y, hout = pl.pallas_call(
        functools.partial(_ssd_kernel, L=L, nc=nc, n_aliased=len(aliases)),
        grid=(nb, nc),
        in_specs=in_specs,
        out_specs=[pl.BlockSpec((L, D_SSM), lambda b, c: (rb0 + b * nc + c, 0)),
                   pl.BlockSpec((None, 1, D_SSM, SSM_STATE), lambda b, c: (out_layer, b, 0, 0))],
        out_shape=[jax.ShapeDtypeStruct((M_ALL, D_SSM), BF16),
                   jax.ShapeDtypeStruct((DEPTH, nb, D_SSM, SSM_STATE), F32)],
        scratch_shapes=[pltpu.VMEM((SUBLANES + L, D_SSM), F32), pltpu.VMEM((SUBLANES + L, BC_WIDTH), F32),
                        pltpu.VMEM((SUBLANES + L, BC_WIDTH), F32), pltpu.VMEM((SSM_STATE, D_SSM), F32)],
        input_output_aliases=aliases,
        compiler_params=_params(("parallel", "arbitrary")),
        name="ssd_L%d" % L,
    )(*args)
    return y, hout


def _expand_onehot(per_head):
    e = np.zeros((LANES, SSM_HEADS * per_head), np.float32)
    for h in range(SSM_HEADS):
        e[h, h * per_head:(h + 1) * per_head] = 1.0
    return jnp.asarray(e, BF16)


def _ret_consts(L):
    lam = np.log1p(-np.exp2(-5.0 - np.arange(RET_HEADS, dtype=np.float64)))
    i = np.arange(L, dtype=np.float64)
    diff = i[:, None] - i[None, :]
    dmat = np.where(diff >= 0, np.exp(lam[:, None, None] * np.maximum(diff, 0.0)), 0.0)
    to_end = np.exp(lam[:, None] * (L - 1 - i)[None, :])
    q_dec = np.exp(lam[:, None] * (i + 1.0)[None, :])
    chunk_decay = np.exp(lam * L)
    expand = lambda t: np.repeat(t.T[:, :, None], RET_DK, axis=2).reshape(L, RET_HEADS * RET_DK)
    return (jnp.asarray(dmat, F32), jnp.asarray(expand(q_dec), F32), jnp.asarray(expand(to_end), F32),
            [float(v) for v in chunk_decay])


RET_HEADS_AHEAD = 8
RET_CHUNKS_PER_STEP = 8


def _ret_kernel(q_ref, k_ref, v_ref, g_ref, cos_ref, sin_ref, qdec_ref, kend_ref, dmat_ref, s0_ref,
                *rest, L, cps, chunk_decay, n_aliased):
    y_ref, sout_ref, s_ref = rest[n_aliased:]
    step = pl.program_id(1)

    @pl.when(step == 0)
    def _():
        s_ref[...] = s0_ref[0]

    half = RET_DK // 2

    def one_chunk(sub, carry):
        rows = slice(0, L) if cps == 1 else pl.ds(pl.multiple_of(sub * L, L), L)
        cos = cos_ref[rows, :]
        sin = sin_ref[rows, :]

        def first_stage(h):
            ks = slice(h * RET_DK, (h + 1) * RET_DK)
            qh = q_ref[rows, ks]
            kh = k_ref[rows, ks]
            qr = qh * cos + pltpu.roll(qh, half, 1) * sin
            kr = (kh * cos + pltpu.roll(kh, half, 1) * sin) * (RET_DK ** -0.5)
            vh = v_ref[rows, h * RET_DV:(h + 1) * RET_DV].astype(BF16)
            s_prev = s_ref[h]
            qk = _dot_nt(qr.astype(BF16), kr.astype(BF16))
            o_state = jnp.dot((qr * qdec_ref[:, ks]).astype(BF16), s_prev.astype(BF16),
                              preferred_element_type=F32)
            kv = _dot_tn((kr * kend_ref[:, ks]).astype(BF16), vh)
            s_ref[h] = chunk_decay[h] * s_prev + kv
            return qk, o_state, vh

        def second_stage(h, qk, o_state, vh):
            vs = slice(h * RET_DV, (h + 1) * RET_DV)
            s = qk * dmat_ref[h]
            o = jnp.dot(s.astype(BF16), vh, preferred_element_type=F32) + o_state
            o = o * lax.rsqrt(jnp.mean(o * o, axis=-1, keepdims=True) + EPS)
            y_ref[rows, vs] = (o * _silu(g_ref[rows, vs])).astype(y_ref.dtype)

        pending = [first_stage(h) for h in range(RET_HEADS_AHEAD)]
        for h in range(RET_HEADS):
            if h + RET_HEADS_AHEAD < RET_HEADS:
                pending.append(first_stage(h + RET_HEADS_AHEAD))
            second_stage(h, *pending.pop(0))
        return carry

    if cps == 1:
        one_chunk(0, 0)
    else:
        lax.fori_loop(0, cps, one_chunk, 0)

    @pl.when(step == pl.num_programs(1) - 1)
    def _():
        sout_ref[0] = s_ref[...]


def _ret_call(u, cos_t, sin_t, s0, layer, *, L, nb, nc, row0, out_layer, prev_out, prev_state):
    cps = RET_CHUNKS_PER_STEP if nc % RET_CHUNKS_PER_STEP == 0 else 1
    blk = cps * L
    nsteps = nc // cps
    rb0 = row0 // blk
    dmat, qdec, kend, chunk_decay = _ret_consts(L)
    hk = RET_HEADS * RET_DK
    hv = RET_HEADS * RET_DV

    def rows(width, colblock):
        return pl.BlockSpec((blk, width), lambda b, c: (rb0 + b * nsteps + c, colblock))

    in_specs = [rows(hk, T_RQ // hk), rows(hk, T_RK // hk), rows(hv, T_RV // hv), rows(hv, T_RG // hv),
                pl.BlockSpec((blk, RET_DK), lambda b, c: (c, 0)),
                pl.BlockSpec((blk, RET_DK), lambda b, c: (c, 0)),
                pl.BlockSpec((L, hk), lambda b, c: (0, 0)),
                pl.BlockSpec((L, hk), lambda b, c: (0, 0)),
                pl.BlockSpec((RET_HEADS, L, L), lambda b, c: (0, 0, 0)),
                pl.BlockSpec((None, 1, RET_HEADS, RET_DK, RET_DV), lambda b, c: (layer, b, 0, 0, 0))]
    args = [u, u, u, u, cos_t, sin_t, qdec, kend, dmat, s0]
    aliases = _append_aliased(in_specs, args, (prev_out, prev_state))
    y, sout = pl.pallas_call(
        functools.partial(_ret_kernel, L=L, cps=cps, chunk_decay=chunk_decay, n_aliased=len(aliases)),
        grid=(nb, nsteps),
        in_specs=in_specs,
        out_specs=[pl.BlockSpec((blk, hv), lambda b, c: (rb0 + b * nsteps + c, 0)),
                   pl.BlockSpec((None, 1, RET_HEADS, RET_DK, RET_DV), lambda b, c: (out_layer, b, 0, 0, 0))],
        out_shape=[jax.ShapeDtypeStruct((M_ALL, hv), BF16),
                   jax.ShapeDtypeStruct((DEPTH, nb, RET_HEADS, RET_DK, RET_DV), F32)],
        scratch_shapes=[pltpu.VMEM((RET_HEADS, RET_DK, RET_DV), F32)],
        input_output_aliases=aliases,
        compiler_params=_params(("parallel", "arbitrary")),
        name="retention_L%d" % L,
    )(*args)
    return y, sout


def _rope_tables(pos):
    half = RET_DK // 2
    inv = ROPE_BASE ** (-jnp.arange(half, dtype=F32) * 2.0 / RET_DK)
    ang = pos.astype(F32)[:, None] * inv[None, :]
    cos = jnp.cos(ang)
    sin = jnp.sin(ang)
    return jnp.concatenate([cos, cos], axis=1), jnp.concatenate([-sin, sin], axis=1)


def _head_norm(xh, g):
    return xh * lax.rsqrt(jnp.mean(xh * xh, axis=-1, keepdims=True) + EPS) * g


def _kv_prep_kernel(k_ref, v_ref, gk_ref, knf_ref, knb_ref, vb_ref):
    for h in range(ATT_HEADS):
        hs = slice(h * ATT_HEAD_DIM, (h + 1) * ATT_HEAD_DIM)
        kn = _head_norm(k_ref[:, hs], gk_ref[...])
        knf_ref[0, :, hs] = kn
        knb_ref[0, :, hs] = kn.astype(BF16)
    vb_ref[0] = v_ref[...].astype(BF16)


def _kv_prep_sample(u, gk):
    dm = ATT_HEADS * ATT_HEAD_DIM
    rb = M_PROMPT // M_SAMPLE
    return pl.pallas_call(
        _kv_prep_kernel,
        grid=(1,),
        in_specs=[pl.BlockSpec((M_SAMPLE, dm), lambda i: (rb, T_AK // dm)),
                  pl.BlockSpec((M_SAMPLE, dm), lambda i: (rb, T_AV // dm)),
                  pl.BlockSpec((1, ATT_HEAD_DIM), lambda i: (0, 0))],
        out_specs=[pl.BlockSpec((1, M_SAMPLE, dm), lambda i: (0, 0, 0)),
                   pl.BlockSpec((1, M_SAMPLE, dm), lambda i: (0, 0, 0)),
                   pl.BlockSpec((1, M_SAMPLE, dm), lambda i: (0, 0, 0))],
        out_shape=[jax.ShapeDtypeStruct((1, M_SAMPLE, dm), F32),
                   jax.ShapeDtypeStruct((1, M_SAMPLE, dm), BF16),
                   jax.ShapeDtypeStruct((1, M_SAMPLE, dm), BF16)],
        compiler_params=_params(("arbitrary",)),
        name="kv_prep_sample",
    )(u, u, gk)


def _scores(qn, k1, k2, bias1, bias2, valid1):
    scale = ATT_HEAD_DIM ** -0.5
    s1 = _dot_nt(qn, k1) * scale + bias1
    if valid1 is not None:
        s1 = jnp.where(valid1, s1, NEG_INF)
    s2 = _dot_nt(qn, k2) * scale + bias2
    return s1, s2


def _softmax_pv(s1, s2, v1, v2):
    m = jnp.maximum(jnp.max(s1, axis=-1, keepdims=True), jnp.max(s2, axis=-1, keepdims=True))
    p1 = jnp.exp(s1 - m)
    p2 = jnp.exp(s2 - m)
    inv = 1.0 / (jnp.sum(p1, axis=-1, keepdims=True) + jnp.sum(p2, axis=-1, keepdims=True))
    return (jnp.dot((p1 * inv).astype(BF16), v1, preferred_element_type=F32)
            + jnp.dot((p2 * inv).astype(BF16), v2, preferred_element_type=F32))


ATTN_HEADS_AHEAD = 16


ATTN_CHUNKS_PER_STEP = 4


def _attn_prompt_kernel(q_ref, k_ref, v_ref, b1_ref, b2_ref, gq_ref, gk_ref, y_ref, knf_ref, kb_ref, vb_ref):
    step = pl.program_id(1)
    heads = [slice(h * ATT_HEAD_DIM, (h + 1) * ATT_HEAD_DIM) for h in range(ATT_HEADS)]

    @pl.when(step == 0)
    def _():
        kb_ref[0:WINDOW, :] = jnp.zeros((WINDOW, kb_ref.shape[1]), BF16)
        vb_ref[0:WINDOW, :] = jnp.zeros((WINDOW, vb_ref.shape[1]), BF16)

    def one_chunk(sub, carry):
        c = step * ATTN_CHUNKS_PER_STEP + sub
        rows = pl.ds(pl.multiple_of(sub * CHUNK, CHUNK), CHUNK)
        base = pl.multiple_of(c * CHUNK, CHUNK)
        own = pl.multiple_of(c * CHUNK + WINDOW, CHUNK)

        for hs in heads:
            kn = _head_norm(k_ref[rows, hs], gk_ref[...])
            knf_ref[0, rows, hs] = kn
            kb_ref[pl.ds(own, CHUNK), hs] = kn.astype(BF16)
        vb_ref[pl.ds(own, CHUNK), :] = v_ref[rows, :].astype(BF16)

        col = lax.broadcasted_iota(jnp.int32, (CHUNK, WINDOW), 1)
        valid1 = col + c * CHUNK >= WINDOW

        def scores(h):
            hs = heads[h]
            qn = _head_norm(q_ref[rows, hs], gq_ref[...]).astype(BF16)
            return _scores(qn, kb_ref[pl.ds(base, WINDOW), hs], kb_ref[pl.ds(own, CHUNK), hs],
                           b1_ref[h], b2_ref[h], valid1)

        pending = [scores(h) for h in range(ATTN_HEADS_AHEAD)]
        for h in range(ATT_HEADS):
            if h + ATTN_HEADS_AHEAD < ATT_HEADS:
                pending.append(scores(h + ATTN_HEADS_AHEAD))
            s1, s2 = pending.pop(0)
            hs = heads[h]
            o = _softmax_pv(s1, s2, vb_ref[pl.ds(base, WINDOW), hs], vb_ref[pl.ds(own, CHUNK), hs])
            y_ref[rows, hs] = o.astype(y_ref.dtype)
        return carry

    lax.fori_loop(0, ATTN_CHUNKS_PER_STEP, one_chunk, 0)


def _attn_prompt_call(u, bias1, bias2, gq, gk):
    dm = ATT_HEADS * ATT_HEAD_DIM
    blk = ATTN_CHUNKS_PER_STEP * CHUNK
    nsteps = SEQ // blk
    keep = min(WINDOW, SEQ)
    assert SEQ % blk == 0 and keep % blk == 0
    first_kept = nsteps - keep // blk

    def rows(col0):
        return pl.BlockSpec((blk, dm), lambda b, c: (b * nsteps + c, col0 // dm))

    return pl.pallas_call(
        _attn_prompt_kernel,
        grid=(BATCH, nsteps),
        in_specs=[rows(T_AQ), rows(T_AK), rows(T_AV),
                  pl.BlockSpec((ATT_HEADS, CHUNK, WINDOW), lambda b, c: (0, 0, 0)),
                  pl.BlockSpec((ATT_HEADS, CHUNK, CHUNK), lambda b, c: (0, 0, 0)),
                  pl.BlockSpec((1, ATT_HEAD_DIM), lambda b, c: (0, 0)),
                  pl.BlockSpec((1, ATT_HEAD_DIM), lambda b, c: (0, 0))],
        out_specs=[pl.BlockSpec((blk, dm), lambda b, c: (b * nsteps + c, 0)),
                   pl.BlockSpec((1, blk, dm), lambda b, c: (b, jnp.maximum(c - first_kept, 0), 0))],
        out_shape=[jax.ShapeDtypeStruct((M_ALL, dm), BF16),
                   jax.ShapeDtypeStruct((BATCH, keep, dm), F32)],
        scratch_shapes=[pltpu.VMEM((SEQ + WINDOW, dm), BF16), pltpu.VMEM((SEQ + WINDOW, dm), BF16)],
        compiler_params=_params(("parallel", "arbitrary")),
        name="attn_prompt",
    )(u, u, u, bias1, bias2, gq, gk)


ATTN_SAMPLE_ROW_GROUPS = 2


def _attn_sample_kernel(q_ref, kh_ref, vh_ref, kn_ref, vn_ref, b1_ref, b2_ref, gq_ref, prev_ref, y_ref):
    del prev_ref
    heads = [slice(h * ATT_HEAD_DIM, (h + 1) * ATT_HEAD_DIM) for h in range(ATT_HEADS)]
    hist = kh_ref.shape[1]
    qall = jnp.concatenate([_head_norm(q_ref[:, hs], gq_ref[...]).astype(BF16) for hs in heads], axis=0)
    k1 = kh_ref[0].reshape(hist * ATT_HEADS, ATT_HEAD_DIM).astype(BF16)
    v1 = vh_ref[0].reshape(hist * ATT_HEADS, ATT_HEAD_DIM).astype(BF16)
    k2 = jnp.concatenate([kn_ref[:, hs] for hs in heads], axis=0)
    v2 = jnp.concatenate([vn_ref[:, hs] for hs in heads], axis=0)
    nrows = qall.shape[0]
    step = nrows // ATTN_SAMPLE_ROW_GROUPS
    groups = [slice(r, r + step) for r in range(0, nrows, step)]
    sc = [_scores(qall[rs], k1, k2, b1_ref[rs, :], b2_ref[rs, :], None) for rs in groups]
    o = jnp.concatenate([_softmax_pv(s1, s2, v1, v2) for s1, s2 in sc], axis=0)
    for h, hs in enumerate(heads):
        y_ref[:, hs] = o[h * DEC_SEQ:(h + 1) * DEC_SEQ].astype(y_ref.dtype)


def _bias_expand_kernel(t1_ref, t2_ref, o1_ref, o2_ref):
    rows, cb = o1_ref.shape
    row_head = lax.broadcasted_iota(jnp.int32, (rows, cb), 0) // DEC_SEQ
    col = lax.broadcasted_iota(jnp.int32, (rows, cb), 1)
    ej = lax.broadcasted_iota(jnp.int32, (LANES, cb), 0)
    ec = lax.broadcasted_iota(jnp.int32, (LANES, cb), 1)
    e1 = jnp.where(ec // ATT_HEADS == ej, 1.0, 0.0).astype(BF16)
    o1_ref[...] = jnp.where(col % ATT_HEADS == row_head, _exact_right(t1_ref[...], e1), NEG_INF)
    n2 = o2_ref.shape[1]
    row_head2 = lax.broadcasted_iota(jnp.int32, (rows, n2), 0) // DEC_SEQ
    col2 = lax.broadcasted_iota(jnp.int32, (rows, n2), 1)
    ej2 = lax.broadcasted_iota(jnp.int32, (LANES, n2), 0)
    ec2 = lax.broadcasted_iota(jnp.int32, (LANES, n2), 1)
    e2 = jnp.where(ec2 % DEC_SEQ == ej2, 1.0, 0.0).astype(BF16)
    o2_ref[...] = jnp.where(col2 // DEC_SEQ == row_head2, _exact_right(t2_ref[...], e2), NEG_INF)


def _head_masked_bias(tbl1, tbl2):
    nh, nq, hist = tbl1.shape
    rows = nh * nq
    t1 = tbl1.reshape(rows, hist)
    t2 = jnp.pad(tbl2.reshape(rows, nq), ((0, 0), (0, LANES - nq)))
    cb = LANES * nh
    return pl.pallas_call(
        _bias_expand_kernel,
        grid=(hist // LANES,),
        in_specs=[pl.BlockSpec((rows, LANES), lambda j: (0, j)),
                  pl.BlockSpec((rows, LANES), lambda j: (0, 0))],
        out_specs=[pl.BlockSpec((rows, cb), lambda j: (0, j)),
                   pl.BlockSpec((rows, nh * nq), lambda j: (0, 0))],
        out_shape=[jax.ShapeDtypeStruct((rows, hist * nh), F32),
                   jax.ShapeDtypeStruct((rows, nh * nq), F32)],
        compiler_params=_params(("arbitrary",)),
        name="bias_expand",
    )(t1, t2)


def _attn_sample_call(u, k_hist, v_hist, layer, kn_new, vn_new, bias1, bias2, gq, prev_out):
    dm = ATT_HEADS * ATT_HEAD_DIM
    rb0 = M_PROMPT // DEC_SEQ
    hist = k_hist.shape[2]
    cache_spec = pl.BlockSpec((None, 1, hist, ATT_HEADS, ATT_HEAD_DIM), lambda b: (layer, b, 0, 0, 0))
    return pl.pallas_call(
        _attn_sample_kernel,
        grid=(DEC_BATCH,),
        in_specs=[pl.BlockSpec((DEC_SEQ, dm), lambda b: (rb0 + b, T_AQ // dm)),
                  cache_spec,
                  cache_spec,
                  pl.BlockSpec((DEC_SEQ, dm), lambda b: (b, 0)),
                  pl.BlockSpec((DEC_SEQ, dm), lambda b: (b, 0)),
                  pl.BlockSpec((ATT_HEADS * DEC_SEQ, hist * ATT_HEADS), lambda b: (0, 0),
                               pipeline_mode=pl.Buffered(1)),
                  pl.BlockSpec((ATT_HEADS * DEC_SEQ, ATT_HEADS * DEC_SEQ), lambda b: (0, 0)),
                  pl.BlockSpec((1, ATT_HEAD_DIM), lambda b: (0, 0)),
                  pl.BlockSpec(memory_space=pl.ANY)],
        out_specs=pl.BlockSpec((DEC_SEQ, dm), lambda b: (rb0 + b, 0)),
        out_shape=jax.ShapeDtypeStruct((M_ALL, dm), BF16),
        input_output_aliases={8: 0},
        compiler_params=_params(("arbitrary",)),
        name="attn_sample",
    )(u, k_hist, v_hist, kn_new, vn_new, bias1, bias2, gq, prev_out)


def _bias_table(rel_bias):
    kbn = WINDOW + CHUNK
    period = kbn + CHUNK
    m = np.arange(period)
    d = np.where(m < kbn, m, m - period)
    idx = np.clip(WINDOW - d, -REL_CLIP, REL_CLIP) + REL_CLIP
    vec = rel_bias[:, idx]
    rows = jnp.tile(vec, (1, CHUNK))[:, :CHUNK * (period - 1)].reshape(ATT_HEADS, CHUNK, period - 1)
    return rows[:, :, :kbn]


def _layer_params(l, norm_mix, w_in, conv_w, conv_b, dt_bias, a_log, d_skip, ssm_norm, q_norm, k_norm,
                  rel_bias, w_br_ssm, w_br_ret, w_br_att, w_out, norm_ffn, w_ffn_in, w_ffn_out):
    assert w_in.shape[-1] == N_IN
    cw = conv_w[l]
    cb = conv_b[l].reshape(1, CONV_DIM)
    lp = {
        "norm_mix": norm_mix[l], "norm_ffn": norm_ffn[l],
        "conv_wx": cw[:, :D_SSM], "conv_wb": cw[:, D_SSM:D_SSM + BC_WIDTH], "conv_wc": cw[:, D_SSM + BC_WIDTH:],
        "conv_bx": cb[:, :D_SSM], "conv_bb": cb[:, D_SSM:D_SSM + BC_WIDTH], "conv_bc": cb[:, D_SSM + BC_WIDTH:],
        "dt_bias": jnp.pad(dt_bias[l], (0, LANES - SSM_HEADS)).reshape(1, LANES),
        "a_log": jnp.pad(a_log[l], (0, LANES - SSM_HEADS)).reshape(1, LANES),
        "d_skip_p": jnp.repeat(d_skip[l], SSM_HEAD_DIM).reshape(1, D_SSM),
        "ssm_norm": ssm_norm[l].reshape(1, D_SSM),
        "q_norm": q_norm[l].reshape(1, ATT_HEAD_DIM), "k_norm": k_norm[l].reshape(1, ATT_HEAD_DIM),
        "rel_bias": rel_bias[l],
    }
    return lp


def _row_block(t, row0, rows, col0, width):
    return lax.slice(t, (row0, col0), (row0 + rows, col0 + width))


def _layer(x, l, lp, w, state, tables, prev):
    cache_k, cache_v, state_ret, state_ssm4, hist8 = state
    hn = _rmsnorm(x, lp["norm_mix"])
    tm, tn = DENSE_TILES["in_proj"]
    u_head = _matmul(hn, w["w_in_t"], l, n=HEAD_WIDTH, out_dtype=F32, tm=tm, tn=tn, w_transposed=True,
                     name="in_proj_head")
    dt_raw = _matmul(hn, w["w_in_t"], l, n=LANES, out_dtype=F32, tm=tm, tn=LANES, w_col0=DT_COL0,
                     w_transposed=True, name="in_proj_dt")
    u_tail = _matmul(hn, w["w_in_t"], l, n=TAIL_WIDTH, out_dtype=F32, tm=tm, tn=tn, w_col0=TAIL_COL0,
                     w_transposed=True, name="in_proj_tail")

    ys, ssm_p = _ssd_call(u_head, dt_raw, tables["zeros_hist"], tables["zeros_h"], 0, lp, L=CHUNK, nb=BATCH,
                          nc=SEQ // CHUNK, row0=0, out_layer=l, prev_out=None, prev_state=prev["ssm_p"])
    ys, ssm_s = _ssd_call(u_head, dt_raw, hist8, state_ssm4, l, lp, L=DEC_SEQ, nb=DEC_BATCH, nc=1,
                          row0=M_PROMPT, out_layer=l, prev_out=ys, prev_state=prev["ssm_s"])

    yr, ret_p = _ret_call(u_tail, tables["cos_p"], tables["sin_p"], tables["zeros_s"], 0, L=CHUNK, nb=BATCH,
                          nc=SEQ // CHUNK, row0=0, out_layer=l, prev_out=None, prev_state=prev["ret_p"])
    yr, ret_s = _ret_call(u_tail, tables["cos_s"], tables["sin_s"], state_ret, l, L=DEC_SEQ, nb=DEC_BATCH,
                          nc=1, row0=M_PROMPT, out_layer=l, prev_out=yr, prev_state=prev["ret_s"])

    dm = ATT_HEADS * ATT_HEAD_DIM
    knf_s, knb_s, vb_s = _kv_prep_sample(u_tail, lp["k_norm"])
    hist = cache_k.shape[2]
    assert hist == WINDOW
    tbl = _bias_table(lp["rel_bias"])
    ya, knf_p = _attn_prompt_call(u_tail, tbl[:, :, :WINDOW], tbl[:, :, WINDOW:], lp["q_norm"], lp["k_norm"])
    bias1_s, bias2_s = _head_masked_bias(tbl[:, :DEC_SEQ, :hist], tbl[:, :DEC_SEQ, hist:hist + DEC_SEQ])
    ya = _attn_sample_call(u_tail, cache_k, cache_v, l, knb_s[0], vb_s[0], bias1_s, bias2_s, lp["q_norm"], ya)

    tm, tn = DENSE_TILES["merge"]
    mm = _gated_merge((ys, yr, ya), (w["w_br_ssm"], w["w_br_ret"], w["w_br_att"]), l, u_tail,
                      (T_GA, T_GB, T_GC), tm=tm, tn=tn)
    x1, h2 = _out_proj_norm(mm, w["w_out"], l, x, lp["norm_ffn"], tm=DENSE_TILES["out_proj"][0])

    tm, tn = DENSE_TILES["ffn_in"]
    hh = _swiglu_matmul(h2, w["w_ffn_in"], l, tm=tm, tn=tn)
    tm, tn = DENSE_TILES["ffn_out"]
    x2 = _matmul(hh, w["w_ffn_out"], l, n=D_MODEL, out_dtype=F32, tm=tm, tn=tn, addend=x1,
                 split_out=(l == DEPTH - 1), name="ffn_out")

    keep = min(WINDOW, SEQ)
    nconv = CONV_W - 1

    def conv_rows(row0):
        return jnp.concatenate([_row_block(u_head, row0, nconv, H_X, D_SSM),
                                _row_block(u_head, row0, nconv, H_B, BC_WIDTH),
                                _row_block(u_head, row0, nconv, H_C, BC_WIDTH)], axis=-1)

    pv = jnp.stack([_row_block(u_tail, (b + 1) * SEQ - keep, keep, T_AV, dm) for b in range(BATCH)])
    pconv = jnp.stack([conv_rows((b + 1) * SEQ - nconv) for b in range(BATCH)])
    sv = _row_block(u_tail, M_PROMPT, M_SAMPLE, T_AV, dm)
    s_xbc = jnp.concatenate([_row_block(u_head, M_PROMPT, M_SAMPLE, H_X, D_SSM),
                             _row_block(u_head, M_PROMPT, M_SAMPLE, H_B, BC_WIDTH),
                             _row_block(u_head, M_PROMPT, M_SAMPLE, H_C, BC_WIDTH)], axis=-1)
    sconv = s_xbc.reshape(DEC_BATCH, DEC_SEQ, CONV_DIM)[:, DEC_SEQ - nconv:]

    outs_p = (knf_p.reshape(BATCH, keep, ATT_HEADS, ATT_HEAD_DIM),
              pv.reshape(BATCH, keep, ATT_HEADS, ATT_HEAD_DIM),
              pconv)
    outs_s = (knf_s.reshape(DEC_BATCH, DEC_SEQ, ATT_HEADS, ATT_HEAD_DIM),
              sv.reshape(DEC_BATCH, DEC_SEQ, ATT_HEADS, ATT_HEAD_DIM),
              sconv)
    new_states = {"ssm_p": ssm_p, "ssm_s": ssm_s, "ret_p": ret_p, "ret_s": ret_s}
    return x2, outs_p, outs_s, new_states


def kernel(x_prompt, x_sample, cache_attn_k, cache_attn_v, state_ret, state_ssm, state_conv,
           norm_mix, w_in, conv_w, conv_b, dt_bias, a_log, d_skip, ssm_norm, q_norm, k_norm,
           rel_bias, w_br_ssm, w_br_ret, w_br_att, w_out, norm_ffn, w_ffn_in, w_ffn_out):
    x = (x_prompt.reshape(M_PROMPT, D_MODEL), x_sample.reshape(M_SAMPLE, D_MODEL))
    cos_p, sin_p = _rope_tables(jnp.arange(SEQ))
    cos_s, sin_s = _rope_tables(PAST_LEN + jnp.arange(DEC_SEQ))
    tables = {"cos_p": cos_p, "sin_p": sin_p, "cos_s": cos_s, "sin_s": sin_s,
              "zeros_hist": jnp.zeros((1, BATCH, SUBLANES, CONV_DIM), F32),
              "zeros_h": jnp.zeros((1, BATCH, D_SSM, SSM_STATE), F32),
              "zeros_s": jnp.zeros((1, BATCH, RET_HEADS, RET_DK, RET_DV), F32)}
    w = {"w_in_t": jnp.swapaxes(w_in, 1, 2), "w_br_ssm": w_br_ssm, "w_br_ret": w_br_ret, "w_br_att": w_br_att, "w_out": w_out,
         "w_ffn_in": w_ffn_in, "w_ffn_out": w_ffn_out}
    hist8 = jnp.pad(state_conv, ((0, 0), (0, 0), (SUBLANES - (CONV_W - 1), 0), (0, 0)))
    state = (cache_attn_k, cache_attn_v, state_ret,
             state_ssm.reshape(DEPTH, DEC_BATCH, D_SSM, SSM_STATE), hist8)
    acc_p = ([], [], [])
    acc_s = ([], [], [])
    prev = {"ssm_p": None, "ssm_s": None, "ret_p": None, "ret_s": None}
    for l in range(DEPTH):
        lp = _layer_params(l, norm_mix, w_in, conv_w, conv_b, dt_bias, a_log, d_skip, ssm_norm, q_norm,
                           k_norm, rel_bias, w_br_ssm, w_br_ret, w_br_att, w_out, norm_ffn, w_ffn_in,
                           w_ffn_out)
        x, outs_p, outs_s, prev = _layer(x, l, lp, w, state, tables, prev)
        for i in range(3):
            acc_p[i].append(outs_p[i])
            acc_s[i].append(outs_s[i])
    pk, pv, pconv = [jnp.stack(a) for a in acc_p]
    sk, sv, sconv = [jnp.stack(a) for a in acc_s]
    pr, sr = prev["ret_p"], prev["ret_s"]
    pssm = prev["ssm_p"].reshape(DEPTH, BATCH, SSM_HEADS, SSM_HEAD_DIM, SSM_STATE)
    sssm = prev["ssm_s"].reshape(DEPTH, DEC_BATCH, SSM_HEADS, SSM_HEAD_DIM, SSM_STATE)
    yp = x[0].reshape(BATCH, SEQ, D_MODEL)
    ys = x[1].reshape(DEC_BATCH, DEC_SEQ, D_MODEL)
    return (yp, ys, pk, pv, pr, pssm, pconv, sk, sv, sr, sssm, sconv)
```

```python
import functools
import math

import numpy as np
import jax
import jax.numpy as jnp
from jax import lax
from jax.experimental import pallas as pl
from jax.experimental.pallas import tpu as pltpu

F32 = jnp.float32
BF16 = jnp.bfloat16

D_MODEL = 2048
BATCH = 4
SEQ = 2048
DEPTH = 2
DEC_BATCH = 32
DEC_SEQ = 16
PAST_LEN = 1024
CHUNK = 64
EPS = 1e-6
NEG_INF = -1e30

D_SSM = 2048
SSM_HEAD_DIM = 64
SSM_HEADS = 32
SSM_GROUPS = 4
SSM_STATE = 128
SSM_GROUP_WIDTH = D_SSM // SSM_GROUPS
CONV_W = 4
CONV_DIM = D_SSM + 2 * SSM_GROUPS * SSM_STATE
BC_WIDTH = SSM_GROUPS * SSM_STATE

RET_HEADS = 8
RET_DK = 128
RET_DV = 256
ROPE_BASE = 10000.0

ATT_HEADS = 16
ATT_HEAD_DIM = 128
BAND_CHUNKS = 8
WINDOW = BAND_CHUNKS * CHUNK
REL_CLIP = 256

FFN_HIDDEN = 5632

M_PROMPT = BATCH * SEQ
M_SAMPLE = DEC_BATCH * DEC_SEQ
M_ALL = M_PROMPT + M_SAMPLE

LANES = 128
SUBLANES = 8
V7X_VMEM_LIMIT_BYTES = 56 * 1024 * 1024

H_Z, H_X, H_B, H_C = 0, 2048, 4096, 4608
HEAD_WIDTH = 5120
DT_COL0 = 5120
TAIL_COL0 = DT_COL0 + SSM_HEADS
T_RQ, T_RK, T_RV, T_RG, T_AQ, T_AK, T_AV, T_GA, T_GB, T_GC = (
    0, 1024, 2048, 4096, 6144, 8192, 10240, 12288, 14336, 16384)
TAIL_WIDTH = 18432
N_IN = TAIL_COL0 + TAIL_WIDTH


DENSE_TILES = {
    "in_proj": (1088, 1024),
    "merge": (544, 512),
    "out_proj": (256, D_MODEL),
    "ffn_in": (1088, 512),
    "ffn_out": (M_SAMPLE, 512),
}


def _params(sem):
    return pltpu.CompilerParams(dimension_semantics=sem, vmem_limit_bytes=V7X_VMEM_LIMIT_BYTES)


def _append_aliased(in_specs, args, out_buffers):
    aliases = {}
    for k, buf in enumerate(out_buffers):
        if buf is not None:
            in_specs.append(pl.BlockSpec(memory_space=pl.ANY))
            args.append(buf)
            aliases[len(args) - 1] = k
    return aliases


def _sigmoid(x):
    return 1.0 / (1.0 + jnp.exp(-x))


def _silu(x):
    return x * _sigmoid(x)


def _rmsnorm_kernel(x_ref, g_ref, o_ref):
    x = x_ref[...]
    ms = jnp.mean(x * x, axis=-1, keepdims=True)
    o_ref[...] = (x * lax.rsqrt(ms + EPS) * g_ref[...]).astype(o_ref.dtype)


def _rmsnorm_split_kernel(xp_ref, xs_ref, g_ref, o_ref):
    last = pl.num_programs(0) - 1

    @pl.when(pl.program_id(0) < last)
    def _():
        _rmsnorm_kernel(xp_ref, g_ref, o_ref)

    @pl.when(pl.program_id(0) == last)
    def _():
        _rmsnorm_kernel(xs_ref, g_ref, o_ref)


def _rmsnorm(x, g, tm=544):
    d = g.shape[-1]
    gspec = pl.BlockSpec((1, d), lambda i: (0, 0))
    if isinstance(x, (tuple, list)):
        tm = M_SAMPLE
        last_prompt_tile = M_PROMPT // tm - 1
        return pl.pallas_call(
            _rmsnorm_split_kernel,
            grid=(M_ALL // tm,),
            in_specs=[pl.BlockSpec((tm, d), lambda i: (jnp.minimum(i, last_prompt_tile), 0)),
                      pl.BlockSpec((tm, d), lambda i: (0, 0)), gspec],
            out_specs=pl.BlockSpec((tm, d), lambda i: (i, 0)),
            out_shape=jax.ShapeDtypeStruct((M_ALL, d), BF16),
            compiler_params=_params(("arbitrary",)),
            name="rmsnorm_split",
        )(x[0], x[1], g.reshape(1, d))
    m = x.shape[0]
    return pl.pallas_call(
        _rmsnorm_kernel,
        grid=(m // tm,),
        in_specs=[pl.BlockSpec((tm, d), lambda i: (i, 0)), gspec],
        out_specs=pl.BlockSpec((tm, d), lambda i: (i, 0)),
        out_shape=jax.ShapeDtypeStruct((m, d), BF16),
        compiler_params=_params(("parallel",)),
        name="rmsnorm",
    )(x, g.reshape(1, d))


W_CAST_ROWS = 256


def _cast_weight_tile(w_ref, wx_ref, wb_ref, shift):
    rows = wb_ref.shape[0]
    for r in range(0, rows - shift, W_CAST_ROWS):
        n = min(W_CAST_ROWS, rows - shift - r)
        wb_ref[r:r + n, :] = w_ref[r + shift:r + shift + n, :].astype(BF16)
    if shift:
        wb_ref[rows - shift:, :] = wx_ref[:shift, :].astype(BF16)


def _mm_kernel(*refs, shift, n_add, n_out, w_transposed):
    a_ref, w_ref = refs[0], refs[1]
    pos = 2
    wx_ref = None
    if shift:
        wx_ref = refs[pos]
        pos += 1
    add_refs = refs[pos:pos + n_add]
    pos += n_add
    o_refs = refs[pos:pos + n_out]
    wb_ref = refs[pos + n_out]
    i = pl.program_id(1)

    @pl.when(i == 0)
    def _():
        _cast_weight_tile(w_ref, wx_ref, wb_ref, shift)

    if w_transposed:
        r = _dot_nt(a_ref[...], wb_ref[...])
    else:
        r = jnp.dot(a_ref[...], wb_ref[...], preferred_element_type=F32)

    def finish(add_ref, o_ref):
        res = r if add_ref is None else add_ref[...] + r
        o_ref[...] = res.astype(o_ref.dtype)

    if n_add < 2 and n_out < 2:
        finish(add_refs[0] if n_add else None, o_refs[0])
    else:
        last = pl.num_programs(1) - 1

        @pl.when(i < last)
        def _():
            finish(add_refs[0] if n_add else None, o_refs[0])

        @pl.when(i == last)
        def _():
            finish(add_refs[-1] if n_add else None, o_refs[-1])


def _matmul(a, w, layer, *, n, out_dtype, tm, tn, w_col0=0, w_transposed=False, addend=None, split_out=False,
            name="matmul"):
    m, kdim = a.shape
    add_parts = isinstance(addend, (tuple, list))
    if add_parts or split_out:
        assert tm == M_SAMPLE and M_PROMPT % tm == 0 and m == M_ALL
    last_prompt_tile = M_PROMPT // tm - 1
    wj0 = w_col0 // tn
    shift = w_col0 - wj0 * tn
    extra = LANES
    assert shift == 0 or (w_transposed and shift % SUBLANES == 0 and shift <= extra and tn % extra == 0)
    if w_transposed:
        w_spec = pl.BlockSpec((None, tn, kdim), lambda j, i: (layer, j + wj0, 0))
        wb_shape = (tn, kdim)
    else:
        w_spec = pl.BlockSpec((None, kdim, tn), lambda j, i: (layer, 0, j + wj0))
        wb_shape = (kdim, tn)
    in_specs = [pl.BlockSpec((tm, kdim), lambda j, i: (i, 0)), w_spec]
    args = [a, w]
    if shift:
        in_specs.append(pl.BlockSpec((None, extra, kdim), lambda j, i: (layer, (j + wj0 + 1) * (tn // extra), 0)))
        args.append(w)
    prompt_rows = pl.BlockSpec((tm, tn), lambda j, i: (jnp.minimum(i, last_prompt_tile), j))
    sample_rows = pl.BlockSpec((tm, tn), lambda j, i: (0, j))
    all_rows = pl.BlockSpec((tm, tn), lambda j, i: (i, j))
    if add_parts:
        in_specs += [prompt_rows, sample_rows]
        args += list(addend)
    elif addend is not None:
        in_specs.append(all_rows)
        args.append(addend)
    n_add = 2 if add_parts else int(addend is not None)
    if split_out:
        out_specs = [prompt_rows, sample_rows]
        out_shape = [jax.ShapeDtypeStruct((M_PROMPT, n), out_dtype), jax.ShapeDtypeStruct((M_SAMPLE, n), out_dtype)]
    else:
        out_specs = all_rows
        out_shape = jax.ShapeDtypeStruct((m, n), out_dtype)
    return pl.pallas_call(
        functools.partial(_mm_kernel, shift=shift, n_add=n_add, n_out=2 if split_out else 1,
                          w_transposed=w_transposed),
        grid=(n // tn, m // tm),
        in_specs=in_specs,
        out_specs=out_specs,
        out_shape=out_shape,
        scratch_shapes=[pltpu.VMEM(wb_shape, BF16)],
        compiler_params=_params(("parallel", "arbitrary")),
        name=name,
    )(*args)


def _out_proj_norm_kernel(*refs, n_add, n_prompt_tiles):
    a_ref, w_ref = refs[0], refs[1]
    add_refs = refs[2:2 + n_add]
    g_ref, x_ref, hn_ref, wb_ref = refs[2 + n_add:]
    i = pl.program_id(0)

    @pl.when(i == 0)
    def _():
        _cast_weight_tile(w_ref, None, wb_ref, 0)

    r = jnp.dot(a_ref[...], wb_ref[...], preferred_element_type=F32)

    def finish(add_ref):
        x_ref[...] = add_ref[...] + r
        _rmsnorm_kernel(x_ref, g_ref, hn_ref)

    if n_add == 1:
        finish(add_refs[0])
    else:
        @pl.when(i < n_prompt_tiles)
        def _():
            finish(add_refs[0])

        @pl.when(i >= n_prompt_tiles)
        def _():
            finish(add_refs[1])


def _out_proj_norm(a, w, layer, addend, g, *, tm):
    m, kdim = a.shape
    d = w.shape[-1]
    add_parts = isinstance(addend, (tuple, list))
    assert M_PROMPT % tm == 0 and M_SAMPLE % tm == 0
    n_prompt_tiles = M_PROMPT // tm
    row_tile = lambda index: pl.BlockSpec((tm, d), index)
    in_specs = [pl.BlockSpec((tm, kdim), lambda i: (i, 0)),
                pl.BlockSpec((None, kdim, d), lambda i: (layer, 0, 0), pipeline_mode=pl.Buffered(1))]
    args = [a, w]
    if add_parts:
        in_specs += [row_tile(lambda i: (jnp.minimum(i, n_prompt_tiles - 1), 0)),
                     row_tile(lambda i: (jnp.maximum(i - n_prompt_tiles, 0), 0))]
        args += list(addend)
    else:
        in_specs.append(row_tile(lambda i: (i, 0)))
        args.append(addend)
    in_specs.append(pl.BlockSpec((1, d), lambda i: (0, 0)))
    args.append(g.reshape(1, d))
    return pl.pallas_call(
        functools.partial(_out_proj_norm_kernel, n_add=2 if add_parts else 1, n_prompt_tiles=n_prompt_tiles),
        grid=(m // tm,),
        in_specs=in_specs,
        out_specs=[row_tile(lambda i: (i, 0)), row_tile(lambda i: (i, 0))],
        out_shape=[jax.ShapeDtypeStruct((m, d), F32), jax.ShapeDtypeStruct((m, d), BF16)],
        scratch_shapes=[pltpu.VMEM((kdim, d), BF16)],
        compiler_params=_params(("arbitrary",)),
        name="out_proj_norm",
    )(*args)


def _merge_kernel(a0_ref, a1_ref, a2_ref, w0_ref, w1_ref, w2_ref, g0_ref, g1_ref, g2_ref, o_ref,
                  wb0_ref, wb1_ref, wb2_ref):
    @pl.when(pl.program_id(1) == 0)
    def _():
        for w_ref, wb_ref in ((w0_ref, wb0_ref), (w1_ref, wb1_ref), (w2_ref, wb2_ref)):
            _cast_weight_tile(w_ref, None, wb_ref, 0)

    acc = None
    for a_ref, wb_ref, g_ref in ((a0_ref, wb0_ref, g0_ref), (a1_ref, wb1_ref, g1_ref), (a2_ref, wb2_ref, g2_ref)):
        term = _sigmoid(g_ref[...]) * jnp.dot(a_ref[...], wb_ref[...], preferred_element_type=F32)
        acc = term if acc is None else acc + term
    o_ref[...] = acc.astype(o_ref.dtype)


def _gated_merge(branches, weights, layer, gates, gate_cols, *, tm, tn):
    m, kdim = branches[0].shape
    a_spec = pl.BlockSpec((tm, kdim), lambda j, i: (i, 0))
    w_spec = pl.BlockSpec((None, kdim, tn), lambda j, i: (layer, 0, j), pipeline_mode=pl.Buffered(1))
    g_specs = [pl.BlockSpec((tm, tn), functools.partial(lambda j, i, j0: (i, j + j0), j0=col // tn))
               for col in gate_cols]
    return pl.pallas_call(
        _merge_kernel,
        grid=(D_MODEL // tn, m // tm),
        in_specs=[a_spec] * 3 + [w_spec] * 3 + g_specs,
        out_specs=pl.BlockSpec((tm, tn), lambda j, i: (i, j)),
        out_shape=jax.ShapeDtypeStruct((m, D_MODEL), BF16),
        scratch_shapes=[pltpu.VMEM((kdim, tn), BF16)] * 3,
        compiler_params=_params(("parallel", "arbitrary")),
        name="gated_merge",
    )(*branches, *weights, gates, gates, gates)


def _swiglu_kernel(a_ref, wa_ref, wc_ref, o_ref, wab_ref, wcb_ref):
    @pl.when(pl.program_id(1) == 0)
    def _():
        _cast_weight_tile(wa_ref, None, wab_ref, 0)
        _cast_weight_tile(wc_ref, None, wcb_ref, 0)

    a = a_ref[...]
    fa = jnp.dot(a, wab_ref[...], preferred_element_type=F32)
    fc = jnp.dot(a, wcb_ref[...], preferred_element_type=F32)
    o_ref[...] = (_silu(fa) * fc).astype(o_ref.dtype)


def _swiglu_matmul(a, w, layer, *, tm, tn):
    m, kdim = a.shape
    nj = FFN_HIDDEN // tn
    return pl.pallas_call(
        _swiglu_kernel,
        grid=(nj, m // tm),
        in_specs=[pl.BlockSpec((tm, kdim), lambda j, i: (i, 0)),
                  pl.BlockSpec((None, kdim, tn), lambda j, i: (layer, 0, j)),
                  pl.BlockSpec((None, kdim, tn), lambda j, i: (layer, 0, j + nj))],
        out_specs=pl.BlockSpec((tm, tn), lambda j, i: (i, j)),
        out_shape=jax.ShapeDtypeStruct((m, FFN_HIDDEN), BF16),
        scratch_shapes=[pltpu.VMEM((kdim, tn), BF16), pltpu.VMEM((kdim, tn), BF16)],
        compiler_params=_params(("parallel", "arbitrary")),
        name="ffn_in_swiglu",
    )(a, w, w)


def _split3(a):
    hi = a.astype(BF16)
    r1 = a - hi.astype(F32)
    mid = r1.astype(BF16)
    lo = (r1 - mid.astype(F32)).astype(BF16)
    return hi, mid, lo


def _exact_right(a, onehot):
    return sum(jnp.dot(p, onehot, preferred_element_type=F32) for p in _split3(a))


def _exact_right_many(arrays, onehot):
    n = arrays[0].shape[0]
    r = jnp.dot(jnp.concatenate([p for a in arrays for p in _split3(a)], axis=0), onehot,
                preferred_element_type=F32)
    return [r[3 * k * n:(3 * k + 1) * n] + r[(3 * k + 1) * n:(3 * k + 2) * n] + r[(3 * k + 2) * n:(3 * k + 3) * n]
            for k in range(len(arrays))]


def _exact_left(onehot, b):
    return sum(jnp.dot(onehot, p, preferred_element_type=F32) for p in _split3(b))


def _dot_nt(a, b):
    return lax.dot_general(a, b, (((1,), (1,)), ((), ())), preferred_element_type=F32)


def _dot_tn(a, b):
    return lax.dot_general(a, b, (((0,), (0,)), ((), ())), preferred_element_type=F32)


def _conv_silu(ext_ref, cw, cb, L):
    acc = ext_ref[SUBLANES:SUBLANES + L, :] * cw[CONV_W - 1:CONV_W, :] + cb
    for s in range(1, CONV_W):
        acc = acc + ext_ref[SUBLANES - s:SUBLANES - s + L, :] * cw[CONV_W - 1 - s:CONV_W - s, :]
    return _silu(acc)


def _ssd_kernel(x_ref, b_ref, c_ref, z_ref, dt_ref, hist_ref, h0_ref,
                cwx_ref, cwb_ref, cwc_ref, cbx_ref, cbb_ref, cbc_ref,
                dtb_ref, alog_ref, dskip_ref, normw_ref, exp_p_ref, exp_l_ref,
                *rest, L, nc, n_aliased):
    y_ref, hout_ref, ex_ref, eb_ref, ec_ref, ht_ref = rest[n_aliased:]
    c = pl.program_id(1)
    lw = SSM_HEADS * L

    @pl.when(c == 0)
    def _():
        hist = hist_ref[0]
        ex_ref[0:SUBLANES, :] = hist[:, :D_SSM]
        eb_ref[0:SUBLANES, :] = hist[:, D_SSM:D_SSM + BC_WIDTH]
        ec_ref[0:SUBLANES, :] = hist[:, D_SSM + BC_WIDTH:]
        ht_ref[...] = h0_ref[0].T

    @pl.when(c > 0)
    def _():
        ex_ref[0:SUBLANES, :] = ex_ref[L:L + SUBLANES, :]
        eb_ref[0:SUBLANES, :] = eb_ref[L:L + SUBLANES, :]
        ec_ref[0:SUBLANES, :] = ec_ref[L:L + SUBLANES, :]

    ex_ref[SUBLANES:SUBLANES + L, :] = x_ref[...]
    eb_ref[SUBLANES:SUBLANES + L, :] = b_ref[...]
    ec_ref[SUBLANES:SUBLANES + L, :] = c_ref[...]
    xs = _conv_silu(ex_ref, cwx_ref[...], cbx_ref[...], L)
    bm = _conv_silu(eb_ref, cwb_ref[...], cbb_ref[...], L).astype(BF16)
    cm = _conv_silu(ec_ref, cwc_ref[...], cbc_ref[...], L).astype(BF16)

    dtr = dt_ref[...] + dtb_ref[...]
    dt = jnp.maximum(dtr, 0.0) + jnp.log1p(jnp.exp(-jnp.abs(dtr)))

    tri = (lax.broadcasted_iota(jnp.int32, (L, L), 1)
           <= lax.broadcasted_iota(jnp.int32, (L, L), 0)).astype(BF16)
    acum = _exact_left(tri, dt * (-jnp.exp(alog_ref[...])))

    if L == SSM_HEAD_DIM:
        dt_p = _exact_right(dt, exp_p_ref[...])
        acum_p = _exact_right(acum, exp_p_ref[...])
    else:
        dt_p, acum_p = _exact_right_many([dt, acum], exp_p_ref[...])
    last_p = acum_p[L - 1:L, :]
    if L == SSM_HEAD_DIM:
        dt_l, acum_l = dt_p, acum_p
    else:
        dt_l, acum_l = _exact_right_many([dt, acum], exp_l_ref[...])

    rowi = lax.broadcasted_iota(jnp.int32, (L, lw), 0)
    colj = lax.broadcasted_iota(jnp.int32, (L, lw), 1) % L
    diag = colj == rowi
    causal = colj <= rowi
    acum_row = jnp.sum(jnp.where(diag, acum_l, 0.0), axis=0, keepdims=True)
    dt_row = jnp.sum(jnp.where(diag, dt_l, 0.0), axis=0, keepdims=True)
    decay = jnp.where(causal, jnp.exp(jnp.where(causal, acum_l - acum_row, 0.0)), 0.0)

    heads_per_group = SSM_HEADS // SSM_GROUPS
    cb = jnp.concatenate(
        [_dot_nt(cm[:, g * SSM_STATE:(g + 1) * SSM_STATE],
                 jnp.concatenate([bm[:, g * SSM_STATE:(g + 1) * SSM_STATE]] * heads_per_group, axis=0))
         for g in range(SSM_GROUPS)], axis=1)

    ht = ht_ref[...]
    ht_b = ht.astype(BF16)
    to_end = jnp.exp(last_p - acum_p) * dt_p
    xw = (xs * to_end).astype(BF16)
    y_inter = []
    st_parts = []
    for g in range(SSM_GROUPS):
        gs = slice(g * SSM_GROUP_WIDTH, (g + 1) * SSM_GROUP_WIDTH)
        ns = slice(g * SSM_STATE, (g + 1) * SSM_STATE)
        y_inter.append(jnp.dot(cm[:, ns], ht_b[:, gs], preferred_element_type=F32))
        st_parts.append(_dot_tn(bm[:, ns], xw[:, gs]))
    ht_new = jnp.exp(last_p) * ht + jnp.concatenate(st_parts, axis=1)
    ht_ref[...] = ht_new

    m_all = (cb * decay * dt_row).astype(BF16)
    hp = LANES // L
    slab = hp * SSM_HEAD_DIM
    xs_b = xs.astype(BF16)
    lane_head = lax.broadcasted_iota(jnp.int32, (L, slab), 1) // SSM_HEAD_DIM
    y_parts = []
    for q in range(lw // LANES):
        xq = xs_b[:, q * slab:(q + 1) * slab]
        bd = jnp.concatenate([jnp.where(lane_head == h, xq, jnp.zeros_like(xq)) for h in range(hp)], axis=0)
        y_parts.append(jnp.dot(m_all[:, q * LANES:(q + 1) * LANES], bd, preferred_element_type=F32))
    y = (jnp.concatenate(y_parts, axis=1) + jnp.concatenate(y_inter, axis=1) * jnp.exp(acum_p)
         + dskip_ref[...] * xs)

    t = y * _silu(z_ref[...])
    outs = []
    for g in range(SSM_GROUPS):
        tg = t[:, g * SSM_GROUP_WIDTH:(g + 1) * SSM_GROUP_WIDTH]
        outs.append(tg * lax.rsqrt(jnp.mean(tg * tg, axis=-1, keepdims=True) + EPS))
    y_ref[...] = (jnp.concatenate(outs, axis=1) * normw_ref[...]).astype(y_ref.dtype)

    @pl.when(c == nc - 1)
    def _():
        hout_ref[0] = ht_new.T


def _ssd_call(u, dt_raw, hist8, h0, layer, lp, *, L, nb, nc, row0, out_layer, prev_out, prev_state):
    rb0 = row0 // L

    def rows(width, colblock):
        return pl.BlockSpec((L, width), lambda b, c: (rb0 + b * nc + c, colblock))

    def const(shape):
        return pl.BlockSpec(shape, lambda b, c: tuple(0 for _ in shape))

    lw = SSM_HEADS * L
    in_specs = [rows(D_SSM, H_X // D_SSM), rows(BC_WIDTH, H_B // BC_WIDTH), rows(BC_WIDTH, H_C // BC_WIDTH),
                rows(D_SSM, H_Z // D_SSM),
                pl.BlockSpec((L, LANES), lambda b, c: (rb0 + b * nc + c, 0)),
                pl.BlockSpec((None, 1, SUBLANES, CONV_DIM), lambda b, c: (layer, b, 0, 0)),
                pl.BlockSpec((None, 1, D_SSM, SSM_STATE), lambda b, c: (layer, b, 0, 0)),
                const((CONV_W, D_SSM)), const((CONV_W, BC_WIDTH)), const((CONV_W, BC_WIDTH)),
                const((1, D_SSM)), const((1, BC_WIDTH)), const((1, BC_WIDTH)),
                const((1, LANES)), const((1, LANES)), const((1, D_SSM)), const((1, D_SSM)),
                const((LANES, D_SSM)), const((LANES, lw))]
    args = [u, u, u, u, dt_raw, hist8, h0,
            lp["conv_wx"], lp["conv_wb"], lp["conv_wc"], lp["conv_bx"], lp["conv_bb"], lp["conv_bc"],
            lp["dt_bias"], lp["a_log"], lp["d_skip_p"], lp["ssm_norm"],
            _expand_onehot(SSM_HEAD_DIM), _expand_onehot(L)]
    aliases = _append_aliased(in_specs, args, (prev_out, prev_state))
    y, hout = pl.pallas_call(
        functools.partial(_ssd_kernel, L=L, nc=nc, n_aliased=len(aliases)),
        grid=(nb, nc),
        in_specs=in_specs,
        out_specs=[pl.BlockSpec((L, D_SSM), lambda b, c: (rb0 + b * nc + c, 0)),
                   pl.BlockSpec((None, 1, D_SSM, SSM_STATE), lambda b, c: (out_layer, b, 0, 0))],
        out_shape=[jax.ShapeDtypeStruct((M_ALL, D_SSM), BF16),
                   jax.ShapeDtypeStruct((DEPTH, nb, D_SSM, SSM_STATE), F32)],
        scratch_shapes=[pltpu.VMEM((SUBLANES + L, D_SSM), F32), pltpu.VMEM((SUBLANES + L, BC_WIDTH), F32),
                        pltpu.VMEM((SUBLANES + L, BC_WIDTH), F32), pltpu.VMEM((SSM_STATE, D_SSM), F32)],
        input_output_aliases=aliases,
        compiler_params=_params(("parallel", "arbitrary")),
        name="ssd_L%d" % L,
    )(*args)
    return y, hout


def _expand_onehot(per_head):
    e = np.zeros((LANES, SSM_HEADS * per_head), np.float32)
    for h in range(SSM_HEADS):
        e[h, h * per_head:(h + 1) * per_head] = 1.0
    return jnp.asarray(e, BF16)


def _ret_consts(L):
    lam = np.log1p(-np.exp2(-5.0 - np.arange(RET_HEADS, dtype=np.float64)))
    i = np.arange(L, dtype=np.float64)
    diff = i[:, None] - i[None, :]
    dmat = np.where(diff >= 0, np.exp(lam[:, None, None] * np.maximum(diff, 0.0)), 0.0)
    to_end = np.exp(lam[:, None] * (L - 1 - i)[None, :])
    q_dec = np.exp(lam[:, None] * (i + 1.0)[None, :])
    chunk_decay = np.exp(lam * L)
    expand = lambda t: np.repeat(t.T[:, :, None], RET_DK, axis=2).reshape(L, RET_HEADS * RET_DK)
    return (jnp.asarray(dmat, F32), jnp.asarray(expand(q_dec), F32), jnp.asarray(expand(to_end), F32),
            [float(v) for v in chunk_decay])


RET_HEADS_AHEAD = 8
RET_CHUNKS_PER_STEP = 4


def _ret_kernel(q_ref, k_ref, v_ref, g_ref, cos_ref, sin_ref, qdec_ref, kend_ref, dmat_ref, s0_ref,
                *rest, L, cps, chunk_decay, n_aliased):
    y_ref, sout_ref, s_ref = rest[n_aliased:]
    step = pl.program_id(1)

    @pl.when(step == 0)
    def _():
        s_ref[...] = s0_ref[0]

    half = RET_DK // 2

    def one_chunk(sub, carry):
        rows = slice(0, L) if cps == 1 else pl.ds(pl.multiple_of(sub * L, L), L)
        cos = cos_ref[rows, :]
        sin = sin_ref[rows, :]

        def first_stage(h):
            ks = slice(h * RET_DK, (h + 1) * RET_DK)
            qh = q_ref[rows, ks]
            kh = k_ref[rows, ks]
            qr = qh * cos + pltpu.roll(qh, half, 1) * sin
            kr = (kh * cos + pltpu.roll(kh, half, 1) * sin) * (RET_DK ** -0.5)
            vh = v_ref[rows, h * RET_DV:(h + 1) * RET_DV].astype(BF16)
            s_prev = s_ref[h]
            qk = _dot_nt(qr.astype(BF16), kr.astype(BF16))
            o_state = jnp.dot((qr * qdec_ref[:, ks]).astype(BF16), s_prev.astype(BF16),
                              preferred_element_type=F32)
            kv = _dot_tn((kr * kend_ref[:, ks]).astype(BF16), vh)
            s_ref[h] = chunk_decay[h] * s_prev + kv
            return qk, o_state, vh

        def second_stage(h, qk, o_state, vh):
            vs = slice(h * RET_DV, (h + 1) * RET_DV)
            s = qk * dmat_ref[h]
            o = jnp.dot(s.astype(BF16), vh, preferred_element_type=F32) + o_state
            o = o * lax.rsqrt(jnp.mean(o * o, axis=-1, keepdims=True) + EPS)
            y_ref[rows, vs] = (o * _silu(g_ref[rows, vs])).astype(y_ref.dtype)

        pending = [first_stage(h) for h in range(RET_HEADS_AHEAD)]
        for h in range(RET_HEADS):
            if h + RET_HEADS_AHEAD < RET_HEADS:
                pending.append(first_stage(h + RET_HEADS_AHEAD))
            second_stage(h, *pending.pop(0))
        return carry

    if cps == 1:
        one_chunk(0, 0)
    else:
        lax.fori_loop(0, cps, one_chunk, 0)

    @pl.when(step == pl.num_programs(1) - 1)
    def _():
        sout_ref[0] = s_ref[...]


def _ret_call(u, cos_t, sin_t, s0, layer, *, L, nb, nc, row0, out_layer, prev_out, prev_state):
    cps = RET_CHUNKS_PER_STEP if nc % RET_CHUNKS_PER_STEP == 0 else 1
    blk = cps * L
    nsteps = nc // cps
    rb0 = row0 // blk
    dmat, qdec, kend, chunk_decay = _ret_consts(L)
    hk = RET_HEADS * RET_DK
    hv = RET_HEADS * RET_DV

    def rows(width, colblock):
        return pl.BlockSpec((blk, width), lambda b, c: (rb0 + b * nsteps + c, colblock))

    in_specs = [rows(hk, T_RQ // hk), rows(hk, T_RK // hk), rows(hv, T_RV // hv), rows(hv, T_RG // hv),
                pl.BlockSpec((blk, RET_DK), lambda b, c: (c, 0)),
                pl.BlockSpec((blk, RET_DK), lambda b, c: (c, 0)),
                pl.BlockSpec((L, hk), lambda b, c: (0, 0)),
                pl.BlockSpec((L, hk), lambda b, c: (0, 0)),
                pl.BlockSpec((RET_HEADS, L, L), lambda b, c: (0, 0, 0)),
                pl.BlockSpec((None, 1, RET_HEADS, RET_DK, RET_DV), lambda b, c: (layer, b, 0, 0, 0))]
    args = [u, u, u, u, cos_t, sin_t, qdec, kend, dmat, s0]
    aliases = _append_aliased(in_specs, args, (prev_out, prev_state))
    y, sout = pl.pallas_call(
        functools.partial(_ret_kernel, L=L, cps=cps, chunk_decay=chunk_decay, n_aliased=len(aliases)),
        grid=(nb, nsteps),
        in_specs=in_specs,
        out_specs=[pl.BlockSpec((blk, hv), lambda b, c: (rb0 + b * nsteps + c, 0)),
                   pl.BlockSpec((None, 1, RET_HEADS, RET_DK, RET_DV), lambda b, c: (out_layer, b, 0, 0, 0))],
        out_shape=[jax.ShapeDtypeStruct((M_ALL, hv), BF16),
                   jax.ShapeDtypeStruct((DEPTH, nb, RET_HEADS, RET_DK, RET_DV), F32)],
        scratch_shapes=[pltpu.VMEM((RET_HEADS, RET_DK, RET_DV), F32)],
        input_output_aliases=aliases,
        compiler_params=_params(("parallel", "arbitrary")),
        name="retention_L%d" % L,
    )(*args)
    return y, sout


def _rope_tables(pos):
    half = RET_DK // 2
    inv = ROPE_BASE ** (-jnp.arange(half, dtype=F32) * 2.0 / RET_DK)
    ang = pos.astype(F32)[:, None] * inv[None, :]
    cos = jnp.cos(ang)
    sin = jnp.sin(ang)
    return jnp.concatenate([cos, cos], axis=1), jnp.concatenate([-sin, sin], axis=1)


def _head_norm(xh, g):
    return xh * lax.rsqrt(jnp.mean(xh * xh, axis=-1, keepdims=True) + EPS) * g


def _kv_prep_kernel(k_ref, v_ref, gk_ref, knf_ref, knb_ref, vb_ref):
    for h in range(ATT_HEADS):
        hs = slice(h * ATT_HEAD_DIM, (h + 1) * ATT_HEAD_DIM)
        kn = _head_norm(k_ref[:, hs], gk_ref[...])
        knf_ref[0, :, hs] = kn
        knb_ref[0, :, hs] = kn.astype(BF16)
    vb_ref[0] = v_ref[...].astype(BF16)


def _kv_prep_sample(u, gk):
    dm = ATT_HEADS * ATT_HEAD_DIM
    rb = M_PROMPT // M_SAMPLE
    return pl.pallas_call(
        _kv_prep_kernel,
        grid=(1,),
        in_specs=[pl.BlockSpec((M_SAMPLE, dm), lambda i: (rb, T_AK // dm)),
                  pl.BlockSpec((M_SAMPLE, dm), lambda i: (rb, T_AV // dm)),
                  pl.BlockSpec((1, ATT_HEAD_DIM), lambda i: (0, 0))],
        out_specs=[pl.BlockSpec((1, M_SAMPLE, dm), lambda i: (0, 0, 0)),
                   pl.BlockSpec((1, M_SAMPLE, dm), lambda i: (0, 0, 0)),
                   pl.BlockSpec((1, M_SAMPLE, dm), lambda i: (0, 0, 0))],
        out_shape=[jax.ShapeDtypeStruct((1, M_SAMPLE, dm), F32),
                   jax.ShapeDtypeStruct((1, M_SAMPLE, dm), BF16),
                   jax.ShapeDtypeStruct((1, M_SAMPLE, dm), BF16)],
        compiler_params=_params(("arbitrary",)),
        name="kv_prep_sample",
    )(u, u, gk)


def _scores(qn, k1, k2, bias1, bias2, valid1):
    scale = ATT_HEAD_DIM ** -0.5
    s1 = _dot_nt(qn, k1) * scale + bias1
    if valid1 is not None:
        s1 = jnp.where(valid1, s1, NEG_INF)
    s2 = _dot_nt(qn, k2) * scale + bias2
    return s1, s2


def _softmax_pv(s1, s2, v1, v2):
    m = jnp.maximum(jnp.max(s1, axis=-1, keepdims=True), jnp.max(s2, axis=-1, keepdims=True))
    p1 = jnp.exp(s1 - m)
    p2 = jnp.exp(s2 - m)
    inv = 1.0 / (jnp.sum(p1, axis=-1, keepdims=True) + jnp.sum(p2, axis=-1, keepdims=True))
    return (jnp.dot((p1 * inv).astype(BF16), v1, preferred_element_type=F32)
            + jnp.dot((p2 * inv).astype(BF16), v2, preferred_element_type=F32))


def _attend(qn, k1, v1, k2, v2, bias1, bias2, valid1):
    s1, s2 = _scores(qn, k1, k2, bias1, bias2, valid1)
    return _softmax_pv(s1, s2, v1, v2)


ATTN_HEADS_AHEAD = 8


ATTN_CHUNKS_PER_STEP = 4


def _attn_prompt_kernel(q_ref, k_ref, v_ref, b1_ref, b2_ref, gq_ref, gk_ref, y_ref, knf_ref, kb_ref, vb_ref):
    step = pl.program_id(1)
    heads = [slice(h * ATT_HEAD_DIM, (h + 1) * ATT_HEAD_DIM) for h in range(ATT_HEADS)]

    @pl.when(step == 0)
    def _():
        kb_ref[0:WINDOW, :] = jnp.zeros((WINDOW, kb_ref.shape[1]), BF16)
        vb_ref[0:WINDOW, :] = jnp.zeros((WINDOW, vb_ref.shape[1]), BF16)

    def one_chunk(sub, carry):
        c = step * ATTN_CHUNKS_PER_STEP + sub
        rows = pl.ds(pl.multiple_of(sub * CHUNK, CHUNK), CHUNK)
        base = pl.multiple_of(c * CHUNK, CHUNK)
        own = pl.multiple_of(c * CHUNK + WINDOW, CHUNK)

        for hs in heads:
            kn = _head_norm(k_ref[rows, hs], gk_ref[...])
            knf_ref[0, rows, hs] = kn
            kb_ref[pl.ds(own, CHUNK), hs] = kn.astype(BF16)
        vb_ref[pl.ds(own, CHUNK), :] = v_ref[rows, :].astype(BF16)

        col = lax.broadcasted_iota(jnp.int32, (CHUNK, WINDOW), 1)
        valid1 = col + c * CHUNK >= WINDOW

        def scores(h):
            hs = heads[h]
            qn = _head_norm(q_ref[rows, hs], gq_ref[...]).astype(BF16)
            return _scores(qn, kb_ref[pl.ds(base, WINDOW), hs], kb_ref[pl.ds(own, CHUNK), hs],
                           b1_ref[h], b2_ref[h], valid1)

        pending = [scores(h) for h in range(ATTN_HEADS_AHEAD)]
        for h in range(ATT_HEADS):
            if h + ATTN_HEADS_AHEAD < ATT_HEADS:
                pending.append(scores(h + ATTN_HEADS_AHEAD))
            s1, s2 = pending.pop(0)
            hs = heads[h]
            o = _softmax_pv(s1, s2, vb_ref[pl.ds(base, WINDOW), hs], vb_ref[pl.ds(own, CHUNK), hs])
            y_ref[rows, hs] = o.astype(y_ref.dtype)
        return carry

    lax.fori_loop(0, ATTN_CHUNKS_PER_STEP, one_chunk, 0)


def _attn_prompt_call(u, bias1, bias2, gq, gk):
    dm = ATT_HEADS * ATT_HEAD_DIM
    blk = ATTN_CHUNKS_PER_STEP * CHUNK
    nsteps = SEQ // blk
    keep = min(WINDOW, SEQ)
    assert SEQ % blk == 0 and keep % blk == 0
    first_kept = nsteps - keep // blk

    def rows(col0):
        return pl.BlockSpec((blk, dm), lambda b, c: (b * nsteps + c, col0 // dm))

    return pl.pallas_call(
        _attn_prompt_kernel,
        grid=(BATCH, nsteps),
        in_specs=[rows(T_AQ), rows(T_AK), rows(T_AV),
                  pl.BlockSpec((ATT_HEADS, CHUNK, WINDOW), lambda b, c: (0, 0, 0)),
                  pl.BlockSpec((ATT_HEADS, CHUNK, CHUNK), lambda b, c: (0, 0, 0)),
                  pl.BlockSpec((1, ATT_HEAD_DIM), lambda b, c: (0, 0)),
                  pl.BlockSpec((1, ATT_HEAD_DIM), lambda b, c: (0, 0))],
        out_specs=[pl.BlockSpec((blk, dm), lambda b, c: (b * nsteps + c, 0)),
                   pl.BlockSpec((1, blk, dm), lambda b, c: (b, jnp.maximum(c - first_kept, 0), 0))],
        out_shape=[jax.ShapeDtypeStruct((M_ALL, dm), BF16),
                   jax.ShapeDtypeStruct((BATCH, keep, dm), F32)],
        scratch_shapes=[pltpu.VMEM((SEQ + WINDOW, dm), BF16), pltpu.VMEM((SEQ + WINDOW, dm), BF16)],
        compiler_params=_params(("parallel", "arbitrary")),
        name="attn_prompt",
    )(u, u, u, bias1, bias2, gq, gk)


ATTN_SAMPLE_ROW_GROUPS = 2


def _attn_sample_kernel(q_ref, kh_ref, vh_ref, kn_ref, vn_ref, b1_ref, b2_ref, gq_ref, prev_ref, y_ref):
    del prev_ref
    heads = [slice(h * ATT_HEAD_DIM, (h + 1) * ATT_HEAD_DIM) for h in range(ATT_HEADS)]
    hist = kh_ref.shape[1]
    qall = jnp.concatenate([_head_norm(q_ref[:, hs], gq_ref[...]).astype(BF16) for hs in heads], axis=0)
    k1 = kh_ref[0].reshape(hist * ATT_HEADS, ATT_HEAD_DIM).astype(BF16)
    v1 = vh_ref[0].reshape(hist * ATT_HEADS, ATT_HEAD_DIM).astype(BF16)
    k2 = jnp.concatenate([kn_ref[:, hs] for hs in heads], axis=0)
    v2 = jnp.concatenate([vn_ref[:, hs] for hs in heads], axis=0)
    nrows = qall.shape[0]
    step = nrows // ATTN_SAMPLE_ROW_GROUPS
    groups = [slice(r, r + step) for r in range(0, nrows, step)]
    sc = [_scores(qall[rs], k1, k2, b1_ref[rs, :], b2_ref[rs, :], None) for rs in groups]
    o = jnp.concatenate([_softmax_pv(s1, s2, v1, v2) for s1, s2 in sc], axis=0)
    for h, hs in enumerate(heads):
        y_ref[:, hs] = o[h * DEC_SEQ:(h + 1) * DEC_SEQ].astype(y_ref.dtype)


def _bias_expand_kernel(t1_ref, t2_ref, o1_ref, o2_ref):
    rows, cb = o1_ref.shape
    row_head = lax.broadcasted_iota(jnp.int32, (rows, cb), 0) // DEC_SEQ
    col = lax.broadcasted_iota(jnp.int32, (rows, cb), 1)
    ej = lax.broadcasted_iota(jnp.int32, (LANES, cb), 0)
    ec = lax.broadcasted_iota(jnp.int32, (LANES, cb), 1)
    e1 = jnp.where(ec // ATT_HEADS == ej, 1.0, 0.0).astype(BF16)
    o1_ref[...] = jnp.where(col % ATT_HEADS == row_head, _exact_right(t1_ref[...], e1), NEG_INF)
    n2 = o2_ref.shape[1]
    row_head2 = lax.broadcasted_iota(jnp.int32, (rows, n2), 0) // DEC_SEQ
    col2 = lax.broadcasted_iota(jnp.int32, (rows, n2), 1)
    ej2 = lax.broadcasted_iota(jnp.int32, (LANES, n2), 0)
    ec2 = lax.broadcasted_iota(jnp.int32, (LANES, n2), 1)
    e2 = jnp.where(ec2 % DEC_SEQ == ej2, 1.0, 0.0).astype(BF16)
    o2_ref[...] = jnp.where(col2 // DEC_SEQ == row_head2, _exact_right(t2_ref[...], e2), NEG_INF)


def _head_masked_bias(tbl1, tbl2):
    nh, nq, hist = tbl1.shape
    rows = nh * nq
    t1 = tbl1.reshape(rows, hist)
    t2 = jnp.pad(tbl2.reshape(rows, nq), ((0, 0), (0, LANES - nq)))
    cb = LANES * nh
    return pl.pallas_call(
        _bias_expand_kernel,
        grid=(hist // LANES,),
        in_specs=[pl.BlockSpec((rows, LANES), lambda j: (0, j)),
                  pl.BlockSpec((rows, LANES), lambda j: (0, 0))],
        out_specs=[pl.BlockSpec((rows, cb), lambda j: (0, j)),
                   pl.BlockSpec((rows, nh * nq), lambda j: (0, 0))],
        out_shape=[jax.ShapeDtypeStruct((rows, hist * nh), F32),
                   jax.ShapeDtypeStruct((rows, nh * nq), F32)],
        compiler_params=_params(("arbitrary",)),
        name="bias_expand",
    )(t1, t2)


def _attn_sample_call(u, k_hist, v_hist, layer, kn_new, vn_new, bias1, bias2, gq, prev_out):
    dm = ATT_HEADS * ATT_HEAD_DIM
    rb0 = M_PROMPT // DEC_SEQ
    hist = k_hist.shape[2]
    cache_spec = pl.BlockSpec((None, 1, hist, ATT_HEADS, ATT_HEAD_DIM), lambda b: (layer, b, 0, 0, 0))
    return pl.pallas_call(
        _attn_sample_kernel,
        grid=(DEC_BATCH,),
        in_specs=[pl.BlockSpec((DEC_SEQ, dm), lambda b: (rb0 + b, T_AQ // dm)),
                  cache_spec,
                  cache_spec,
                  pl.BlockSpec((DEC_SEQ, dm), lambda b: (b, 0)),
                  pl.BlockSpec((DEC_SEQ, dm), lambda b: (b, 0)),
                  pl.BlockSpec((ATT_HEADS * DEC_SEQ, hist * ATT_HEADS), lambda b: (0, 0),
                               pipeline_mode=pl.Buffered(1)),
                  pl.BlockSpec((ATT_HEADS * DEC_SEQ, ATT_HEADS * DEC_SEQ), lambda b: (0, 0)),
                  pl.BlockSpec((1, ATT_HEAD_DIM), lambda b: (0, 0)),
                  pl.BlockSpec(memory_space=pl.ANY)],
        out_specs=pl.BlockSpec((DEC_SEQ, dm), lambda b: (rb0 + b, 0)),
        out_shape=jax.ShapeDtypeStruct((M_ALL, dm), BF16),
        input_output_aliases={8: 0},
        compiler_params=_params(("arbitrary",)),
        name="attn_sample",
    )(u, k_hist, v_hist, kn_new, vn_new, bias1, bias2, gq, prev_out)


def _bias_table(rel_bias):
    kbn = WINDOW + CHUNK
    period = kbn + CHUNK
    m = np.arange(period)
    d = np.where(m < kbn, m, m - period)
    idx = np.clip(WINDOW - d, -REL_CLIP, REL_CLIP) + REL_CLIP
    vec = rel_bias[:, idx]
    rows = jnp.tile(vec, (1, CHUNK))[:, :CHUNK * (period - 1)].reshape(ATT_HEADS, CHUNK, period - 1)
    return rows[:, :, :kbn]


def _layer_params(l, norm_mix, w_in, conv_w, conv_b, dt_bias, a_log, d_skip, ssm_norm, q_norm, k_norm,
                  rel_bias, w_br_ssm, w_br_ret, w_br_att, w_out, norm_ffn, w_ffn_in, w_ffn_out):
    assert w_in.shape[-1] == N_IN
    cw = conv_w[l]
    cb = conv_b[l].reshape(1, CONV_DIM)
    lp = {
        "norm_mix": norm_mix[l], "norm_ffn": norm_ffn[l],
        "conv_wx": cw[:, :D_SSM], "conv_wb": cw[:, D_SSM:D_SSM + BC_WIDTH], "conv_wc": cw[:, D_SSM + BC_WIDTH:],
        "conv_bx": cb[:, :D_SSM], "conv_bb": cb[:, D_SSM:D_SSM + BC_WIDTH], "conv_bc": cb[:, D_SSM + BC_WIDTH:],
        "dt_bias": jnp.pad(dt_bias[l], (0, LANES - SSM_HEADS)).reshape(1, LANES),
        "a_log": jnp.pad(a_log[l], (0, LANES - SSM_HEADS)).reshape(1, LANES),
        "d_skip_p": jnp.repeat(d_skip[l], SSM_HEAD_DIM).reshape(1, D_SSM),
        "ssm_norm": ssm_norm[l].reshape(1, D_SSM),
        "q_norm": q_norm[l].reshape(1, ATT_HEAD_DIM), "k_norm": k_norm[l].reshape(1, ATT_HEAD_DIM),
        "rel_bias": rel_bias[l],
    }
    return lp


def _row_block(t, row0, rows, col0, width):
    return lax.slice(t, (row0, col0), (row0 + rows, col0 + width))


def _layer(x, l, lp, w, state, tables, prev):
    cache_k, cache_v, state_ret, state_ssm4, hist8 = state
    hn = _rmsnorm(x, lp["norm_mix"])
    tm, tn = DENSE_TILES["in_proj"]
    u_head = _matmul(hn, w["w_in_t"], l, n=HEAD_WIDTH, out_dtype=F32, tm=tm, tn=tn, w_transposed=True,
                     name="in_proj_head")
    dt_raw = _matmul(hn, w["w_in_t"], l, n=LANES, out_dtype=F32, tm=tm, tn=LANES, w_col0=DT_COL0,
                     w_transposed=True, name="in_proj_dt")
    u_tail = _matmul(hn, w["w_in_t"], l, n=TAIL_WIDTH, out_dtype=F32, tm=tm, tn=tn, w_col0=TAIL_COL0,
                     w_transposed=True, name="in_proj_tail")

    ys, ssm_p = _ssd_call(u_head, dt_raw, tables["zeros_hist"], tables["zeros_h"], 0, lp, L=CHUNK, nb=BATCH,
                          nc=SEQ // CHUNK, row0=0, out_layer=l, prev_out=None, prev_state=prev["ssm_p"])
    ys, ssm_s = _ssd_call(u_head, dt_raw, hist8, state_ssm4, l, lp, L=DEC_SEQ, nb=DEC_BATCH, nc=1,
                          row0=M_PROMPT, out_layer=l, prev_out=ys, prev_state=prev["ssm_s"])

    yr, ret_p = _ret_call(u_tail, tables["cos_p"], tables["sin_p"], tables["zeros_s"], 0, L=CHUNK, nb=BATCH,
                          nc=SEQ // CHUNK, row0=0, out_layer=l, prev_out=None, prev_state=prev["ret_p"])
    yr, ret_s = _ret_call(u_tail, tables["cos_s"], tables["sin_s"], state_ret, l, L=DEC_SEQ, nb=DEC_BATCH,
                          nc=1, row0=M_PROMPT, out_layer=l, prev_out=yr, prev_state=prev["ret_s"])

    dm = ATT_HEADS * ATT_HEAD_DIM
    knf_s, knb_s, vb_s = _kv_prep_sample(u_tail, lp["k_norm"])
    hist = cache_k.shape[2]
    assert hist == WINDOW
    tbl = _bias_table(lp["rel_bias"])
    ya, knf_p = _attn_prompt_call(u_tail, tbl[:, :, :WINDOW], tbl[:, :, WINDOW:], lp["q_norm"], lp["k_norm"])
    bias1_s, bias2_s = _head_masked_bias(tbl[:, :DEC_SEQ, :hist], tbl[:, :DEC_SEQ, hist:hist + DEC_SEQ])
    ya = _attn_sample_call(u_tail, cache_k, cache_v, l, knb_s[0], vb_s[0], bias1_s, bias2_s, lp["q_norm"], ya)

    tm, tn = DENSE_TILES["merge"]
    mm = _gated_merge((ys, yr, ya), (w["w_br_ssm"], w["w_br_ret"], w["w_br_att"]), l, u_tail,
                      (T_GA, T_GB, T_GC), tm=tm, tn=tn)
    x1, h2 = _out_proj_norm(mm, w["w_out"], l, x, lp["norm_ffn"], tm=DENSE_TILES["out_proj"][0])

    tm, tn = DENSE_TILES["ffn_in"]
    hh = _swiglu_matmul(h2, w["w_ffn_in"], l, tm=tm, tn=tn)
    tm, tn = DENSE_TILES["ffn_out"]
    x2 = _matmul(hh, w["w_ffn_out"], l, n=D_MODEL, out_dtype=F32, tm=tm, tn=tn, addend=x1,
                 split_out=(l == DEPTH - 1), name="ffn_out")

    keep = min(WINDOW, SEQ)
    nconv = CONV_W - 1

    def conv_rows(row0):
        return jnp.concatenate([_row_block(u_head, row0, nconv, H_X, D_SSM),
                                _row_block(u_head, row0, nconv, H_B, BC_WIDTH),
                                _row_block(u_head, row0, nconv, H_C, BC_WIDTH)], axis=-1)

    pv = jnp.stack([_row_block(u_tail, (b + 1) * SEQ - keep, keep, T_AV, dm) for b in range(BATCH)])
    pconv = jnp.stack([conv_rows((b + 1) * SEQ - nconv) for b in range(BATCH)])
    sv = _row_block(u_tail, M_PROMPT, M_SAMPLE, T_AV, dm)
    s_xbc = jnp.concatenate([_row_block(u_head, M_PROMPT, M_SAMPLE, H_X, D_SSM),
                             _row_block(u_head, M_PROMPT, M_SAMPLE, H_B, BC_WIDTH),
                             _row_block(u_head, M_PROMPT, M_SAMPLE, H_C, BC_WIDTH)], axis=-1)
    sconv = s_xbc.reshape(DEC_BATCH, DEC_SEQ, CONV_DIM)[:, DEC_SEQ - nconv:]

    outs_p = (knf_p.reshape(BATCH, keep, ATT_HEADS, ATT_HEAD_DIM),
              pv.reshape(BATCH, keep, ATT_HEADS, ATT_HEAD_DIM),
              pconv)
    outs_s = (knf_s.reshape(DEC_BATCH, DEC_SEQ, ATT_HEADS, ATT_HEAD_DIM),
              sv.reshape(DEC_BATCH, DEC_SEQ, ATT_HEADS, ATT_HEAD_DIM),
              sconv)
    new_states = {"ssm_p": ssm_p, "ssm_s": ssm_s, "ret_p": ret_p, "ret_s": ret_s}
    return x2, outs_p, outs_s, new_states


def kernel(x_prompt, x_sample, cache_attn_k, cache_attn_v, state_ret, state_ssm, state_conv,
           norm_mix, w_in, conv_w, conv_b, dt_bias, a_log, d_skip, ssm_norm, q_norm, k_norm,
           rel_bias, w_br_ssm, w_br_ret, w_br_att, w_out, norm_ffn, w_ffn_in, w_ffn_out):
    x = (x_prompt.reshape(M_PROMPT, D_MODEL), x_sample.reshape(M_SAMPLE, D_MODEL))
    cos_p, sin_p = _rope_tables(jnp.arange(SEQ))
    cos_s, sin_s = _rope_tables(PAST_LEN + jnp.arange(DEC_SEQ))
    tables = {"cos_p": cos_p, "sin_p": sin_p, "cos_s": cos_s, "sin_s": sin_s,
              "zeros_hist": jnp.zeros((1, BATCH, SUBLANES, CONV_DIM), F32),
              "zeros_h": jnp.zeros((1, BATCH, D_SSM, SSM_STATE), F32),
              "zeros_s": jnp.zeros((1, BATCH, RET_HEADS, RET_DK, RET_DV), F32)}
    w = {"w_in_t": jnp.swapaxes(w_in, 1, 2), "w_br_ssm": w_br_ssm, "w_br_ret": w_br_ret, "w_br_att": w_br_att, "w_out": w_out,
         "w_ffn_in": w_ffn_in, "w_ffn_out": w_ffn_out}
    hist8 = jnp.pad(state_conv, ((0, 0), (0, 0), (SUBLANES - (CONV_W - 1), 0), (0, 0)))
    state = (cache_attn_k, cache_attn_v, state_ret,
             state_ssm.reshape(DEPTH, DEC_BATCH, D_SSM, SSM_STATE), hist8)
    acc_p = ([], [], [])
    acc_s = ([], [], [])
    prev = {"ssm_p": None, "ssm_s": None, "ret_p": None, "ret_s": None}
    for l in range(DEPTH):
        lp = _layer_params(l, norm_mix, w_in, conv_w, conv_b, dt_bias, a_log, d_skip, ssm_norm, q_norm,
                           k_norm, rel_bias, w_br_ssm, w_br_ret, w_br_att, w_out, norm_ffn, w_ffn_in,
                           w_ffn_out)
        x, outs_p, outs_s, prev = _layer(x, l, lp, w, state, tables, prev)
        for i in range(3):
            acc_p[i].append(outs_p[i])
            acc_s[i].append(outs_s[i])
    pk, pv, pconv = [jnp.stack(a) for a in acc_p]
    sk, sv, sconv = [jnp.stack(a) for a in acc_s]
    pr, sr = prev["ret_p"], prev["ret_s"]
    pssm = prev["ssm_p"].reshape(DEPTH, BATCH, SSM_HEADS, SSM_HEAD_DIM, SSM_STATE)
    sssm = prev["ssm_s"].reshape(DEPTH, DEC_BATCH, SSM_HEADS, SSM_HEAD_DIM, SSM_STATE)
    yp = x[0].reshape(BATCH, SEQ, D_MODEL)
    ys = x[1].reshape(DEC_BATCH, DEC_SEQ, D_MODEL)
    return (yp, ys, pk, pv, pr, pssm, pconv, sk, sv, sr, sssm, sconv)
```
